```python
import jax, jax.numpy as jnp
from jax import lax
import numpy as np

D_MODEL = 2048
BATCH = 4
SEQ = 2048
DEPTH = 1

N_HEADS = 16
QK_NOPE_DIM = 128
QK_ROPE_DIM = 64
QK_HEAD_DIM = QK_NOPE_DIM + QK_ROPE_DIM
V_HEAD_DIM = 128
Q_LORA_RANK = 512
KV_LORA_RANK = 256
ATTN_WIDTH = N_HEADS * V_HEAD_DIM
CONV_WIDTH = D_MODEL // 2
CONV_K = 3
PLE_DIM = 256
Q_BLOCK = 128
ROPE_THETA = 10000.0
EPS = 1e-6

SPLIT_SIZES = (Q_LORA_RANK, KV_LORA_RANK, QK_ROPE_DIM, ATTN_WIDTH,
               CONV_WIDTH, CONV_WIDTH, CONV_WIDTH, CONV_WIDTH, D_MODEL, D_MODEL)
IN_WIDTH = sum(SPLIT_SIZES)

kernel_name = "hybrid_mla_shortconv_gated_block"


def rmsnorm(x, g):
    xf = x.astype(jnp.float32)
    xf = xf * lax.rsqrt(jnp.mean(xf * xf, axis=-1, keepdims=True) + EPS)
    return xf.astype(x.dtype) * g


def rope_tables(positions, dtype):
    inv_freq = 1.0 / (ROPE_THETA ** (jnp.arange(0, QK_ROPE_DIM, 2, dtype=jnp.float32) / QK_ROPE_DIM))
    ang = positions.astype(jnp.float32)[..., None] * inv_freq
    return jnp.cos(ang)[:, :, None, :].astype(dtype), jnp.sin(ang)[:, :, None, :].astype(dtype)


def apply_rope(x, cos, sin):
    x1, x2 = jnp.split(x, 2, axis=-1)
    return jnp.concatenate([x1 * cos - x2 * sin, x2 * cos + x1 * sin], axis=-1)


def split_columns(z):
    idx, acc = [], 0
    for s in SPLIT_SIZES[:-1]:
        acc += s
        idx.append(acc)
    return jnp.split(z, idx, axis=-1)


def blocked_attention(q, k, v):
    b, s, h, d = q.shape
    scale = QK_HEAD_DIM ** -0.5
    qb = q.reshape(b, s // Q_BLOCK, Q_BLOCK, h, d).transpose(1, 0, 2, 3, 4)

    def attend(q_blk):
        sc = jnp.einsum('bqhd,bkhd->bhqk', q_blk, k).astype(jnp.float32) * scale
        pr = jax.nn.softmax(sc, axis=-1).astype(v.dtype)
        return jnp.einsum('bhqk,bkhv->bqhv', pr, v)

    out = lax.map(attend, qb)
    return out.transpose(1, 0, 2, 3, 4).reshape(b, s, h * v.shape[-1])


def centred_conv3(u, w, bias):
    up = jnp.pad(u, ((0, 0), (1, 1), (0, 0)))
    return up[:, :-2] * w[0] + up[:, 1:-1] * w[1] + up[:, 2:] * w[2] + bias


def setup_inputs(seed: int = 0) -> dict:
    key = jax.random.key(seed)
    ks = jax.random.split(key, 20)
    f32 = jnp.float32

    def w(k, shape, fan_in):
        return jax.random.normal(k, shape, f32) * (fan_in ** -0.5)

    def gain(k, shape):
        return 1.0 + 0.02 * jax.random.normal(k, shape, f32)

    return {
        "x": jax.random.normal(ks[0], (BATCH, SEQ, D_MODEL), f32),
        "p": jax.random.normal(ks[1], (DEPTH, BATCH, SEQ, PLE_DIM), f32),
        "positions": jnp.broadcast_to(jnp.arange(SEQ, dtype=jnp.int32), (BATCH, SEQ)),
        "norm_g": gain(ks[2], (DEPTH, D_MODEL)),
        "w_in": w(ks[3], (DEPTH, D_MODEL, IN_WIDTH), D_MODEL),
        "q_lat_g": gain(ks[4], (DEPTH, Q_LORA_RANK)),
        "kv_lat_g": gain(ks[5], (DEPTH, KV_LORA_RANK)),
        "w_uq": w(ks[6], (DEPTH, Q_LORA_RANK, N_HEADS * QK_HEAD_DIM), Q_LORA_RANK),
        "w_ukv": w(ks[7], (DEPTH, KV_LORA_RANK, N_HEADS * (QK_NOPE_DIM + V_HEAD_DIM)), KV_LORA_RANK),
        "q_norm_g": gain(ks[8], (DEPTH, QK_HEAD_DIM)),
        "k_norm_g": gain(ks[9], (DEPTH, QK_HEAD_DIM)),
        "conv_w": w(ks[10], (DEPTH, CONV_K, CONV_WIDTH), CONV_K),
        "conv_b": 0.02 * jax.random.normal(ks[11], (DEPTH, CONV_WIDTH), f32),
        "w_branch_attn": w(ks[12], (DEPTH, ATTN_WIDTH, D_MODEL), ATTN_WIDTH),
        "w_branch_conv": w(ks[13], (DEPTH, CONV_WIDTH, D_MODEL), CONV_WIDTH),
        "w_out": w(ks[14], (DEPTH, D_MODEL, D_MODEL), D_MODEL),
        "ple_norm_g": gain(ks[15], (DEPTH, D_MODEL)),
        "w_ple_gate": w(ks[16], (DEPTH, D_MODEL, D_MODEL), D_MODEL),
        "w_ple_proj": w(ks[17], (DEPTH, PLE_DIM, D_MODEL), PLE_DIM),
    }


def reference(x, p, positions, norm_g, w_in, q_lat_g, kv_lat_g, w_uq, w_ukv,
              q_norm_g, k_norm_g, conv_w, conv_b, w_branch_attn, w_branch_conv,
              w_out, ple_norm_g, w_ple_gate, w_ple_proj):
    b, s, _ = x.shape
    cos, sin = rope_tables(positions, x.dtype)
    for i in range(DEPTH):
        h = rmsnorm(x, norm_g[i])
        z = jnp.einsum('bsd,de->bse', h, w_in[i])
        (q_a, kv_a, k_pe, gate_a, conv_b_gate, conv_c_gate, conv_x,
         gate_c, merge_a, merge_c) = split_columns(z)

        q = jnp.einsum('bsr,re->bse', rmsnorm(q_a, q_lat_g[i]), w_uq[i])
        q = q.reshape(b, s, N_HEADS, QK_HEAD_DIM)
        kv = jnp.einsum('bsr,re->bse', rmsnorm(kv_a, kv_lat_g[i]), w_ukv[i])
        kv = kv.reshape(b, s, N_HEADS, QK_NOPE_DIM + V_HEAD_DIM)
        k_nope, v = kv[..., :QK_NOPE_DIM], kv[..., QK_NOPE_DIM:]
        k_pe_h = jnp.broadcast_to(k_pe[:, :, None, :], (b, s, N_HEADS, QK_ROPE_DIM))
        k = jnp.concatenate([k_nope, k_pe_h], axis=-1)
        q = rmsnorm(q, q_norm_g[i])
        k = rmsnorm(k, k_norm_g[i])
        q = jnp.concatenate([q[..., :QK_NOPE_DIM], apply_rope(q[..., QK_NOPE_DIM:], cos, sin)], axis=-1)
        k = jnp.concatenate([k[..., :QK_NOPE_DIM], apply_rope(k[..., QK_NOPE_DIM:], cos, sin)], axis=-1)
        attn = blocked_attention(q, k, v)
        y_attn = jnp.einsum('bse,ed->bsd', attn * jax.nn.silu(gate_a), w_branch_attn[i])

        cv = conv_b_gate * centred_conv3(conv_c_gate * conv_x, conv_w[i], conv_b[i])
        y_conv = jnp.einsum('bse,ed->bsd', cv * jax.nn.silu(gate_c), w_branch_conv[i])

        merged = jax.nn.sigmoid(merge_a) * y_attn + jax.nn.sigmoid(merge_c) * y_conv
        x = x + jnp.einsum('bsd,de->bse', merged, w_out[i])

        gate = jax.nn.sigmoid(jnp.einsum('bsd,de->bse', rmsnorm(x, ple_norm_g[i]), w_ple_gate[i]))
        x = x + gate * jnp.einsum('bsp,pd->bsd', p[i], w_ple_proj[i])
    return x
```

```python
import functools

import jax
import jax.numpy as jnp
from jax import lax
from jax.experimental import pallas as pl
from jax.experimental.pallas import tpu as pltpu

N_HEADS = 16
QK_NOPE_DIM = 128
QK_ROPE_DIM = 64
QK_HEAD_DIM = QK_NOPE_DIM + QK_ROPE_DIM
V_HEAD_DIM = 128
Q_LORA_RANK = 512
KV_LORA_RANK = 256
ROPE_THETA = 10000.0
EPS = 1e-6

LANES = 128
HEAD_PAD = 2 * LANES
SEG = 1024
HALO_ROWS = 16
VMEM_LIMIT = 56 * 1024 * 1024

BF16 = jnp.bfloat16
F32 = jnp.float32


def _rms(v, axis_size):
    return v * lax.rsqrt(jnp.sum(v * v, axis=-1, keepdims=True) * (1.0 / axis_size) + EPS)


def _sigmoid(v):
    return 1.0 / (1.0 + jnp.exp(-v))


def _in_proj_kernel(x_ref, g_ref, w_ref, z_ref, h_ref):
    @pl.when(pl.program_id(1) == 0)
    def _():
        x = x_ref[...]
        h_ref[...] = (_rms(x, x.shape[-1]) * g_ref[...]).astype(BF16)

    z_ref[...] = jnp.dot(h_ref[...], w_ref[...], preferred_element_type=F32).astype(BF16)


def _in_proj(x2, norm_g, w_in_p, tm, tn):
    t, d = x2.shape
    n = w_in_p.shape[1]
    return pl.pallas_call(
        _in_proj_kernel,
        grid=(t // tm, n // tn),
        in_specs=[
            pl.BlockSpec((tm, d), lambda i, j: (i, 0)),
            pl.BlockSpec((1, d), lambda i, j: (0, 0)),
            pl.BlockSpec((d, tn), lambda i, j: (0, j)),
        ],
        out_specs=pl.BlockSpec((tm, tn), lambda i, j: (i, j)),
        out_shape=jax.ShapeDtypeStruct((t, n), BF16),
        scratch_shapes=[pltpu.VMEM((tm, d), BF16)],
        compiler_params=pltpu.CompilerParams(
            dimension_semantics=("arbitrary", "arbitrary"),
            vmem_limit_bytes=VMEM_LIMIT),
        name="in_proj",
    )(x2, norm_g, w_in_p)


def _qkv_kernel(z_ref, pos_ref, freq_ref, qlg_ref, kvlg_ref, wuq_ref, wukv_ref,
                qg_ref, kg_ref, q_ref, k_ref, v_ref):
    qa = z_ref[:, 0:Q_LORA_RANK].astype(F32)
    kva = z_ref[:, Q_LORA_RANK:Q_LORA_RANK + KV_LORA_RANK].astype(F32)
    kpe = z_ref[:, Q_LORA_RANK + KV_LORA_RANK:Q_LORA_RANK + KV_LORA_RANK + LANES].astype(F32)

    qn = (_rms(qa, Q_LORA_RANK) * qlg_ref[...]).astype(BF16)
    kvn = (_rms(kva, KV_LORA_RANK) * kvlg_ref[...]).astype(BF16)
    q = jnp.dot(qn, wuq_ref[...], preferred_element_type=F32)
    kv = jnp.dot(kvn, wukv_ref[...], preferred_element_type=F32)

    ang = pos_ref[...].astype(F32) * freq_ref[...]
    cosv = jnp.cos(ang)
    sinv = jnp.sin(ang)
    lane = lax.broadcasted_iota(jnp.int32, ang.shape, 1)
    half = QK_ROPE_DIM // 2
    sin_lo = jnp.where(lane < half, -sinv, 0.0)
    sin_hi = jnp.where((lane >= half) & (lane < QK_ROPE_DIM), sinv, 0.0)

    def rope(u):
        return (u * cosv + pltpu.roll(u, LANES - half, axis=1) * sin_lo
                + pltpu.roll(u, half, axis=1) * sin_hi)

    qg_nope, qg_rope = qg_ref[:, 0:LANES], qg_ref[:, LANES:HEAD_PAD]
    kg_nope, kg_rope = kg_ref[:, 0:LANES], kg_ref[:, LANES:HEAD_PAD]
    k_rope = rope(kpe * kg_rope)
    kpe_ss = jnp.sum(kpe * kpe, axis=-1, keepdims=True)
    scale = QK_HEAD_DIM ** -0.5
    inv_d = 1.0 / QK_HEAD_DIM

    for h in range(N_HEADS):
        c0 = h * HEAD_PAD
        q_nope = q[:, c0:c0 + LANES]
        q_rope = q[:, c0 + LANES:c0 + HEAD_PAD]
        ss = jnp.sum(q_nope * q_nope + q_rope * q_rope, axis=-1, keepdims=True)
        rq = lax.rsqrt(ss * inv_d + EPS) * scale
        q_ref[0, h, :, 0:LANES] = (q_nope * rq * qg_nope).astype(BF16)
        q_ref[0, h, :, LANES:HEAD_PAD] = (rope(q_rope * qg_rope) * rq).astype(BF16)

        k_nope = kv[:, c0:c0 + LANES]
        ssk = jnp.sum(k_nope * k_nope, axis=-1, keepdims=True) + kpe_ss
        rk = lax.rsqrt(ssk * inv_d + EPS)
        k_ref[0, h, :, 0:LANES] = (k_nope * rk * kg_nope).astype(BF16)
        k_ref[0, h, :, LANES:HEAD_PAD] = (k_rope * rk).astype(BF16)
        v_ref[0, h, :, :] = kv[:, c0 + LANES:c0 + HEAD_PAD].astype(BF16)


def _qkv_prep(z, pos, freq, q_lat_g, kv_lat_g, wuq_p, wukv, qg_p, kg_p, b, s, tm):
    t = z.shape[0]
    spb = s // tm
    full = lambda shape: pl.BlockSpec(shape, lambda i: (0,) * len(shape))
    head_out = lambda w: pl.BlockSpec((1, N_HEADS, tm, w), lambda i: (i // spb, 0, i % spb, 0))
    return pl.pallas_call(
        _qkv_kernel,
        grid=(t // tm,),
        in_specs=[
            pl.BlockSpec((tm, SEG), lambda i: (i, 0)),
            pl.BlockSpec((tm, 1), lambda i: (i, 0)),
            full((1, LANES)),
            full((1, Q_LORA_RANK)),
            full((1, KV_LORA_RANK)),
            full(wuq_p.shape),
            full(wukv.shape),
            full((1, HEAD_PAD)),
            full((1, HEAD_PAD)),
        ],
        out_specs=[head_out(HEAD_PAD), head_out(HEAD_PAD), head_out(V_HEAD_DIM)],
        out_shape=[
            jax.ShapeDtypeStruct((b, N_HEADS, s, HEAD_PAD), BF16),
            jax.ShapeDtypeStruct((b, N_HEADS, s, HEAD_PAD), BF16),
            jax.ShapeDtypeStruct((b, N_HEADS, s, V_HEAD_DIM), BF16),
        ],
        compiler_params=pltpu.CompilerParams(
            dimension_semantics=("arbitrary",), vmem_limit_bytes=VMEM_LIMIT),
        name="qkv_prep",
    )(z, pos, freq, q_lat_g, kv_lat_g, wuq_p, wukv, qg_p, kg_p)


def _attn_kernel(q_ref, k_ref, v_ref, o_ref, *, tq):
    k = k_ref[0, 0]
    v = v_ref[0, 0]
    s_len = k.shape[0]

    def body(i, carry):
        r0 = pl.multiple_of(i * tq, tq)
        q = q_ref[0, 0, pl.ds(r0, tq), :]
        sc = lax.dot_general(q, k, (((1,), (1,)), ((), ())), preferred_element_type=F32)
        m = jnp.max(sc, axis=-1, keepdims=True)
        p = jnp.exp(sc - m)
        l = jnp.sum(p, axis=-1, keepdims=True)
        o = jnp.dot(p.astype(BF16), v, preferred_element_type=F32)
        o_ref[pl.ds(r0, tq), :] = (o / l).astype(BF16)
        return carry

    lax.fori_loop(0, s_len // tq, body, 0)


def _attention(q, k, v, tq):
    b, h, s, _ = q.shape
    return pl.pallas_call(
        functools.partial(_attn_kernel, tq=tq),
        grid=(b, h),
        in_specs=[
            pl.BlockSpec((1, 1, s, HEAD_PAD), lambda bi, hi: (bi, hi, 0, 0)),
            pl.BlockSpec((1, 1, s, HEAD_PAD), lambda bi, hi: (bi, hi, 0, 0)),
            pl.BlockSpec((1, 1, s, V_HEAD_DIM), lambda bi, hi: (bi, hi, 0, 0)),
        ],
        out_specs=pl.BlockSpec((s, V_HEAD_DIM), lambda bi, hi: (bi, hi)),
        out_shape=jax.ShapeDtypeStruct((b * s, h * V_HEAD_DIM), BF16),
        compiler_params=pltpu.CompilerParams(
            dimension_semantics=("arbitrary", "arbitrary"), vmem_limit_bytes=VMEM_LIMIT),
        name="attention",
    )(q, k, v)


def _mix_kernel(x_ref, attn_ref, ga0_ref, ga1_ref, cb_ref, cc_ref, cx_ref, gc_ref,
                ma0_ref, ma1_ref, mc0_ref, mc1_ref, ccp_ref, cxp_ref, ccn_ref, cxn_ref,
                cw_ref, cbias_ref, wba_ref, wbc_ref, wout_ref, o_ref, u_ref, *, tm, seq):
    i = pl.program_id(0)

    ga = jnp.concatenate([ga0_ref[...], ga1_ref[...]], axis=1).astype(F32)
    a = (attn_ref[...].astype(F32) * (ga * _sigmoid(ga))).astype(BF16)
    y_attn = jnp.dot(a, wba_ref[...], preferred_element_type=F32)

    u = cc_ref[...].astype(F32) * cx_ref[...].astype(F32)
    first = (i * tm) % seq == 0
    last = ((i + 1) * tm) % seq == 0
    prev = (ccp_ref[HALO_ROWS - 1:HALO_ROWS, :].astype(F32)
            * cxp_ref[HALO_ROWS - 1:HALO_ROWS, :].astype(F32))
    nxt = ccn_ref[0:1, :].astype(F32) * cxn_ref[0:1, :].astype(F32)
    u_ref[7:8, :] = jnp.where(first, 0.0, prev)
    u_ref[8:8 + tm, :] = u
    u_ref[8 + tm:9 + tm, :] = jnp.where(last, 0.0, nxt)
    conv = (u_ref[7:7 + tm, :] * cw_ref[0:1, :] + u * cw_ref[1:2, :]
            + u_ref[9:9 + tm, :] * cw_ref[2:3, :] + cbias_ref[...])
    gc = gc_ref[...].astype(F32)
    cv = (cb_ref[...].astype(F32) * conv * (gc * _sigmoid(gc))).astype(BF16)
    y_conv = jnp.dot(cv, wbc_ref[...], preferred_element_type=F32)

    ma = jnp.concatenate([ma0_ref[...], ma1_ref[...]], axis=1).astype(F32)
    mc = jnp.concatenate([mc0_ref[...], mc1_ref[...]], axis=1).astype(F32)
    merged = (_sigmoid(ma) * y_attn + _sigmoid(mc) * y_conv).astype(BF16)
    o_ref[...] = x_ref[...] + jnp.dot(merged, wout_ref[...], preferred_element_type=F32)


def _mix(x2, attn, z, conv_w, conv_b, wba, wbc, wout, seq, tm):
    t, d = x2.shape
    hb = tm // HALO_ROWS
    n_halo = t // HALO_ROWS
    seg = lambda c: pl.BlockSpec((tm, SEG), lambda i, c=c: (i, c))
    prev = lambda c: pl.BlockSpec((HALO_ROWS, SEG), lambda i, c=c: (jnp.maximum(i * hb - 1, 0), c))
    nxt = lambda c: pl.BlockSpec((HALO_ROWS, SEG),
                                 lambda i, c=c: (jnp.minimum((i + 1) * hb, n_halo - 1), c))
    resident = lambda shape: pl.BlockSpec(shape, lambda i: (0,) * len(shape),
                                          pipeline_mode=pl.Buffered(1))
    return pl.pallas_call(
        functools.partial(_mix_kernel, tm=tm, seq=seq),
        grid=(t // tm,),
        in_specs=[
            pl.BlockSpec((tm, d), lambda i: (i, 0)),
            pl.BlockSpec((tm, attn.shape[1]), lambda i: (i, 0)),
            seg(1), seg(2), seg(3), seg(4), seg(5), seg(6), seg(7), seg(8), seg(9), seg(10),
            prev(4), prev(5), nxt(4), nxt(5),
            resident(conv_w.shape), resident(conv_b.shape),
            resident(wba.shape), resident(wbc.shape), resident(wout.shape),
        ],
        out_specs=pl.BlockSpec((tm, d), lambda i: (i, 0)),
        out_shape=jax.ShapeDtypeStruct((t, d), F32),
        scratch_shapes=[pltpu.VMEM((tm + 16, SEG), F32)],
        compiler_params=pltpu.CompilerParams(
            dimension_semantics=("arbitrary",), vmem_limit_bytes=VMEM_LIMIT),
        name="mix",
    )(x2, attn, z, z, z, z, z, z, z, z, z, z, z, z, z, z, conv_w, conv_b, wba, wbc, wout)


def _ple_kernel(x_ref, p_ref, g_ref, wg_ref, wp_ref, o_ref):
    x = x_ref[...]
    h = (_rms(x, x.shape[-1]) * g_ref[...]).astype(BF16)
    gate = _sigmoid(jnp.dot(h, wg_ref[...], preferred_element_type=F32))
    proj = jnp.dot(p_ref[...].astype(BF16), wp_ref[...], preferred_element_type=F32)
    o_ref[...] = x + gate * proj


def _ple(x2, p2, ple_g, wg, wp, tm):
    t, d = x2.shape
    resident = lambda shape: pl.BlockSpec(shape, lambda i: (0,) * len(shape),
                                          pipeline_mode=pl.Buffered(1))
    return pl.pallas_call(
        _ple_kernel,
        grid=(t // tm,),
        in_specs=[
            pl.BlockSpec((tm, d), lambda i: (i, 0)),
            pl.BlockSpec((tm, p2.shape[1]), lambda i: (i, 0)),
            resident((1, d)), resident(wg.shape), resident(wp.shape),
        ],
        out_specs=pl.BlockSpec((tm, d), lambda i: (i, 0)),
        out_shape=jax.ShapeDtypeStruct((t, d), F32),
        compiler_params=pltpu.CompilerParams(
            dimension_semantics=("arbitrary",), vmem_limit_bytes=VMEM_LIMIT),
        name="ple",
    )(x2, p2, ple_g, wg, wp)


def _layer(x2, p2, pos, freq, b, s, norm_g, w_in, q_lat_g, kv_lat_g, w_uq, w_ukv,
           q_norm_g, k_norm_g, conv_w, conv_b, w_branch_attn, w_branch_conv,
           w_out, ple_norm_g, w_ple_gate, w_ple_proj):
    d = x2.shape[1]
    head_cols = Q_LORA_RANK + KV_LORA_RANK + QK_ROPE_DIM
    w_in_p = jnp.concatenate(
        [w_in[:, :head_cols], jnp.zeros((d, SEG - head_cols), w_in.dtype), w_in[:, head_cols:]],
        axis=1).astype(BF16)
    wuq_p = jnp.pad(w_uq.reshape(Q_LORA_RANK, N_HEADS, QK_HEAD_DIM),
                    ((0, 0), (0, 0), (0, HEAD_PAD - QK_HEAD_DIM))
                    ).reshape(Q_LORA_RANK, N_HEADS * HEAD_PAD).astype(BF16)
    pad_g = lambda g: jnp.pad(g, (0, HEAD_PAD - QK_HEAD_DIM)).reshape(1, HEAD_PAD)

    z = _in_proj(x2, norm_g.reshape(1, d), w_in_p, tm=1024, tn=512)
    q, k, v = _qkv_prep(z, pos, freq, q_lat_g.reshape(1, -1), kv_lat_g.reshape(1, -1),
                        wuq_p, w_ukv.astype(BF16), pad_g(q_norm_g), pad_g(k_norm_g), b, s, tm=256)
    attn = _attention(q, k, v, tq=512)
    x2 = _mix(x2, attn, z, conv_w, conv_b.reshape(1, -1), w_branch_attn.astype(BF16),
              w_branch_conv.astype(BF16), w_out.astype(BF16), seq=s, tm=256)
    return _ple(x2, p2, ple_norm_g.reshape(1, d), w_ple_gate.astype(BF16),
                w_ple_proj.astype(BF16), tm=512)


def kernel(x, p, positions, norm_g, w_in, q_lat_g, kv_lat_g, w_uq, w_ukv, q_norm_g, k_norm_g, conv_w, conv_b, w_branch_attn, w_branch_conv, w_out, ple_norm_g, w_ple_gate, w_ple_proj):
    b, s, d = x.shape
    depth = p.shape[0]
    x2 = x.reshape(b * s, d)
    pos = positions.reshape(b * s, 1)
    half = QK_ROPE_DIM // 2
    inv_freq = 1.0 / (ROPE_THETA ** (jnp.arange(0, QK_ROPE_DIM, 2, dtype=F32) / QK_ROPE_DIM))
    freq = jnp.concatenate([inv_freq, inv_freq, jnp.zeros((LANES - 2 * half,), F32)]).reshape(1, LANES)
    for i in range(depth):
        x2 = _layer(x2, p[i].reshape(b * s, -1), pos, freq, b, s, norm_g[i], w_in[i],
                    q_lat_g[i], kv_lat_g[i], w_uq[i], w_ukv[i], q_norm_g[i], k_norm_g[i],
                    conv_w[i], conv_b[i], w_branch_attn[i], w_branch_conv[i], w_out[i],
                    ple_norm_g[i], w_ple_gate[i], w_ple_proj[i])
    return x2.reshape(b, s, d)
```

```python
import functools

import jax
import jax.numpy as jnp
from jax import lax
from jax.experimental import pallas as pl
from jax.experimental.pallas import tpu as pltpu

N_HEADS = 16
QK_NOPE_DIM = 128
QK_ROPE_DIM = 64
QK_HEAD_DIM = QK_NOPE_DIM + QK_ROPE_DIM
V_HEAD_DIM = 128
Q_LORA_RANK = 512
KV_LORA_RANK = 256
ROPE_THETA = 10000.0
EPS = 1e-6
LOG2E = 1.4426950408889634

LANES = 128
HEAD_PAD = 2 * LANES
SEG = 1024
HALO_ROWS = 16
VMEM_LIMIT = 56 * 1024 * 1024

BF16 = jnp.bfloat16
F32 = jnp.float32


def _rms(v, axis_size):
    return v * lax.rsqrt(jnp.sum(v * v, axis=-1, keepdims=True) * (1.0 / axis_size) + EPS)


def _sigmoid(v):
    return 1.0 / (1.0 + jnp.exp(-v))


def _in_proj_kernel(x_ref, g_ref, wh_ref, wr_ref, z_ref, h_ref, *, n_head_tiles):
    j = pl.program_id(1)

    @pl.when(j == 0)
    def _():
        x = x_ref[...]
        h_ref[...] = (_rms(x, x.shape[-1]) * g_ref[...]).astype(BF16)

    @pl.when(j < n_head_tiles)
    def _():
        z_ref[...] = jnp.dot(h_ref[...], wh_ref[...], preferred_element_type=F32).astype(BF16)

    @pl.when(j >= n_head_tiles)
    def _():
        z_ref[...] = jnp.dot(h_ref[...], wr_ref[...], preferred_element_type=F32).astype(BF16)


def _in_proj(x2, norm_g, w_head, w_rest, tm, tn):
    t, d = x2.shape
    nh = w_head.shape[1] // tn
    n = w_head.shape[1] + w_rest.shape[1]
    return pl.pallas_call(
        functools.partial(_in_proj_kernel, n_head_tiles=nh),
        grid=(t // tm, n // tn),
        in_specs=[
            pl.BlockSpec((tm, d), lambda i, j: (i, 0)),
            pl.BlockSpec((1, d), lambda i, j: (0, 0)),
            pl.BlockSpec((d, tn), lambda i, j: (0, jnp.minimum(j, nh - 1))),
            pl.BlockSpec((d, tn), lambda i, j: (0, jnp.maximum(j - nh, 0))),
        ],
        out_specs=pl.BlockSpec((tm, tn), lambda i, j: (i, j)),
        out_shape=jax.ShapeDtypeStruct((t, n), BF16),
        scratch_shapes=[pltpu.VMEM((tm, d), BF16)],
        compiler_params=pltpu.CompilerParams(
            dimension_semantics=("arbitrary", "arbitrary"),
            vmem_limit_bytes=VMEM_LIMIT),
        name="in_proj",
    )(x2, norm_g, w_head, w_rest)


def _qkv_kernel(z_ref, pos_ref, freq_ref, qlg_ref, kvlg_ref, wuq_ref, wukv_ref,
                qg_ref, kg_ref, q_ref, k_ref, v_ref):
    qa = z_ref[:, 0:Q_LORA_RANK].astype(F32)
    kva = z_ref[:, Q_LORA_RANK:Q_LORA_RANK + KV_LORA_RANK].astype(F32)
    kpe = z_ref[:, Q_LORA_RANK + KV_LORA_RANK:Q_LORA_RANK + KV_LORA_RANK + LANES].astype(F32)

    qn = (_rms(qa, Q_LORA_RANK) * qlg_ref[...]).astype(BF16)
    kvn = (_rms(kva, KV_LORA_RANK) * kvlg_ref[...]).astype(BF16)
    q = jnp.dot(qn, wuq_ref[...], preferred_element_type=F32)
    kv = jnp.dot(kvn, wukv_ref[...], preferred_element_type=F32)

    ang = pos_ref[...].astype(F32) * freq_ref[...]
    cosv = jnp.cos(ang)
    sinv = jnp.sin(ang)
    lane = lax.broadcasted_iota(jnp.int32, ang.shape, 1)
    half = QK_ROPE_DIM // 2
    sin_lo = jnp.where(lane < half, -sinv, 0.0)
    sin_hi = jnp.where((lane >= half) & (lane < QK_ROPE_DIM), sinv, 0.0)

    def rope(u):
        return (u * cosv + pltpu.roll(u, LANES - half, axis=1) * sin_lo
                + pltpu.roll(u, half, axis=1) * sin_hi)

    qg_nope, qg_rope = qg_ref[:, 0:LANES], qg_ref[:, LANES:HEAD_PAD]
    kg_nope, kg_rope = kg_ref[:, 0:LANES], kg_ref[:, LANES:HEAD_PAD]
    k_rope = rope(kpe * kg_rope)
    kpe_ss = jnp.sum(kpe * kpe, axis=-1, keepdims=True)
    scale = QK_HEAD_DIM ** -0.5 * LOG2E
    inv_d = 1.0 / QK_HEAD_DIM

    for h in range(N_HEADS):
        c0 = h * HEAD_PAD
        q_nope = q[:, c0:c0 + LANES]
        q_rope = q[:, c0 + LANES:c0 + HEAD_PAD]
        ss = jnp.sum(q_nope * q_nope + q_rope * q_rope, axis=-1, keepdims=True)
        rq = lax.rsqrt(ss * inv_d + EPS) * scale
        q_ref[0, h, :, 0:LANES] = (q_nope * rq * qg_nope).astype(BF16)
        q_ref[0, h, :, LANES:HEAD_PAD] = (rope(q_rope * qg_rope) * rq).astype(BF16)

        k_nope = kv[:, c0:c0 + LANES]
        ssk = jnp.sum(k_nope * k_nope, axis=-1, keepdims=True) + kpe_ss
        rk = lax.rsqrt(ssk * inv_d + EPS)
        k_ref[0, h, :, 0:LANES] = (k_nope * rk * kg_nope).astype(BF16)
        k_ref[0, h, :, LANES:HEAD_PAD] = (k_rope * rk).astype(BF16)
        v_ref[0, h, :, :] = kv[:, c0 + LANES:c0 + HEAD_PAD].astype(BF16)


def _qkv_prep(z, pos, freq, q_lat_g, kv_lat_g, wuq_p, wukv, qg_p, kg_p, b, s, tm):
    t = z.shape[0]
    spb = s // tm
    full = lambda shape: pl.BlockSpec(shape, lambda i: (0,) * len(shape))
    head_out = lambda w: pl.BlockSpec((1, N_HEADS, tm, w), lambda i: (i // spb, 0, i % spb, 0))
    return pl.pallas_call(
        _qkv_kernel,
        grid=(t // tm,),
        in_specs=[
            pl.BlockSpec((tm, SEG), lambda i: (i, 0)),
            pl.BlockSpec((tm, 1), lambda i: (i, 0)),
            full((1, LANES)),
            full((1, Q_LORA_RANK)),
            full((1, KV_LORA_RANK)),
            full(wuq_p.shape),
            full(wukv.shape),
            full((1, HEAD_PAD)),
            full((1, HEAD_PAD)),
        ],
        out_specs=[head_out(HEAD_PAD), head_out(HEAD_PAD), head_out(V_HEAD_DIM)],
        out_shape=[
            jax.ShapeDtypeStruct((b, N_HEADS, s, HEAD_PAD), BF16),
            jax.ShapeDtypeStruct((b, N_HEADS, s, HEAD_PAD), BF16),
            jax.ShapeDtypeStruct((b, N_HEADS, s, V_HEAD_DIM), BF16),
        ],
        compiler_params=pltpu.CompilerParams(
            dimension_semantics=("arbitrary",), vmem_limit_bytes=VMEM_LIMIT),
        name="qkv_prep",
    )(z, pos, freq, q_lat_g, kv_lat_g, wuq_p, wukv, qg_p, kg_p)


def _attn_kernel(q_ref, k_ref, v_ref, o_ref, *, tq):
    k = k_ref[0, 0]
    v = v_ref[0, 0]
    s_len = k.shape[0]
    lane = lax.broadcasted_iota(jnp.int32, v.shape, 1)
    v_aug = jnp.concatenate([v, jnp.where(lane == 0, 1.0, 0.0).astype(BF16)], axis=1)

    for i in range(s_len // tq):
        q = q_ref[0, 0, i * tq:(i + 1) * tq, :]
        sc = lax.dot_general(q, k, (((1,), (1,)), ((), ())), preferred_element_type=F32)
        m = jnp.max(sc, axis=-1, keepdims=True)
        p = jnp.exp2(sc - m).astype(BF16)
        o = jnp.dot(p, v_aug, preferred_element_type=F32)
        o_ref[i * tq:(i + 1) * tq, :] = (o[:, :V_HEAD_DIM] / o[:, V_HEAD_DIM:V_HEAD_DIM + 1]
                                        ).astype(BF16)


def _attention(q, k, v, tq):
    b, h, s, _ = q.shape
    return pl.pallas_call(
        functools.partial(_attn_kernel, tq=tq),
        grid=(b, h),
        in_specs=[
            pl.BlockSpec((1, 1, s, HEAD_PAD), lambda bi, hi: (bi, hi, 0, 0)),
            pl.BlockSpec((1, 1, s, HEAD_PAD), lambda bi, hi: (bi, hi, 0, 0)),
            pl.BlockSpec((1, 1, s, V_HEAD_DIM), lambda bi, hi: (bi, hi, 0, 0)),
        ],
        out_specs=pl.BlockSpec((s, V_HEAD_DIM), lambda bi, hi: (bi, hi)),
        out_shape=jax.ShapeDtypeStruct((b * s, h * V_HEAD_DIM), BF16),
        compiler_params=pltpu.CompilerParams(
            dimension_semantics=("arbitrary", "arbitrary"), vmem_limit_bytes=VMEM_LIMIT),
        name="attention",
    )(q, k, v)


def _mix_kernel(x_ref, attn_ref, ga0_ref, ga1_ref, cb_ref, cc_ref, cx_ref, gc_ref,
                ma0_ref, ma1_ref, mc0_ref, mc1_ref, ccp_ref, cxp_ref, ccn_ref, cxn_ref,
                cw_ref, cbias_ref, wba_ref, wbc_ref, wout_ref, o_ref, u_ref, *, tm, seq):
    i = pl.program_id(0)

    ga = jnp.concatenate([ga0_ref[...], ga1_ref[...]], axis=1).astype(F32)
    a = (attn_ref[...].astype(F32) * (ga * _sigmoid(ga))).astype(BF16)
    y_attn = jnp.dot(a, wba_ref[...], preferred_element_type=F32)

    u = cc_ref[...].astype(F32) * cx_ref[...].astype(F32)
    first = (i * tm) % seq == 0
    last = ((i + 1) * tm) % seq == 0
    prev = (ccp_ref[HALO_ROWS - 1:HALO_ROWS, :].astype(F32)
            * cxp_ref[HALO_ROWS - 1:HALO_ROWS, :].astype(F32))
    nxt = ccn_ref[0:1, :].astype(F32) * cxn_ref[0:1, :].astype(F32)
    u_ref[7:8, :] = jnp.where(first, 0.0, prev)
    u_ref[8:8 + tm, :] = u
    u_ref[8 + tm:9 + tm, :] = jnp.where(last, 0.0, nxt)
    conv = (u_ref[7:7 + tm, :] * cw_ref[0:1, :] + u * cw_ref[1:2, :]
            + u_ref[9:9 + tm, :] * cw_ref[2:3, :] + cbias_ref[...])
    gc = gc_ref[...].astype(F32)
    cv = (cb_ref[...].astype(F32) * conv * (gc * _sigmoid(gc))).astype(BF16)
    y_conv = jnp.dot(cv, wbc_ref[...], preferred_element_type=F32)

    ma = jnp.concatenate([ma0_ref[...], ma1_ref[...]], axis=1).astype(F32)
    mc = jnp.concatenate([mc0_ref[...], mc1_ref[...]], axis=1).astype(F32)
    merged = (_sigmoid(ma) * y_attn + _sigmoid(mc) * y_conv).astype(BF16)
    o_ref[...] = x_ref[...] + jnp.dot(merged, wout_ref[...], preferred_element_type=F32)


def _mix(x2, attn, z, conv_w, conv_b, wba, wbc, wout, seq, tm):
    t, d = x2.shape
    hb = tm // HALO_ROWS
    n_halo = t // HALO_ROWS
    seg = lambda c: pl.BlockSpec((tm, SEG), lambda i, c=c: (i, c))
    prev = lambda c: pl.BlockSpec((HALO_ROWS, SEG), lambda i, c=c: (jnp.maximum(i * hb - 1, 0), c))
    nxt = lambda c: pl.BlockSpec((HALO_ROWS, SEG),
                                 lambda i, c=c: (jnp.minimum((i + 1) * hb, n_halo - 1), c))
    resident = lambda shape: pl.BlockSpec(shape, lambda i: (0,) * len(shape),
                                          pipeline_mode=pl.Buffered(1))
    return pl.pallas_call(
        functools.partial(_mix_kernel, tm=tm, seq=seq),
        grid=(t // tm,),
        in_specs=[
            pl.BlockSpec((tm, d), lambda i: (i, 0)),
            pl.BlockSpec((tm, attn.shape[1]), lambda i: (i, 0)),
            seg(1), seg(2), seg(3), seg(4), seg(5), seg(6), seg(7), seg(8), seg(9), seg(10),
            prev(4), prev(5), nxt(4), nxt(5),
            resident(conv_w.shape), resident(conv_b.shape),
            resident(wba.shape), resident(wbc.shape), resident(wout.shape),
        ],
        out_specs=pl.BlockSpec((tm, d), lambda i: (i, 0)),
        out_shape=jax.ShapeDtypeStruct((t, d), F32),
        scratch_shapes=[pltpu.VMEM((tm + 16, SEG), F32)],
        compiler_params=pltpu.CompilerParams(
            dimension_semantics=("arbitrary",), vmem_limit_bytes=VMEM_LIMIT),
        name="mix",
    )(x2, attn, z, z, z, z, z, z, z, z, z, z, z, z, z, z, conv_w, conv_b, wba, wbc, wout)


def _ple_kernel(x_ref, p_ref, g_ref, wg_ref, wp_ref, o_ref):
    x = x_ref[...]
    h = (_rms(x, x.shape[-1]) * g_ref[...]).astype(BF16)
    gate = _sigmoid(jnp.dot(h, wg_ref[...], preferred_element_type=F32))
    proj = jnp.dot(p_ref[...].astype(BF16), wp_ref[...], preferred_element_type=F32)
    o_ref[...] = x + gate * proj


def _ple(x2, p2, ple_g, wg, wp, tm):
    t, d = x2.shape
    resident = lambda shape: pl.BlockSpec(shape, lambda i: (0,) * len(shape),
                                          pipeline_mode=pl.Buffered(1))
    return pl.pallas_call(
        _ple_kernel,
        grid=(t // tm,),
        in_specs=[
            pl.BlockSpec((tm, d), lambda i: (i, 0)),
            pl.BlockSpec((tm, p2.shape[1]), lambda i: (i, 0)),
            resident((1, d)), resident(wg.shape), resident(wp.shape),
        ],
        out_specs=pl.BlockSpec((tm, d), lambda i: (i, 0)),
        out_shape=jax.ShapeDtypeStruct((t, d), F32),
        compiler_params=pltpu.CompilerParams(
            dimension_semantics=("arbitrary",), vmem_limit_bytes=VMEM_LIMIT),
        name="ple",
    )(x2, p2, ple_g, wg, wp)


def _layer(x2, p2, pos, freq, b, s, norm_g, w_in, q_lat_g, kv_lat_g, w_uq, w_ukv,
           q_norm_g, k_norm_g, conv_w, conv_b, w_branch_attn, w_branch_conv,
           w_out, ple_norm_g, w_ple_gate, w_ple_proj):
    d = x2.shape[1]
    head_cols = Q_LORA_RANK + KV_LORA_RANK + QK_ROPE_DIM
    w_head = jnp.pad(w_in[:, :head_cols].astype(BF16), ((0, 0), (0, SEG - head_cols)))
    w_rest = w_in[:, head_cols:].astype(BF16)
    wuq_p = jnp.pad(w_uq.reshape(Q_LORA_RANK, N_HEADS, QK_HEAD_DIM),
                    ((0, 0), (0, 0), (0, HEAD_PAD - QK_HEAD_DIM))
                    ).reshape(Q_LORA_RANK, N_HEADS * HEAD_PAD).astype(BF16)
    pad_g = lambda g: jnp.pad(g, (0, HEAD_PAD - QK_HEAD_DIM)).reshape(1, HEAD_PAD)

    z = _in_proj(x2, norm_g.reshape(1, d), w_head, w_rest, tm=1024, tn=512)
    q, k, v = _qkv_prep(z, pos, freq, q_lat_g.reshape(1, -1), kv_lat_g.reshape(1, -1),
                        wuq_p, w_ukv.astype(BF16), pad_g(q_norm_g), pad_g(k_norm_g), b, s, tm=256)
    attn = _attention(q, k, v, tq=512)
    x2 = _mix(x2, attn, z, conv_w, conv_b.reshape(1, -1), w_branch_attn.astype(BF16),
              w_branch_conv.astype(BF16), w_out.astype(BF16), seq=s, tm=256)
    return _ple(x2, p2, ple_norm_g.reshape(1, d), w_ple_gate.astype(BF16),
                w_ple_proj.astype(BF16), tm=512)


def kernel(x, p, positions, norm_g, w_in, q_lat_g, kv_lat_g, w_uq, w_ukv, q_norm_g, k_norm_g, conv_w, conv_b, w_branch_attn, w_branch_conv, w_out, ple_norm_g, w_ple_gate, w_ple_proj):
    b, s, d = x.shape
    depth = p.shape[0]
    x2 = x.reshape(b * s, d)
    pos = positions.reshape(b * s, 1)
    half = QK_ROPE_DIM // 2
    inv_freq = 1.0 / (ROPE_THETA ** (jnp.arange(0, QK_ROPE_DIM, 2, dtype=F32) / QK_ROPE_DIM))
    freq = jnp.concatenate([inv_freq, inv_freq, jnp.zeros((LANES - 2 * half,), F32)]).reshape(1, LANES)
    for i in range(depth):
        x2 = _layer(x2, p[i].reshape(b * s, -1), pos, freq, b, s, norm_g[i], w_in[i],
                    q_lat_g[i], kv_lat_g[i], w_uq[i], w_ukv[i], q_norm_g[i], k_norm_g[i],
                    conv_w[i], conv_b[i], w_branch_attn[i], w_branch_conv[i], w_out[i],
                    ple_norm_g[i], w_ple_gate[i], w_ple_proj[i])
    return x2.reshape(b, s, d)
```

```python
import functools

import jax
import jax.numpy as jnp
from jax import lax
from jax.experimental import pallas as pl
from jax.experimental.pallas import tpu as pltpu

N_HEADS = 16
QK_NOPE_DIM = 128
QK_ROPE_DIM = 64
QK_HEAD_DIM = QK_NOPE_DIM + QK_ROPE_DIM
V_HEAD_DIM = 128
Q_LORA_RANK = 512
KV_LORA_RANK = 256
ROPE_THETA = 10000.0
EPS = 1e-6
LOG2E = 1.4426950408889634

LANES = 128
HEAD_PAD = 2 * LANES
SEG = 1024
HALO_ROWS = 16
VMEM_LIMIT = 56 * 1024 * 1024

BF16 = jnp.bfloat16
F32 = jnp.float32


def _rms(v, axis_size):
    return v * lax.rsqrt(jnp.sum(v * v, axis=-1, keepdims=True) * (1.0 / axis_size) + EPS)


def _sigmoid(v):
    return 1.0 / (1.0 + jnp.exp(-v))


def _in_proj_kernel(x_ref, g_ref, wh_ref, wr_ref, z_ref, h_ref, *, n_head_tiles):
    j = pl.program_id(1)

    @pl.when(j == 0)
    def _():
        x = x_ref[...]
        h_ref[...] = (_rms(x, x.shape[-1]) * g_ref[...]).astype(BF16)

    @pl.when(j < n_head_tiles)
    def _():
        z_ref[...] = jnp.dot(h_ref[...], wh_ref[...], preferred_element_type=F32).astype(BF16)

    @pl.when(j >= n_head_tiles)
    def _():
        z_ref[...] = jnp.dot(h_ref[...], wr_ref[...], preferred_element_type=F32).astype(BF16)


def _in_proj(x2, norm_g, w_head, w_rest, tm, tn):
    t, d = x2.shape
    nh = w_head.shape[1] // tn
    n = w_head.shape[1] + w_rest.shape[1]
    return pl.pallas_call(
        functools.partial(_in_proj_kernel, n_head_tiles=nh),
        grid=(t // tm, n // tn),
        in_specs=[
            pl.BlockSpec((tm, d), lambda i, j: (i, 0)),
            pl.BlockSpec((1, d), lambda i, j: (0, 0)),
            pl.BlockSpec((d, tn), lambda i, j: (0, jnp.minimum(j, nh - 1))),
            pl.BlockSpec((d, tn), lambda i, j: (0, jnp.maximum(j - nh, 0))),
        ],
        out_specs=pl.BlockSpec((tm, tn), lambda i, j: (i, j)),
        out_shape=jax.ShapeDtypeStruct((t, n), BF16),
        scratch_shapes=[pltpu.VMEM((tm, d), BF16)],
        compiler_params=pltpu.CompilerParams(
            dimension_semantics=("arbitrary", "arbitrary"),
            vmem_limit_bytes=VMEM_LIMIT),
        name="in_proj",
    )(x2, norm_g, w_head, w_rest)


def _mla_kernel(z_ref, pos_ref, freq_ref, qlg_ref, kvlg_ref, qg_ref, kg_ref,
                wuq0_ref, wukv0_ref, wuq1_ref, wukv1_ref, wuq2_ref, wukv2_ref,
                o_ref,
                qn_ref, kvn_ref, cos_ref, sinlo_ref, sinhi_ref, krope_ref, kss_ref,
                qa_ref, ka_ref, va_ref, qb_ref, kb_ref, vb_ref, *, tq, tp):
    s_len = z_ref.shape[0]
    n_chunks = s_len // tq
    slot_a = (qa_ref, ka_ref, va_ref)
    slot_b = (qb_ref, kb_ref, vb_ref)
    half = QK_ROPE_DIM // 2
    scale = QK_HEAD_DIM ** -0.5 * LOG2E
    inv_d = 1.0 / QK_HEAD_DIM
    qg_nope, qg_rope = qg_ref[:, 0:LANES], qg_ref[:, LANES:HEAD_PAD]
    kg_nope, kg_rope = kg_ref[:, 0:LANES], kg_ref[:, LANES:HEAD_PAD]

    def rope(u, rows):
        return (u * cos_ref[rows, :] + pltpu.roll(u, LANES - half, axis=1) * sinlo_ref[rows, :]
                + pltpu.roll(u, half, axis=1) * sinhi_ref[rows, :])

    def prep_rows(wuq_ref, wukv_ref, q_out, k_out, v_out, r):
        rows = slice(r * tq, (r + 1) * tq)
        q = jnp.dot(qn_ref[rows, :], wuq_ref[...], preferred_element_type=F32)
        kv = jnp.dot(kvn_ref[rows, :], wukv_ref[...], preferred_element_type=F32)
        q_nope, q_rope = q[:, 0:LANES], q[:, LANES:HEAD_PAD]
        ss = jnp.sum(q_nope * q_nope + q_rope * q_rope, axis=-1, keepdims=True)
        rq = lax.rsqrt(ss * inv_d + EPS) * scale
        q_out[rows, 0:LANES] = (q_nope * rq * qg_nope).astype(BF16)
        q_out[rows, LANES:HEAD_PAD] = (rope(q_rope * qg_rope, rows) * rq).astype(BF16)
        k_nope = kv[:, 0:LANES]
        ssk = jnp.sum(k_nope * k_nope, axis=-1, keepdims=True) + kss_ref[rows, :]
        rk = lax.rsqrt(ssk * inv_d + EPS)
        k_out[rows, 0:LANES] = (k_nope * rk * kg_nope).astype(BF16)
        k_out[rows, LANES:HEAD_PAD] = (krope_ref[rows, :] * rk).astype(BF16)
        v_out[rows, 0:LANES] = kv[:, LANES:HEAD_PAD].astype(BF16)

    def scores(slot, i):
        q_in, k_in, _ = slot
        return lax.dot_general(q_in[i * tq:(i + 1) * tq, :], k_in[...], (((1,), (1,)), ((), ())),
                               preferred_element_type=F32)

    def finish(sc, slot, col, i):
        m = jnp.max(sc, axis=-1, keepdims=True)
        p = jnp.exp2(sc - m).astype(BF16)
        o = jnp.dot(p, slot[2][...], preferred_element_type=F32)
        o_ref[i * tq:(i + 1) * tq, col * LANES:(col + 1) * LANES] = (
            o[:, :V_HEAD_DIM] / o[:, V_HEAD_DIM:V_HEAD_DIM + 1]).astype(BF16)

    @pl.when(pl.program_id(1) == 0)
    def _():
        for r in range(s_len // tp):
            rows = slice(r * tp, (r + 1) * tp)
            qa = z_ref[rows, 0:Q_LORA_RANK].astype(F32)
            kva = z_ref[rows, Q_LORA_RANK:Q_LORA_RANK + KV_LORA_RANK].astype(F32)
            kpe = z_ref[rows, Q_LORA_RANK + KV_LORA_RANK:Q_LORA_RANK + KV_LORA_RANK + LANES
                        ].astype(F32)
            qn_ref[rows, :] = (_rms(qa, Q_LORA_RANK) * qlg_ref[...]).astype(BF16)
            kvn_ref[rows, :] = (_rms(kva, KV_LORA_RANK) * kvlg_ref[...]).astype(BF16)
            ang = pos_ref[rows, :].astype(F32) * freq_ref[...]
            sinv = jnp.sin(ang)
            lane = lax.broadcasted_iota(jnp.int32, ang.shape, 1)
            cos_ref[rows, :] = jnp.cos(ang)
            sinlo_ref[rows, :] = jnp.where(lane < half, -sinv, 0.0)
            sinhi_ref[rows, :] = jnp.where((lane >= half) & (lane < QK_ROPE_DIM), sinv, 0.0)
            krope_ref[rows, :] = rope(kpe * kg_rope, rows)
            kss_ref[rows, :] = jnp.sum(kpe * kpe, axis=-1, keepdims=True)
            ones_col = jnp.where(lane == 0, 1.0, 0.0).astype(BF16)
            va_ref[rows, LANES:HEAD_PAD] = ones_col
            vb_ref[rows, LANES:HEAD_PAD] = ones_col
        for r in range(n_chunks):
            prep_rows(wuq0_ref, wukv0_ref, *slot_a, r)

    preps = ((wuq1_ref, wukv1_ref) + slot_b, (wuq2_ref, wukv2_ref) + slot_a)
    slots = (slot_a, slot_b)
    total = 2 * n_chunks
    sc_next = scores(slot_a, 0)
    for t in range(total):
        hd, i = divmod(t, n_chunks)
        sc_cur = sc_next
        prep_rows(*preps[hd], i)
        if t + 1 < total:
            hd2, i2 = divmod(t + 1, n_chunks)
            sc_next = scores(slots[hd2], i2)
        finish(sc_cur, slots[hd], hd, i)


def _mla_attention(z, pos, freq, q_lat_g, kv_lat_g, wuq_p, wukv, qg_p, kg_p, b, s, tq, tp):
    pairs = N_HEADS // 2
    const = lambda shape: pl.BlockSpec(shape, lambda bi, g: (0,) * len(shape))
    head_w = lambda rank, f: pl.BlockSpec((rank, HEAD_PAD), lambda bi, g: (0, f(g)))
    first = lambda g: 0
    odd = lambda g: 2 * g + 1
    nxt = lambda g: jnp.minimum(2 * g + 2, N_HEADS - 1)
    rows = lambda w, dt: pltpu.VMEM((s, w), dt)
    return pl.pallas_call(
        functools.partial(_mla_kernel, tq=tq, tp=tp),
        grid=(b, pairs),
        in_specs=[
            pl.BlockSpec((s, SEG), lambda bi, g: (bi, 0)),
            pl.BlockSpec((s, 1), lambda bi, g: (bi, 0)),
            const((1, LANES)), const((1, Q_LORA_RANK)), const((1, KV_LORA_RANK)),
            const((1, HEAD_PAD)), const((1, HEAD_PAD)),
            head_w(Q_LORA_RANK, first), head_w(KV_LORA_RANK, first),
            head_w(Q_LORA_RANK, odd), head_w(KV_LORA_RANK, odd),
            head_w(Q_LORA_RANK, nxt), head_w(KV_LORA_RANK, nxt),
        ],
        out_specs=pl.BlockSpec((s, 2 * V_HEAD_DIM), lambda bi, g: (bi, g)),
        out_shape=jax.ShapeDtypeStruct((b * s, N_HEADS * V_HEAD_DIM), BF16),
        scratch_shapes=[
            rows(Q_LORA_RANK, BF16), rows(KV_LORA_RANK, BF16),
            rows(LANES, F32), rows(LANES, F32), rows(LANES, F32), rows(LANES, F32), rows(1, F32),
            rows(HEAD_PAD, BF16), rows(HEAD_PAD, BF16), rows(HEAD_PAD, BF16),
            rows(HEAD_PAD, BF16), rows(HEAD_PAD, BF16), rows(HEAD_PAD, BF16),
        ],
        compiler_params=pltpu.CompilerParams(
            dimension_semantics=("arbitrary", "arbitrary"), vmem_limit_bytes=VMEM_LIMIT),
        name="mla_attention",
    )(z, pos, freq, q_lat_g, kv_lat_g, qg_p, kg_p, wuq_p, wukv, wuq_p, wukv, wuq_p, wukv)


def _mix_kernel(x_ref, attn_ref, ga0_ref, ga1_ref, cb_ref, cc_ref, cx_ref, gc_ref,
                ma0_ref, ma1_ref, mc0_ref, mc1_ref, ccp_ref, cxp_ref, ccn_ref, cxn_ref,
                cw_ref, cbias_ref, wba_ref, wbc_ref, wout_ref, o_ref, u_ref, *, tm, seq):
    i = pl.program_id(0)

    ga = jnp.concatenate([ga0_ref[...], ga1_ref[...]], axis=1).astype(F32)
    a = (attn_ref[...].astype(F32) * (ga * _sigmoid(ga))).astype(BF16)
    y_attn = jnp.dot(a, wba_ref[...], preferred_element_type=F32)

    u = cc_ref[...].astype(F32) * cx_ref[...].astype(F32)
    first = (i * tm) % seq == 0
    last = ((i + 1) * tm) % seq == 0
    prev = (ccp_ref[HALO_ROWS - 1:HALO_ROWS, :].astype(F32)
            * cxp_ref[HALO_ROWS - 1:HALO_ROWS, :].astype(F32))
    nxt = ccn_ref[0:1, :].astype(F32) * cxn_ref[0:1, :].astype(F32)
    u_ref[7:8, :] = jnp.where(first, 0.0, prev)
    u_ref[8:8 + tm, :] = u
    u_ref[8 + tm:9 + tm, :] = jnp.where(last, 0.0, nxt)
    conv = (u_ref[7:7 + tm, :] * cw_ref[0:1, :] + u * cw_ref[1:2, :]
            + u_ref[9:9 + tm, :] * cw_ref[2:3, :] + cbias_ref[...])
    gc = gc_ref[...].astype(F32)
    cv = (cb_ref[...].astype(F32) * conv * (gc * _sigmoid(gc))).astype(BF16)
    y_conv = jnp.dot(cv, wbc_ref[...], preferred_element_type=F32)

    ma = jnp.concatenate([ma0_ref[...], ma1_ref[...]], axis=1).astype(F32)
    mc = jnp.concatenate([mc0_ref[...], mc1_ref[...]], axis=1).astype(F32)
    merged = (_sigmoid(ma) * y_attn + _sigmoid(mc) * y_conv).astype(BF16)
    o_ref[...] = x_ref[...] + jnp.dot(merged, wout_ref[...], preferred_element_type=F32)


def _mix(x2, attn, z, conv_w, conv_b, wba, wbc, wout, seq, tm):
    t, d = x2.shape
    hb = tm // HALO_ROWS
    n_halo = t // HALO_ROWS
    seg = lambda c: pl.BlockSpec((tm, SEG), lambda i, c=c: (i, c))
    prev = lambda c: pl.BlockSpec((HALO_ROWS, SEG), lambda i, c=c: (jnp.maximum(i * hb - 1, 0), c))
    nxt = lambda c: pl.BlockSpec((HALO_ROWS, SEG),
                                 lambda i, c=c: (jnp.minimum((i + 1) * hb, n_halo - 1), c))
    resident = lambda shape: pl.BlockSpec(shape, lambda i: (0,) * len(shape),
                                          pipeline_mode=pl.Buffered(1))
    return pl.pallas_call(
        functools.partial(_mix_kernel, tm=tm, seq=seq),
        grid=(t // tm,),
        in_specs=[
            pl.BlockSpec((tm, d), lambda i: (i, 0)),
            pl.BlockSpec((tm, attn.shape[1]), lambda i: (i, 0)),
            seg(1), seg(2), seg(3), seg(4), seg(5), seg(6), seg(7), seg(8), seg(9), seg(10),
            prev(4), prev(5), nxt(4), nxt(5),
            resident(conv_w.shape), resident(conv_b.shape),
            resident(wba.shape), resident(wbc.shape), resident(wout.shape),
        ],
        out_specs=pl.BlockSpec((tm, d), lambda i: (i, 0)),
        out_shape=jax.ShapeDtypeStruct((t, d), F32),
        scratch_shapes=[pltpu.VMEM((tm + 16, SEG), F32)],
        compiler_params=pltpu.CompilerParams(
            dimension_semantics=("arbitrary",), vmem_limit_bytes=VMEM_LIMIT),
        name="mix",
    )(x2, attn, z, z, z, z, z, z, z, z, z, z, z, z, z, z, conv_w, conv_b, wba, wbc, wout)


def _ple_kernel(x_ref, p_ref, g_ref, wg_ref, wp_ref, o_ref):
    x = x_ref[...]
    h = (_rms(x, x.shape[-1]) * g_ref[...]).astype(BF16)
    gate = _sigmoid(jnp.dot(h, wg_ref[...], preferred_element_type=F32))
    proj = jnp.dot(p_ref[...].astype(BF16), wp_ref[...], preferred_element_type=F32)
    o_ref[...] = x + gate * proj


def _ple(x2, p2, ple_g, wg, wp, tm):
    t, d = x2.shape
    resident = lambda shape: pl.BlockSpec(shape, lambda i: (0,) * len(shape),
                                          pipeline_mode=pl.Buffered(1))
    return pl.pallas_call(
        _ple_kernel,
        grid=(t // tm,),
        in_specs=[
            pl.BlockSpec((tm, d), lambda i: (i, 0)),
            pl.BlockSpec((tm, p2.shape[1]), lambda i: (i, 0)),
            resident((1, d)), resident(wg.shape), resident(wp.shape),
        ],
        out_specs=pl.BlockSpec((tm, d), lambda i: (i, 0)),
        out_shape=jax.ShapeDtypeStruct((t, d), F32),
        compiler_params=pltpu.CompilerParams(
            dimension_semantics=("arbitrary",), vmem_limit_bytes=VMEM_LIMIT),
        name="ple",
    )(x2, p2, ple_g, wg, wp)


def _layer(x2, p2, pos, freq, b, s, norm_g, w_in, q_lat_g, kv_lat_g, w_uq, w_ukv,
           q_norm_g, k_norm_g, conv_w, conv_b, w_branch_attn, w_branch_conv,
           w_out, ple_norm_g, w_ple_gate, w_ple_proj):
    d = x2.shape[1]
    head_cols = Q_LORA_RANK + KV_LORA_RANK + QK_ROPE_DIM
    w_head = jnp.pad(w_in[:, :head_cols].astype(BF16), ((0, 0), (0, SEG - head_cols)))
    w_rest = w_in[:, head_cols:].astype(BF16)
    wuq_p = jnp.pad(w_uq.reshape(Q_LORA_RANK, N_HEADS, QK_HEAD_DIM),
                    ((0, 0), (0, 0), (0, HEAD_PAD - QK_HEAD_DIM))
                    ).reshape(Q_LORA_RANK, N_HEADS * HEAD_PAD).astype(BF16)
    pad_g = lambda g: jnp.pad(g, (0, HEAD_PAD - QK_HEAD_DIM)).reshape(1, HEAD_PAD)

    z = _in_proj(x2, norm_g.reshape(1, d), w_head, w_rest, tm=1024, tn=1024)
    attn = _mla_attention(z, pos, freq, q_lat_g.reshape(1, -1), kv_lat_g.reshape(1, -1),
                          wuq_p, w_ukv.astype(BF16), pad_g(q_norm_g), pad_g(k_norm_g),
                          b, s, tq=512, tp=512)
    x2 = _mix(x2, attn, z, conv_w, conv_b.reshape(1, -1), w_branch_attn.astype(BF16),
              w_branch_conv.astype(BF16), w_out.astype(BF16), seq=s, tm=256)
    return _ple(x2, p2, ple_norm_g.reshape(1, d), w_ple_gate.astype(BF16),
                w_ple_proj.astype(BF16), tm=512)


def kernel(x, p, positions, norm_g, w_in, q_lat_g, kv_lat_g, w_uq, w_ukv, q_norm_g, k_norm_g, conv_w, conv_b, w_branch_attn, w_branch_conv, w_out, ple_norm_g, w_ple_gate, w_ple_proj):
    b, s, d = x.shape
    depth = p.shape[0]
    x2 = x.reshape(b * s, d)
    pos = positions.reshape(b * s, 1)
    half = QK_ROPE_DIM // 2
    inv_freq = 1.0 / (ROPE_THETA ** (jnp.arange(0, QK_ROPE_DIM, 2, dtype=F32) / QK_ROPE_DIM))
    freq = jnp.concatenate([inv_freq, inv_freq, jnp.zeros((LANES - 2 * half,), F32)]).reshape(1, LANES)
    for i in range(depth):
        x2 = _layer(x2, p[i].reshape(b * s, -1), pos, freq, b, s, norm_g[i], w_in[i],
                    q_lat_g[i], kv_lat_g[i], w_uq[i], w_ukv[i], q_norm_g[i], k_norm_g[i],
                    conv_w[i], conv_b[i], w_branch_attn[i], w_branch_conv[i], w_out[i],
                    ple_norm_g[i], w_ple_gate[i], w_ple_proj[i])
    return x2.reshape(b, s, d)
```

```python
import functools

import jax
import jax.numpy as jnp
from jax import lax
from jax.experimental import pallas as pl
from jax.experimental.pallas import tpu as pltpu

N_HEADS = 16
QK_NOPE_DIM = 128
QK_ROPE_DIM = 64
QK_HEAD_DIM = QK_NOPE_DIM + QK_ROPE_DIM
V_HEAD_DIM = 128
Q_LORA_RANK = 512
KV_LORA_RANK = 256
ROPE_THETA = 10000.0
EPS = 1e-6
LOG2E = 1.4426950408889634

LANES = 128
HEAD_PAD = 2 * LANES
SEG = 1024
HALO_ROWS = 16
VMEM_LIMIT = 56 * 1024 * 1024

BF16 = jnp.bfloat16
F32 = jnp.float32


def _rms(v, axis_size):
    return v * lax.rsqrt(jnp.sum(v * v, axis=-1, keepdims=True) * (1.0 / axis_size) + EPS)


def _sigmoid(v):
    return 0.5 * jnp.tanh(0.5 * v) + 0.5


def _silu(v):
    return v * _sigmoid(v)


def _identity(v):
    return v


Z_TILE_ACTS = (_identity, _silu, _silu, _identity, _identity, _identity, _silu,
               _sigmoid, _sigmoid, _sigmoid, _sigmoid)


def _in_proj_kernel(x_ref, g_ref, wh_ref, wr_ref, z_ref, h_ref, *, n_head_tiles):
    j = pl.program_id(1)

    @pl.when(j == 0)
    def _():
        x = x_ref[...]
        h_ref[...] = (_rms(x, x.shape[-1]) * g_ref[...]).astype(BF16)

    variants = {}
    for tile, act in enumerate(Z_TILE_ACTS):
        variants.setdefault((tile < n_head_tiles, act), []).append(tile)
    for (is_head, act), tiles in variants.items():
        w_ref = wh_ref if is_head else wr_ref
        cond = functools.reduce(jnp.logical_or, [j == tile for tile in tiles])

        @pl.when(cond)
        def _(w_ref=w_ref, act=act):
            z_ref[...] = act(jnp.dot(h_ref[...], w_ref[...], preferred_element_type=F32)
                             ).astype(BF16)


def _in_proj(x2, norm_g, w_head, w_rest, tm):
    t, d = x2.shape
    tn = SEG
    nh = w_head.shape[1] // tn
    n = w_head.shape[1] + w_rest.shape[1]
    assert n == tn * len(Z_TILE_ACTS)
    return pl.pallas_call(
        functools.partial(_in_proj_kernel, n_head_tiles=nh),
        grid=(t // tm, n // tn),
        in_specs=[
            pl.BlockSpec((tm, d), lambda i, j: (i, 0)),
            pl.BlockSpec((1, d), lambda i, j: (0, 0)),
            pl.BlockSpec((d, tn), lambda i, j: (0, jnp.minimum(j, nh - 1))),
            pl.BlockSpec((d, tn), lambda i, j: (0, jnp.maximum(j - nh, 0))),
        ],
        out_specs=pl.BlockSpec((tm, tn), lambda i, j: (i, j)),
        out_shape=jax.ShapeDtypeStruct((t, n), BF16),
        scratch_shapes=[pltpu.VMEM((tm, d), BF16)],
        compiler_params=pltpu.CompilerParams(
            dimension_semantics=("arbitrary", "arbitrary"),
            vmem_limit_bytes=VMEM_LIMIT),
        name="in_proj",
    )(x2, norm_g, w_head, w_rest)


def _mla_kernel(z_ref, pos_ref, freq_ref, qlg_ref, kvlg_ref, qg_ref, kg_ref,
                wuq0_ref, wukv0_ref, wuq1_ref, wukv1_ref, wuq2_ref, wukv2_ref,
                o_ref,
                qn_ref, kvn_ref, cos_ref, sinlo_ref, sinhi_ref, krope_ref, kss_ref,
                qa_ref, ka_ref, va_ref, qb_ref, kb_ref, vb_ref, *, tq, tp):
    s_len = z_ref.shape[0]
    n_chunks = s_len // tq
    slot_a = (qa_ref, ka_ref, va_ref)
    slot_b = (qb_ref, kb_ref, vb_ref)
    half = QK_ROPE_DIM // 2
    scale = QK_HEAD_DIM ** -0.5 * LOG2E
    inv_d = 1.0 / QK_HEAD_DIM
    qg_nope, qg_rope = qg_ref[:, 0:LANES], qg_ref[:, LANES:HEAD_PAD]
    kg_nope, kg_rope = kg_ref[:, 0:LANES], kg_ref[:, LANES:HEAD_PAD]

    def rope(u, rows):
        return (u * cos_ref[rows, :] + pltpu.roll(u, LANES - half, axis=1) * sinlo_ref[rows, :]
                + pltpu.roll(u, half, axis=1) * sinhi_ref[rows, :])

    def prep_rows(wuq_ref, wukv_ref, q_out, k_out, v_out, r):
        rows = slice(r * tq, (r + 1) * tq)
        q = jnp.dot(qn_ref[rows, :], wuq_ref[...], preferred_element_type=F32)
        kv = jnp.dot(kvn_ref[rows, :], wukv_ref[...], preferred_element_type=F32)
        q_nope, q_rope = q[:, 0:LANES], q[:, LANES:HEAD_PAD]
        ss = jnp.sum(q_nope * q_nope + q_rope * q_rope, axis=-1, keepdims=True)
        rq = lax.rsqrt(ss * inv_d + EPS) * scale
        q_out[rows, 0:LANES] = (q_nope * rq * qg_nope).astype(BF16)
        q_out[rows, LANES:HEAD_PAD] = (rope(q_rope * qg_rope, rows) * rq).astype(BF16)
        k_nope = kv[:, 0:LANES]
        ssk = jnp.sum(k_nope * k_nope, axis=-1, keepdims=True) + kss_ref[rows, :]
        rk = lax.rsqrt(ssk * inv_d + EPS)
        k_out[rows, 0:LANES] = (k_nope * rk * kg_nope).astype(BF16)
        k_out[rows, LANES:HEAD_PAD] = (krope_ref[rows, :] * rk).astype(BF16)
        v_out[rows, 0:LANES] = kv[:, LANES:HEAD_PAD].astype(BF16)

    def scores(slot, i):
        q_in, k_in, _ = slot
        return lax.dot_general(q_in[i * tq:(i + 1) * tq, :], k_in[...], (((1,), (1,)), ((), ())),
                               preferred_element_type=F32)

    def finish(sc, slot, col, i):
        m = jnp.max(sc, axis=-1, keepdims=True)
        p = jnp.exp2(sc - m).astype(BF16)
        o = jnp.dot(p, slot[2][...], preferred_element_type=F32)
        o_ref[i * tq:(i + 1) * tq, col * LANES:(col + 1) * LANES] = (
            o[:, :V_HEAD_DIM] / o[:, V_HEAD_DIM:V_HEAD_DIM + 1]).astype(BF16)

    @pl.when(pl.program_id(1) == 0)
    def _():
        for r in range(s_len // tp):
            rows = slice(r * tp, (r + 1) * tp)
            qa = z_ref[rows, 0:Q_LORA_RANK].astype(F32)
            kva = z_ref[rows, Q_LORA_RANK:Q_LORA_RANK + KV_LORA_RANK].astype(F32)
            kpe = z_ref[rows, Q_LORA_RANK + KV_LORA_RANK:Q_LORA_RANK + KV_LORA_RANK + LANES
                        ].astype(F32)
            qn_ref[rows, :] = (_rms(qa, Q_LORA_RANK) * qlg_ref[...]).astype(BF16)
            kvn_ref[rows, :] = (_rms(kva, KV_LORA_RANK) * kvlg_ref[...]).astype(BF16)
            ang = pos_ref[rows, :].astype(F32) * freq_ref[...]
            sinv = jnp.sin(ang)
            lane = lax.broadcasted_iota(jnp.int32, ang.shape, 1)
            cos_ref[rows, :] = jnp.cos(ang)
            sinlo_ref[rows, :] = jnp.where(lane < half, -sinv, 0.0)
            sinhi_ref[rows, :] = jnp.where((lane >= half) & (lane < QK_ROPE_DIM), sinv, 0.0)
            krope_ref[rows, :] = rope(kpe * kg_rope, rows)
            kss_ref[rows, :] = jnp.sum(kpe * kpe, axis=-1, keepdims=True)
            ones_col = jnp.where(lane == 0, 1.0, 0.0).astype(BF16)
            va_ref[rows, LANES:HEAD_PAD] = ones_col
            vb_ref[rows, LANES:HEAD_PAD] = ones_col
        for r in range(n_chunks):
            prep_rows(wuq0_ref, wukv0_ref, *slot_a, r)

    preps = ((wuq1_ref, wukv1_ref) + slot_b, (wuq2_ref, wukv2_ref) + slot_a)
    slots = (slot_a, slot_b)
    total = 2 * n_chunks
    sc_next = scores(slot_a, 0)
    for t in range(total):
        hd, i = divmod(t, n_chunks)
        sc_cur = sc_next
        prep_rows(*preps[hd], i)
        if t + 1 < total:
            hd2, i2 = divmod(t + 1, n_chunks)
            sc_next = scores(slots[hd2], i2)
        finish(sc_cur, slots[hd], hd, i)


def _mla_attention(z, pos, freq, q_lat_g, kv_lat_g, wuq_p, wukv, qg_p, kg_p, b, s, tq, tp):
    pairs = N_HEADS // 2
    const = lambda shape: pl.BlockSpec(shape, lambda bi, g: (0,) * len(shape))
    head_w = lambda rank, f: pl.BlockSpec((rank, HEAD_PAD), lambda bi, g: (0, f(g)))
    first = lambda g: 0
    odd = lambda g: 2 * g + 1
    nxt = lambda g: jnp.minimum(2 * g + 2, N_HEADS - 1)
    rows = lambda w, dt: pltpu.VMEM((s, w), dt)
    return pl.pallas_call(
        functools.partial(_mla_kernel, tq=tq, tp=tp),
        grid=(b, pairs),
        in_specs=[
            pl.BlockSpec((s, SEG), lambda bi, g: (bi, 0)),
            pl.BlockSpec((s, 1), lambda bi, g: (bi, 0)),
            const((1, LANES)), const((1, Q_LORA_RANK)), const((1, KV_LORA_RANK)),
            const((1, HEAD_PAD)), const((1, HEAD_PAD)),
            head_w(Q_LORA_RANK, first), head_w(KV_LORA_RANK, first),
            head_w(Q_LORA_RANK, odd), head_w(KV_LORA_RANK, odd),
            head_w(Q_LORA_RANK, nxt), head_w(KV_LORA_RANK, nxt),
        ],
        out_specs=pl.BlockSpec((s, 2 * V_HEAD_DIM), lambda bi, g: (bi, g)),
        out_shape=jax.ShapeDtypeStruct((b * s, N_HEADS * V_HEAD_DIM), BF16),
        scratch_shapes=[
            rows(Q_LORA_RANK, BF16), rows(KV_LORA_RANK, BF16),
            rows(LANES, F32), rows(LANES, F32), rows(LANES, F32), rows(LANES, F32), rows(1, F32),
            rows(HEAD_PAD, BF16), rows(HEAD_PAD, BF16), rows(HEAD_PAD, BF16),
            rows(HEAD_PAD, BF16), rows(HEAD_PAD, BF16), rows(HEAD_PAD, BF16),
        ],
        compiler_params=pltpu.CompilerParams(
            dimension_semantics=("arbitrary", "arbitrary"), vmem_limit_bytes=VMEM_LIMIT),
        name="mla_attention",
    )(z, pos, freq, q_lat_g, kv_lat_g, qg_p, kg_p, wuq_p, wukv, wuq_p, wukv, wuq_p, wukv)


def _mix_kernel(attn_ref, ga0_ref, ga1_ref, cb_ref, cc_ref, cx_ref, gc_ref,
                ma0_ref, ma1_ref, mc0_ref, mc1_ref, ccp_ref, cxp_ref, ccn_ref, cxn_ref,
                cw_ref, cbias_ref, wba_ref, wbc_ref, o_ref, u_ref, *, tm, seq):
    i = pl.program_id(0)

    a = attn_ref[...] * jnp.concatenate([ga0_ref[...], ga1_ref[...]], axis=1)

    u = cc_ref[...].astype(F32) * cx_ref[...].astype(F32)
    first = (i * tm) % seq == 0
    last = ((i + 1) * tm) % seq == 0
    prev = (ccp_ref[HALO_ROWS - 1:HALO_ROWS, :].astype(F32)
            * cxp_ref[HALO_ROWS - 1:HALO_ROWS, :].astype(F32))
    nxt = ccn_ref[0:1, :].astype(F32) * cxn_ref[0:1, :].astype(F32)
    u_ref[7:8, :] = jnp.where(first, 0.0, prev)
    u_ref[8:8 + tm, :] = u
    u_ref[8 + tm:9 + tm, :] = jnp.where(last, 0.0, nxt)
    conv = (u_ref[7:7 + tm, :] * cw_ref[0:1, :] + u * cw_ref[1:2, :]
            + u_ref[9:9 + tm, :] * cw_ref[2:3, :] + cbias_ref[...])
    cv = (cb_ref[...].astype(F32) * conv * gc_ref[...].astype(F32)).astype(BF16)

    for c, (ma_ref, mc_ref) in enumerate(((ma0_ref, mc0_ref), (ma1_ref, mc1_ref))):
        cols = slice(c * SEG, (c + 1) * SEG)
        y_attn = jnp.dot(a, wba_ref[:, cols], preferred_element_type=F32)
        y_conv = jnp.dot(cv, wbc_ref[:, cols], preferred_element_type=F32)
        o_ref[:, cols] = (ma_ref[...].astype(F32) * y_attn
                          + mc_ref[...].astype(F32) * y_conv).astype(BF16)


def _mix(attn, z, conv_w, conv_b, wba, wbc, seq, tm):
    t, d = attn.shape[0], wba.shape[1]
    hb = tm // HALO_ROWS
    n_halo = t // HALO_ROWS
    seg = lambda c: pl.BlockSpec((tm, SEG), lambda i, c=c: (i, c))
    prev = lambda c: pl.BlockSpec((HALO_ROWS, SEG), lambda i, c=c: (jnp.maximum(i * hb - 1, 0), c))
    nxt = lambda c: pl.BlockSpec((HALO_ROWS, SEG),
                                 lambda i, c=c: (jnp.minimum((i + 1) * hb, n_halo - 1), c))
    resident = lambda shape: pl.BlockSpec(shape, lambda i: (0,) * len(shape),
                                          pipeline_mode=pl.Buffered(1))
    return pl.pallas_call(
        functools.partial(_mix_kernel, tm=tm, seq=seq),
        grid=(t // tm,),
        in_specs=[
            pl.BlockSpec((tm, attn.shape[1]), lambda i: (i, 0)),
            seg(1), seg(2), seg(3), seg(4), seg(5), seg(6), seg(7), seg(8), seg(9), seg(10),
            prev(4), prev(5), nxt(4), nxt(5),
            resident(conv_w.shape), resident(conv_b.shape),
            resident(wba.shape), resident(wbc.shape),
        ],
        out_specs=pl.BlockSpec((tm, d), lambda i: (i, 0)),
        out_shape=jax.ShapeDtypeStruct((t, d), BF16),
        scratch_shapes=[pltpu.VMEM((tm + 16, SEG), F32)],
        compiler_params=pltpu.CompilerParams(
            dimension_semantics=("arbitrary",), vmem_limit_bytes=VMEM_LIMIT),
        name="mix",
    )(attn, z, z, z, z, z, z, z, z, z, z, z, z, z, z, conv_w, conv_b, wba, wbc)


def _out_ple_kernel(x_ref, m_ref, p_ref, g_ref, wout_ref, wg_ref, wp_ref, o_ref, x1_ref):
    x1_ref[...] = x_ref[...] + jnp.dot(m_ref[...], wout_ref[...], preferred_element_type=F32)
    pb = p_ref[...].astype(BF16)
    x1 = x1_ref[...]
    h = (_rms(x1, x1.shape[-1]) * g_ref[...]).astype(BF16)
    for c in range(x1.shape[-1] // SEG):
        cols = slice(c * SEG, (c + 1) * SEG)
        proj = jnp.dot(pb, wp_ref[:, cols], preferred_element_type=F32)
        gate = _sigmoid(jnp.dot(h, wg_ref[:, cols], preferred_element_type=F32))
        o_ref[:, cols] = x1_ref[:, cols] + gate * proj


def _out_ple(x2, merged, p2, ple_g, wout, wg, wp, tm):
    t, d = x2.shape
    resident = lambda shape: pl.BlockSpec(shape, lambda i: (0,) * len(shape),
                                          pipeline_mode=pl.Buffered(1))
    return pl.pallas_call(
        _out_ple_kernel,
        grid=(t // tm,),
        in_specs=[
            pl.BlockSpec((tm, d), lambda i: (i, 0)),
            pl.BlockSpec((tm, d), lambda i: (i, 0)),
            pl.BlockSpec((tm, p2.shape[1]), lambda i: (i, 0)),
            resident((1, d)), resident(wout.shape), resident(wg.shape), resident(wp.shape),
        ],
        out_specs=pl.BlockSpec((tm, d), lambda i: (i, 0)),
        out_shape=jax.ShapeDtypeStruct((t, d), F32),
        scratch_shapes=[pltpu.VMEM((tm, d), F32)],
        compiler_params=pltpu.CompilerParams(
            dimension_semantics=("arbitrary",), vmem_limit_bytes=VMEM_LIMIT),
        name="out_ple",
    )(x2, merged, p2, ple_g, wout, wg, wp)


def _layer(x2, p2, pos, freq, b, s, norm_g, w_in, q_lat_g, kv_lat_g, w_uq, w_ukv,
           q_norm_g, k_norm_g, conv_w, conv_b, w_branch_attn, w_branch_conv,
           w_out, ple_norm_g, w_ple_gate, w_ple_proj):
    d = x2.shape[1]
    head_cols = Q_LORA_RANK + KV_LORA_RANK + QK_ROPE_DIM
    w_head = jnp.pad(w_in[:, :head_cols].astype(BF16), ((0, 0), (0, SEG - head_cols)))
    w_rest = w_in[:, head_cols:].astype(BF16)
    wuq_p = jnp.pad(w_uq.reshape(Q_LORA_RANK, N_HEADS, QK_HEAD_DIM),
                    ((0, 0), (0, 0), (0, HEAD_PAD - QK_HEAD_DIM))
                    ).reshape(Q_LORA_RANK, N_HEADS * HEAD_PAD).astype(BF16)
    pad_g = lambda g: jnp.pad(g, (0, HEAD_PAD - QK_HEAD_DIM)).reshape(1, HEAD_PAD)

    z = _in_proj(x2, norm_g.reshape(1, d), w_head, w_rest, tm=1024)
    attn = _mla_attention(z, pos, freq, q_lat_g.reshape(1, -1), kv_lat_g.reshape(1, -1),
                          wuq_p, w_ukv.astype(BF16), pad_g(q_norm_g), pad_g(k_norm_g),
                          b, s, tq=512, tp=512)
    merged = _mix(attn, z, conv_w, conv_b.reshape(1, -1), w_branch_attn.astype(BF16),
                  w_branch_conv.astype(BF16), seq=s, tm=512)
    return _out_ple(x2, merged, p2, ple_norm_g.reshape(1, d), w_out.astype(BF16),
                    w_ple_gate.astype(BF16), w_ple_proj.astype(BF16), tm=512)


def kernel(x, p, positions, norm_g, w_in, q_lat_g, kv_lat_g, w_uq, w_ukv, q_norm_g, k_norm_g, conv_w, conv_b, w_branch_attn, w_branch_conv, w_out, ple_norm_g, w_ple_gate, w_ple_proj):
    b, s, d = x.shape
    depth = p.shape[0]
    x2 = x.reshape(b * s, d)
    pos = positions.reshape(b * s, 1)
    half = QK_ROPE_DIM // 2
    inv_freq = 1.0 / (ROPE_THETA ** (jnp.arange(0, QK_ROPE_DIM, 2, dtype=F32) / QK_ROPE_DIM))
    freq = jnp.concatenate([inv_freq, inv_freq, jnp.zeros((LANES - 2 * half,), F32)]).reshape(1, LANES)
    for i in range(depth):
        x2 = _layer(x2, p[i].reshape(b * s, -1), pos, freq, b, s, norm_g[i], w_in[i],
                    q_lat_g[i], kv_lat_g[i], w_uq[i], w_ukv[i], q_norm_g[i], k_norm_g[i],
                    conv_w[i], conv_b[i], w_branch_attn[i], w_branch_conv[i], w_out[i],
                    ple_norm_g[i], w_ple_gate[i], w_ple_proj[i])
    return x2.reshape(b, s, d)
```

```python
import functools

import jax
import jax.numpy as jnp
from jax import lax
from jax.experimental import pallas as pl
from jax.experimental.pallas import tpu as pltpu

N_HEADS = 16
QK_NOPE_DIM = 128
QK_ROPE_DIM = 64
QK_HEAD_DIM = QK_NOPE_DIM + QK_ROPE_DIM
V_HEAD_DIM = 128
Q_LORA_RANK = 512
KV_LORA_RANK = 256
ROPE_THETA = 10000.0
EPS = 1e-6
LOG2E = 1.4426950408889634

LANES = 128
HEAD_PAD = 2 * LANES
SEG = 1024
HALO_ROWS = 16
VMEM_LIMIT = 56 * 1024 * 1024

BF16 = jnp.bfloat16
F32 = jnp.float32


def _rms(v, axis_size):
    return v * lax.rsqrt(jnp.sum(v * v, axis=-1, keepdims=True) * (1.0 / axis_size) + EPS)


def _sigmoid(v):
    return 0.5 * jnp.tanh(0.5 * v) + 0.5


def _silu(v):
    return v * _sigmoid(v)


def _identity(v):
    return v


Z_TILE_ACTS = (_identity, _silu, _silu, _identity, _identity, _identity, _silu,
               _sigmoid, _sigmoid, _sigmoid, _sigmoid)


def _w_in_split_kernel(w_ref, head_ref, rest_ref, *, head_cols):
    head_ref[:, 0:head_cols] = w_ref[:, 0:head_cols].astype(BF16)
    head_ref[:, head_cols:SEG] = jnp.zeros((head_ref.shape[0], SEG - head_cols), BF16)
    rest_ref[...] = w_ref[:, head_cols:].astype(BF16)


def _w_in_split(w_in, head_cols, tr):
    d, n = w_in.shape
    return pl.pallas_call(
        functools.partial(_w_in_split_kernel, head_cols=head_cols),
        grid=(d // tr,),
        in_specs=[pl.BlockSpec((tr, n), lambda i: (i, 0))],
        out_specs=[pl.BlockSpec((tr, SEG), lambda i: (i, 0)),
                   pl.BlockSpec((tr, n - head_cols), lambda i: (i, 0))],
        out_shape=[jax.ShapeDtypeStruct((d, SEG), BF16),
                   jax.ShapeDtypeStruct((d, n - head_cols), BF16)],
        compiler_params=pltpu.CompilerParams(
            dimension_semantics=("arbitrary",), vmem_limit_bytes=VMEM_LIMIT),
        name="w_in_split",
    )(w_in)


def _in_proj_kernel(x_ref, g_ref, wh_ref, wr_ref, z_ref, h_ref, *, n_head_tiles):
    j = pl.program_id(1)

    @pl.when(j == 0)
    def _():
        x = x_ref[...]
        h_ref[...] = (_rms(x, x.shape[-1]) * g_ref[...]).astype(BF16)

    variants = {}
    for tile, act in enumerate(Z_TILE_ACTS):
        variants.setdefault((tile < n_head_tiles, act), []).append(tile)
    for (is_head, act), tiles in variants.items():
        w_ref = wh_ref if is_head else wr_ref
        cond = functools.reduce(jnp.logical_or, [j == tile for tile in tiles])

        @pl.when(cond)
        def _(w_ref=w_ref, act=act):
            z_ref[...] = act(jnp.dot(h_ref[...], w_ref[...], preferred_element_type=F32)
                             ).astype(BF16)


def _in_proj(x2, norm_g, w_head, w_rest, tm):
    t, d = x2.shape
    tn = SEG
    nh = w_head.shape[1] // tn
    n = w_head.shape[1] + w_rest.shape[1]
    assert n == tn * len(Z_TILE_ACTS)
    return pl.pallas_call(
        functools.partial(_in_proj_kernel, n_head_tiles=nh),
        grid=(t // tm, n // tn),
        in_specs=[
            pl.BlockSpec((tm, d), lambda i, j: (i, 0)),
            pl.BlockSpec((1, d), lambda i, j: (0, 0)),
            pl.BlockSpec((d, tn), lambda i, j: (0, jnp.minimum(j, nh - 1))),
            pl.BlockSpec((d, tn), lambda i, j: (0, jnp.maximum(j - nh, 0))),
        ],
        out_specs=pl.BlockSpec((tm, tn), lambda i, j: (i, j)),
        out_shape=jax.ShapeDtypeStruct((t, n), BF16),
        scratch_shapes=[pltpu.VMEM((tm, d), BF16)],
        compiler_params=pltpu.CompilerParams(
            dimension_semantics=("arbitrary", "arbitrary"),
            vmem_limit_bytes=VMEM_LIMIT),
        name="in_proj",
    )(x2, norm_g, w_head, w_rest)


def _mla_kernel(z_ref, pos_ref, freq_ref, qlg_ref, kvlg_ref, qg_ref, kg_ref,
                wuq0_ref, wukv0_ref, wuq1_ref, wukv1_ref, wuq2_ref, wukv2_ref,
                o_ref,
                qn_ref, kvn_ref, cos_ref, sinlo_ref, sinhi_ref, krope_ref, kss_ref,
                qa_ref, ka_ref, va_ref, qb_ref, kb_ref, vb_ref, *, tq, tp):
    s_len = z_ref.shape[0]
    n_chunks = s_len // tq
    slot_a = (qa_ref, ka_ref, va_ref)
    slot_b = (qb_ref, kb_ref, vb_ref)
    half = QK_ROPE_DIM // 2
    scale = QK_HEAD_DIM ** -0.5 * LOG2E
    inv_d = 1.0 / QK_HEAD_DIM
    qg_nope, qg_rope = qg_ref[:, 0:LANES], qg_ref[:, LANES:HEAD_PAD]
    kg_nope, kg_rope = kg_ref[:, 0:LANES], kg_ref[:, LANES:HEAD_PAD]

    def rope(u, rows):
        return (u * cos_ref[rows, :] + pltpu.roll(u, LANES - half, axis=1) * sinlo_ref[rows, :]
                + pltpu.roll(u, half, axis=1) * sinhi_ref[rows, :])

    def prep_rows(wuq_ref, wukv_ref, q_out, k_out, v_out, r):
        rows = slice(r * tq, (r + 1) * tq)
        q = jnp.dot(qn_ref[rows, :], wuq_ref[...], preferred_element_type=F32)
        kv = jnp.dot(kvn_ref[rows, :], wukv_ref[...], preferred_element_type=F32)
        q_nope, q_rope = q[:, 0:LANES], q[:, LANES:HEAD_PAD]
        ss = jnp.sum(q_nope * q_nope + q_rope * q_rope, axis=-1, keepdims=True)
        rq = lax.rsqrt(ss * inv_d + EPS) * scale
        q_out[rows, 0:LANES] = (q_nope * rq * qg_nope).astype(BF16)
        q_out[rows, LANES:HEAD_PAD] = (rope(q_rope * qg_rope, rows) * rq).astype(BF16)
        k_nope = kv[:, 0:LANES]
        ssk = jnp.sum(k_nope * k_nope, axis=-1, keepdims=True) + kss_ref[rows, :]
        rk = lax.rsqrt(ssk * inv_d + EPS)
        k_out[rows, 0:LANES] = (k_nope * rk * kg_nope).astype(BF16)
        k_out[rows, LANES:HEAD_PAD] = (krope_ref[rows, :] * rk).astype(BF16)
        v_out[rows, 0:LANES] = kv[:, LANES:HEAD_PAD].astype(BF16)

    def scores(slot, i):
        q_in, k_in, _ = slot
        return lax.dot_general(q_in[i * tq:(i + 1) * tq, :], k_in[...], (((1,), (1,)), ((), ())),
                               preferred_element_type=F32)

    def finish(sc, slot, col, i):
        m = jnp.max(sc, axis=-1, keepdims=True)
        p = jnp.exp2(sc - m).astype(BF16)
        o = jnp.dot(p, slot[2][...], preferred_element_type=F32)
        o_ref[i * tq:(i + 1) * tq, col * LANES:(col + 1) * LANES] = (
            o[:, :V_HEAD_DIM] / o[:, V_HEAD_DIM:V_HEAD_DIM + 1]).astype(BF16)

    @pl.when(pl.program_id(1) == 0)
    def _():
        for r in range(s_len // tp):
            rows = slice(r * tp, (r + 1) * tp)
            qa = z_ref[rows, 0:Q_LORA_RANK].astype(F32)
            kva = z_ref[rows, Q_LORA_RANK:Q_LORA_RANK + KV_LORA_RANK].astype(F32)
            kpe = z_ref[rows, Q_LORA_RANK + KV_LORA_RANK:Q_LORA_RANK + KV_LORA_RANK + LANES
                        ].astype(F32)
            qn_ref[rows, :] = (_rms(qa, Q_LORA_RANK) * qlg_ref[...]).astype(BF16)
            kvn_ref[rows, :] = (_rms(kva, KV_LORA_RANK) * kvlg_ref[...]).astype(BF16)
            ang = pos_ref[rows, :].astype(F32) * freq_ref[...]
            sinv = jnp.sin(ang)
            lane = lax.broadcasted_iota(jnp.int32, ang.shape, 1)
            cos_ref[rows, :] = jnp.cos(ang)
            sinlo_ref[rows, :] = jnp.where(lane < half, -sinv, 0.0)
            sinhi_ref[rows, :] = jnp.where((lane >= half) & (lane < QK_ROPE_DIM), sinv, 0.0)
            krope_ref[rows, :] = rope(kpe * kg_rope, rows)
            kss_ref[rows, :] = jnp.sum(kpe * kpe, axis=-1, keepdims=True)
            ones_col = jnp.where(lane == 0, 1.0, 0.0).astype(BF16)
            va_ref[rows, LANES:HEAD_PAD] = ones_col
            vb_ref[rows, LANES:HEAD_PAD] = ones_col
        for r in range(n_chunks):
            prep_rows(wuq0_ref, wukv0_ref, *slot_a, r)

    preps = ((wuq1_ref, wukv1_ref) + slot_b, (wuq2_ref, wukv2_ref) + slot_a)
    slots = (slot_a, slot_b)
    total = 2 * n_chunks
    sc_next = scores(slot_a, 0)
    for t in range(total):
        hd, i = divmod(t, n_chunks)
        sc_cur = sc_next
        prep_rows(*preps[hd], i)
        if t + 1 < total:
            hd2, i2 = divmod(t + 1, n_chunks)
            sc_next = scores(slots[hd2], i2)
        finish(sc_cur, slots[hd], hd, i)


def _mla_attention(z, pos, freq, q_lat_g, kv_lat_g, wuq_p, wukv, qg_p, kg_p, b, s, tq, tp):
    pairs = N_HEADS // 2
    const = lambda shape: pl.BlockSpec(shape, lambda bi, g: (0,) * len(shape))
    head_w = lambda rank, f: pl.BlockSpec((rank, HEAD_PAD), lambda bi, g: (0, f(g)))
    first = lambda g: 0
    odd = lambda g: 2 * g + 1
    nxt = lambda g: jnp.minimum(2 * g + 2, N_HEADS - 1)
    rows = lambda w, dt: pltpu.VMEM((s, w), dt)
    return pl.pallas_call(
        functools.partial(_mla_kernel, tq=tq, tp=tp),
        grid=(b, pairs),
        in_specs=[
            pl.BlockSpec((s, SEG), lambda bi, g: (bi, 0)),
            pl.BlockSpec((s, 1), lambda bi, g: (bi, 0)),
            const((1, LANES)), const((1, Q_LORA_RANK)), const((1, KV_LORA_RANK)),
            const((1, HEAD_PAD)), const((1, HEAD_PAD)),
            head_w(Q_LORA_RANK, first), head_w(KV_LORA_RANK, first),
            head_w(Q_LORA_RANK, odd), head_w(KV_LORA_RANK, odd),
            head_w(Q_LORA_RANK, nxt), head_w(KV_LORA_RANK, nxt),
        ],
        out_specs=pl.BlockSpec((s, 2 * V_HEAD_DIM), lambda bi, g: (bi, g)),
        out_shape=jax.ShapeDtypeStruct((b * s, N_HEADS * V_HEAD_DIM), BF16),
        scratch_shapes=[
            rows(Q_LORA_RANK, BF16), rows(KV_LORA_RANK, BF16),
            rows(LANES, F32), rows(LANES, F32), rows(LANES, F32), rows(LANES, F32), rows(1, F32),
            rows(HEAD_PAD, BF16), rows(HEAD_PAD, BF16), rows(HEAD_PAD, BF16),
            rows(HEAD_PAD, BF16), rows(HEAD_PAD, BF16), rows(HEAD_PAD, BF16),
        ],
        compiler_params=pltpu.CompilerParams(
            dimension_semantics=("arbitrary", "arbitrary"), vmem_limit_bytes=VMEM_LIMIT),
        name="mla_attention",
    )(z, pos, freq, q_lat_g, kv_lat_g, qg_p, kg_p, wuq_p, wukv, wuq_p, wukv, wuq_p, wukv)


def _mix_kernel(attn_ref, ga0_ref, ga1_ref, cb_ref, cc_ref, cx_ref, gc_ref,
                ma0_ref, ma1_ref, mc0_ref, mc1_ref, ccp_ref, cxp_ref, ccn_ref, cxn_ref,
                cw_ref, cbias_ref, wba_ref, wbc_ref, o_ref, u_ref, *, tm, seq):
    i = pl.program_id(0)

    a = attn_ref[...] * jnp.concatenate([ga0_ref[...], ga1_ref[...]], axis=1)

    u = cc_ref[...].astype(F32) * cx_ref[...].astype(F32)
    first = (i * tm) % seq == 0
    last = ((i + 1) * tm) % seq == 0
    prev = (ccp_ref[HALO_ROWS - 1:HALO_ROWS, :].astype(F32)
            * cxp_ref[HALO_ROWS - 1:HALO_ROWS, :].astype(F32))
    nxt = ccn_ref[0:1, :].astype(F32) * cxn_ref[0:1, :].astype(F32)
    u_ref[7:8, :] = jnp.where(first, 0.0, prev)
    u_ref[8:8 + tm, :] = u
    u_ref[8 + tm:9 + tm, :] = jnp.where(last, 0.0, nxt)
    conv = (u_ref[7:7 + tm, :] * cw_ref[0:1, :] + u * cw_ref[1:2, :]
            + u_ref[9:9 + tm, :] * cw_ref[2:3, :] + cbias_ref[...])
    cv = (cb_ref[...].astype(F32) * conv * gc_ref[...].astype(F32)).astype(BF16)

    for c, (ma_ref, mc_ref) in enumerate(((ma0_ref, mc0_ref), (ma1_ref, mc1_ref))):
        cols = slice(c * SEG, (c + 1) * SEG)
        y_attn = jnp.dot(a, wba_ref[:, cols], preferred_element_type=F32)
        y_conv = jnp.dot(cv, wbc_ref[:, cols], preferred_element_type=F32)
        o_ref[:, cols] = (ma_ref[...].astype(F32) * y_attn
                          + mc_ref[...].astype(F32) * y_conv).astype(BF16)


def _mix(attn, z, conv_w, conv_b, wba, wbc, seq, tm):
    t, d = attn.shape[0], wba.shape[1]
    hb = tm // HALO_ROWS
    n_halo = t // HALO_ROWS
    seg = lambda c: pl.BlockSpec((tm, SEG), lambda i, c=c: (i, c))
    prev = lambda c: pl.BlockSpec((HALO_ROWS, SEG), lambda i, c=c: (jnp.maximum(i * hb - 1, 0), c))
    nxt = lambda c: pl.BlockSpec((HALO_ROWS, SEG),
                                 lambda i, c=c: (jnp.minimum((i + 1) * hb, n_halo - 1), c))
    resident = lambda shape: pl.BlockSpec(shape, lambda i: (0,) * len(shape),
                                          pipeline_mode=pl.Buffered(1))
    return pl.pallas_call(
        functools.partial(_mix_kernel, tm=tm, seq=seq),
        grid=(t // tm,),
        in_specs=[
            pl.BlockSpec((tm, attn.shape[1]), lambda i: (i, 0)),
            seg(1), seg(2), seg(3), seg(4), seg(5), seg(6), seg(7), seg(8), seg(9), seg(10),
            prev(4), prev(5), nxt(4), nxt(5),
            resident(conv_w.shape), resident(conv_b.shape),
            resident(wba.shape), resident(wbc.shape),
        ],
        out_specs=pl.BlockSpec((tm, d), lambda i: (i, 0)),
        out_shape=jax.ShapeDtypeStruct((t, d), BF16),
        scratch_shapes=[pltpu.VMEM((tm + 16, SEG), F32)],
        compiler_params=pltpu.CompilerParams(
            dimension_semantics=("arbitrary",), vmem_limit_bytes=VMEM_LIMIT),
        name="mix",
    )(attn, z, z, z, z, z, z, z, z, z, z, z, z, z, z, conv_w, conv_b, wba, wbc)


def _out_ple_kernel(x_ref, m_ref, p_ref, g_ref, wout_ref, wg_ref, wp_ref, o_ref, x1_ref):
    x1_ref[...] = x_ref[...] + jnp.dot(m_ref[...], wout_ref[...], preferred_element_type=F32)
    pb = p_ref[...].astype(BF16)
    x1 = x1_ref[...]
    h = (_rms(x1, x1.shape[-1]) * g_ref[...]).astype(BF16)
    for c in range(x1.shape[-1] // SEG):
        cols = slice(c * SEG, (c + 1) * SEG)
        proj = jnp.dot(pb, wp_ref[:, cols], preferred_element_type=F32)
        gate = _sigmoid(jnp.dot(h, wg_ref[:, cols], preferred_element_type=F32))
        o_ref[:, cols] = x1_ref[:, cols] + gate * proj


def _out_ple(x2, merged, p2, ple_g, wout, wg, wp, tm):
    t, d = x2.shape
    resident = lambda shape: pl.BlockSpec(shape, lambda i: (0,) * len(shape),
                                          pipeline_mode=pl.Buffered(1))
    return pl.pallas_call(
        _out_ple_kernel,
        grid=(t // tm,),
        in_specs=[
            pl.BlockSpec((tm, d), lambda i: (i, 0)),
            pl.BlockSpec((tm, d), lambda i: (i, 0)),
            pl.BlockSpec((tm, p2.shape[1]), lambda i: (i, 0)),
            resident((1, d)), resident(wout.shape), resident(wg.shape), resident(wp.shape),
        ],
        out_specs=pl.BlockSpec((tm, d), lambda i: (i, 0)),
        out_shape=jax.ShapeDtypeStruct((t, d), F32),
        scratch_shapes=[pltpu.VMEM((tm, d), F32)],
        compiler_params=pltpu.CompilerParams(
            dimension_semantics=("arbitrary",), vmem_limit_bytes=VMEM_LIMIT),
        name="out_ple",
    )(x2, merged, p2, ple_g, wout, wg, wp)


def _layer(x2, p2, pos, freq, b, s, norm_g, w_in, q_lat_g, kv_lat_g, w_uq, w_ukv,
           q_norm_g, k_norm_g, conv_w, conv_b, w_branch_attn, w_branch_conv,
           w_out, ple_norm_g, w_ple_gate, w_ple_proj):
    d = x2.shape[1]
    head_cols = Q_LORA_RANK + KV_LORA_RANK + QK_ROPE_DIM
    w_head, w_rest = _w_in_split(w_in, head_cols, tr=256)
    wuq_p = jnp.pad(w_uq.reshape(Q_LORA_RANK, N_HEADS, QK_HEAD_DIM),
                    ((0, 0), (0, 0), (0, HEAD_PAD - QK_HEAD_DIM))
                    ).reshape(Q_LORA_RANK, N_HEADS * HEAD_PAD).astype(BF16)
    pad_g = lambda g: jnp.pad(g, (0, HEAD_PAD - QK_HEAD_DIM)).reshape(1, HEAD_PAD)

    z = _in_proj(x2, norm_g.reshape(1, d), w_head, w_rest, tm=1024)
    attn = _mla_attention(z, pos, freq, q_lat_g.reshape(1, -1), kv_lat_g.reshape(1, -1),
                          wuq_p, w_ukv.astype(BF16), pad_g(q_norm_g), pad_g(k_norm_g),
                          b, s, tq=512, tp=512)
    merged = _mix(attn, z, conv_w, conv_b.reshape(1, -1), w_branch_attn.astype(BF16),
                  w_branch_conv.astype(BF16), seq=s, tm=512)
    return _out_ple(x2, merged, p2, ple_norm_g.reshape(1, d), w_out.astype(BF16),
                    w_ple_gate.astype(BF16), w_ple_proj.astype(BF16), tm=512)


def kernel(x, p, positions, norm_g, w_in, q_lat_g, kv_lat_g, w_uq, w_ukv, q_norm_g, k_norm_g, conv_w, conv_b, w_branch_attn, w_branch_conv, w_out, ple_norm_g, w_ple_gate, w_ple_proj):
    b, s, d = x.shape
    depth = p.shape[0]
    x2 = x.reshape(b * s, d)
    pos = positions.reshape(b * s, 1)
    half = QK_ROPE_DIM // 2
    inv_freq = 1.0 / (ROPE_THETA ** (jnp.arange(0, QK_ROPE_DIM, 2, dtype=F32) / QK_ROPE_DIM))
    freq = jnp.concatenate([inv_freq, inv_freq, jnp.zeros((LANES - 2 * half,), F32)]).reshape(1, LANES)
    for i in range(depth):
        x2 = _layer(x2, p[i].reshape(b * s, -1), pos, freq, b, s, norm_g[i], w_in[i],
                    q_lat_g[i], kv_lat_g[i], w_uq[i], w_ukv[i], q_norm_g[i], k_norm_g[i],
                    conv_w[i], conv_b[i], w_branch_attn[i], w_branch_conv[i], w_out[i],
                    ple_norm_g[i], w_ple_gate[i], w_ple_proj[i])
    return x2.reshape(b, s, d)
```

```python
import functools

import jax
import jax.numpy as jnp
from jax import lax
from jax.experimental import pallas as pl
from jax.experimental.pallas import tpu as pltpu

N_HEADS = 16
QK_NOPE_DIM = 128
QK_ROPE_DIM = 64
QK_HEAD_DIM = QK_NOPE_DIM + QK_ROPE_DIM
V_HEAD_DIM = 128
Q_LORA_RANK = 512
KV_LORA_RANK = 256
ROPE_THETA = 10000.0
EPS = 1e-6
LOG2E = 1.4426950408889634

LANES = 128
HEAD_PAD = 2 * LANES
SEG = 1024
HEAD_COLS = Q_LORA_RANK + KV_LORA_RANK + QK_ROPE_DIM
HEAD_COLS_ALIGN = 64
assert HEAD_COLS % HEAD_COLS_ALIGN == 0 and SEG % HEAD_COLS_ALIGN == 0
HALO_ROWS = 16
VMEM_LIMIT = 56 * 1024 * 1024

BF16 = jnp.bfloat16
F32 = jnp.float32


def _rms(v, axis_size):
    return v * lax.rsqrt(jnp.sum(v * v, axis=-1, keepdims=True) * (1.0 / axis_size) + EPS)


def _sigmoid(v):
    return 0.5 * jnp.tanh(0.5 * v) + 0.5


def _silu(v):
    return v * _sigmoid(v)


def _identity(v):
    return v


def _keep_head_cols(v):
    col = lax.broadcasted_iota(jnp.int32, v.shape, 1)
    return jnp.where(col < HEAD_COLS, v, 0.0)


Z_TILE_ACTS = (_keep_head_cols, _silu, _silu, _identity, _identity, _identity, _silu,
               _sigmoid, _sigmoid, _sigmoid, _sigmoid)


def _in_proj_kernel(x_ref, g_ref, wt_ref, z_ref, h_ref):
    j = pl.program_id(1)

    @pl.when(j == 0)
    def _():
        x = x_ref[...]
        h_ref[...] = (_rms(x, x.shape[-1]) * g_ref[...]).astype(BF16)

    variants = {}
    for tile, act in enumerate(Z_TILE_ACTS):
        variants.setdefault(act, []).append(tile)
    for act, tiles in variants.items():
        cond = functools.reduce(jnp.logical_or, [j == tile for tile in tiles])

        @pl.when(cond)
        def _(act=act):
            zt = lax.dot_general(h_ref[...], wt_ref[...].astype(BF16), (((1,), (1,)), ((), ())),
                                 preferred_element_type=F32)
            z_ref[...] = act(zt).astype(BF16)


def _in_proj(x2, norm_g, w_t, tm):
    t, d = x2.shape
    n_tiles = len(Z_TILE_ACTS)
    assert w_t.shape == (HEAD_COLS + (n_tiles - 1) * SEG, d)
    return pl.pallas_call(
        _in_proj_kernel,
        grid=(t // tm, n_tiles),
        in_specs=[
            pl.BlockSpec((tm, d), lambda i, j: (i, 0)),
            pl.BlockSpec((1, d), lambda i, j: (0, 0)),
            pl.BlockSpec((pl.Element(SEG), pl.Element(d)),
                         lambda i, j: (pl.multiple_of(
                             jnp.maximum(j * SEG - (SEG - HEAD_COLS), 0), HEAD_COLS_ALIGN), 0)),
        ],
        out_specs=pl.BlockSpec((tm, SEG), lambda i, j: (i, j)),
        out_shape=jax.ShapeDtypeStruct((t, n_tiles * SEG), BF16),
        scratch_shapes=[pltpu.VMEM((tm, d), BF16)],
        compiler_params=pltpu.CompilerParams(
            dimension_semantics=("arbitrary", "arbitrary"),
            vmem_limit_bytes=VMEM_LIMIT),
        name="in_proj",
    )(x2, norm_g, w_t)


def _mla_kernel(z_ref, pos_ref, freq_ref, qlg_ref, kvlg_ref, qg_ref, kg_ref,
                wuq0_ref, wukv0_ref, wuq1_ref, wukv1_ref, wuq2_ref, wukv2_ref,
                o_ref,
                qn_ref, kvn_ref, cos_ref, sinlo_ref, sinhi_ref, krope_ref, kss_ref,
                qa_ref, ka_ref, va_ref, qb_ref, kb_ref, vb_ref, *, tq, tp):
    s_len = z_ref.shape[0]
    n_chunks = s_len // tq
    slot_a = (qa_ref, ka_ref, va_ref)
    slot_b = (qb_ref, kb_ref, vb_ref)
    half = QK_ROPE_DIM // 2
    scale = QK_HEAD_DIM ** -0.5 * LOG2E
    inv_d = 1.0 / QK_HEAD_DIM
    qg_nope, qg_rope = qg_ref[:, 0:LANES], qg_ref[:, LANES:HEAD_PAD]
    kg_nope, kg_rope = kg_ref[:, 0:LANES], kg_ref[:, LANES:HEAD_PAD]

    def rope(u, rows):
        return (u * cos_ref[rows, :] + pltpu.roll(u, LANES - half, axis=1) * sinlo_ref[rows, :]
                + pltpu.roll(u, half, axis=1) * sinhi_ref[rows, :])

    def prep_rows(wuq_ref, wukv_ref, q_out, k_out, v_out, r):
        rows = slice(r * tq, (r + 1) * tq)
        q = jnp.dot(qn_ref[rows, :], wuq_ref[...], preferred_element_type=F32)
        kv = jnp.dot(kvn_ref[rows, :], wukv_ref[...], preferred_element_type=F32)
        q_nope, q_rope = q[:, 0:LANES], q[:, LANES:HEAD_PAD]
        ss = jnp.sum(q_nope * q_nope + q_rope * q_rope, axis=-1, keepdims=True)
        rq = lax.rsqrt(ss * inv_d + EPS) * scale
        q_out[rows, 0:LANES] = (q_nope * rq * qg_nope).astype(BF16)
        q_out[rows, LANES:HEAD_PAD] = (rope(q_rope * qg_rope, rows) * rq).astype(BF16)
        k_nope = kv[:, 0:LANES]
        ssk = jnp.sum(k_nope * k_nope, axis=-1, keepdims=True) + kss_ref[rows, :]
        rk = lax.rsqrt(ssk * inv_d + EPS)
        k_out[rows, 0:LANES] = (k_nope * rk * kg_nope).astype(BF16)
        k_out[rows, LANES:HEAD_PAD] = (krope_ref[rows, :] * rk).astype(BF16)
        v_out[rows, 0:LANES] = kv[:, LANES:HEAD_PAD].astype(BF16)

    def scores(slot, i):
        q_in, k_in, _ = slot
        return lax.dot_general(q_in[i * tq:(i + 1) * tq, :], k_in[...], (((1,), (1,)), ((), ())),
                               preferred_element_type=F32)

    def finish(sc, slot, col, i):
        m = jnp.max(sc, axis=-1, keepdims=True)
        p = jnp.exp2(sc - m).astype(BF16)
        o = jnp.dot(p, slot[2][...], preferred_element_type=F32)
        o_ref[i * tq:(i + 1) * tq, col * LANES:(col + 1) * LANES] = (
            o[:, :V_HEAD_DIM] / o[:, V_HEAD_DIM:V_HEAD_DIM + 1]).astype(BF16)

    @pl.when(pl.program_id(1) == 0)
    def _():
        for r in range(s_len // tp):
            rows = slice(r * tp, (r + 1) * tp)
            qa = z_ref[rows, 0:Q_LORA_RANK].astype(F32)
            kva = z_ref[rows, Q_LORA_RANK:Q_LORA_RANK + KV_LORA_RANK].astype(F32)
            kpe = z_ref[rows, Q_LORA_RANK + KV_LORA_RANK:Q_LORA_RANK + KV_LORA_RANK + LANES
                        ].astype(F32)
            qn_ref[rows, :] = (_rms(qa, Q_LORA_RANK) * qlg_ref[...]).astype(BF16)
            kvn_ref[rows, :] = (_rms(kva, KV_LORA_RANK) * kvlg_ref[...]).astype(BF16)
            ang = pos_ref[rows, :].astype(F32) * freq_ref[...]
            sinv = jnp.sin(ang)
            lane = lax.broadcasted_iota(jnp.int32, ang.shape, 1)
            cos_ref[rows, :] = jnp.cos(ang)
            sinlo_ref[rows, :] = jnp.where(lane < half, -sinv, 0.0)
            sinhi_ref[rows, :] = jnp.where((lane >= half) & (lane < QK_ROPE_DIM), sinv, 0.0)
            krope_ref[rows, :] = rope(kpe * kg_rope, rows)
            kss_ref[rows, :] = jnp.sum(kpe * kpe, axis=-1, keepdims=True)
            ones_col = jnp.where(lane == 0, 1.0, 0.0).astype(BF16)
            va_ref[rows, LANES:HEAD_PAD] = ones_col
            vb_ref[rows, LANES:HEAD_PAD] = ones_col
        for r in range(n_chunks):
            prep_rows(wuq0_ref, wukv0_ref, *slot_a, r)

    preps = ((wuq1_ref, wukv1_ref) + slot_b, (wuq2_ref, wukv2_ref) + slot_a)
    slots = (slot_a, slot_b)
    total = 2 * n_chunks
    sc_next = scores(slot_a, 0)
    for t in range(total):
        hd, i = divmod(t, n_chunks)
        sc_cur = sc_next
        prep_rows(*preps[hd], i)
        if t + 1 < total:
            hd2, i2 = divmod(t + 1, n_chunks)
            sc_next = scores(slots[hd2], i2)
        finish(sc_cur, slots[hd], hd, i)


def _mla_attention(z, pos, freq, q_lat_g, kv_lat_g, wuq_p, wukv, qg_p, kg_p, b, s, tq, tp):
    pairs = N_HEADS // 2
    const = lambda shape: pl.BlockSpec(shape, lambda bi, g: (0,) * len(shape))
    head_w = lambda rank, f: pl.BlockSpec((rank, HEAD_PAD), lambda bi, g: (0, f(g)))
    first = lambda g: 0
    odd = lambda g: 2 * g + 1
    nxt = lambda g: jnp.minimum(2 * g + 2, N_HEADS - 1)
    rows = lambda w, dt: pltpu.VMEM((s, w), dt)
    return pl.pallas_call(
        functools.partial(_mla_kernel, tq=tq, tp=tp),
        grid=(b, pairs),
        in_specs=[
            pl.BlockSpec((s, SEG), lambda bi, g: (bi, 0)),
            pl.BlockSpec((s, 1), lambda bi, g: (bi, 0)),
            const((1, LANES)), const((1, Q_LORA_RANK)), const((1, KV_LORA_RANK)),
            const((1, HEAD_PAD)), const((1, HEAD_PAD)),
            head_w(Q_LORA_RANK, first), head_w(KV_LORA_RANK, first),
            head_w(Q_LORA_RANK, odd), head_w(KV_LORA_RANK, odd),
            head_w(Q_LORA_RANK, nxt), head_w(KV_LORA_RANK, nxt),
        ],
        out_specs=pl.BlockSpec((s, 2 * V_HEAD_DIM), lambda bi, g: (bi, g)),
        out_shape=jax.ShapeDtypeStruct((b * s, N_HEADS * V_HEAD_DIM), BF16),
        scratch_shapes=[
            rows(Q_LORA_RANK, BF16), rows(KV_LORA_RANK, BF16),
            rows(LANES, F32), rows(LANES, F32), rows(LANES, F32), rows(LANES, F32), rows(1, F32),
            rows(HEAD_PAD, BF16), rows(HEAD_PAD, BF16), rows(HEAD_PAD, BF16),
            rows(HEAD_PAD, BF16), rows(HEAD_PAD, BF16), rows(HEAD_PAD, BF16),
        ],
        compiler_params=pltpu.CompilerParams(
            dimension_semantics=("arbitrary", "arbitrary"), vmem_limit_bytes=VMEM_LIMIT),
        name="mla_attention",
    )(z, pos, freq, q_lat_g, kv_lat_g, qg_p, kg_p, wuq_p, wukv, wuq_p, wukv, wuq_p, wukv)


def _mix_kernel(attn_ref, ga0_ref, ga1_ref, cb_ref, cc_ref, cx_ref, gc_ref,
                ma0_ref, ma1_ref, mc0_ref, mc1_ref, ccp_ref, cxp_ref, ccn_ref, cxn_ref,
                cw_ref, cbias_ref, wba_ref, wbc_ref, o_ref, u_ref, *, tm, seq):
    i = pl.program_id(0)

    a = attn_ref[...] * jnp.concatenate([ga0_ref[...], ga1_ref[...]], axis=1)

    u = cc_ref[...].astype(F32) * cx_ref[...].astype(F32)
    first = (i * tm) % seq == 0
    last = ((i + 1) * tm) % seq == 0
    prev = (ccp_ref[HALO_ROWS - 1:HALO_ROWS, :].astype(F32)
            * cxp_ref[HALO_ROWS - 1:HALO_ROWS, :].astype(F32))
    nxt = ccn_ref[0:1, :].astype(F32) * cxn_ref[0:1, :].astype(F32)
    u_ref[7:8, :] = jnp.where(first, 0.0, prev)
    u_ref[8:8 + tm, :] = u
    u_ref[8 + tm:9 + tm, :] = jnp.where(last, 0.0, nxt)
    conv = (u_ref[7:7 + tm, :] * cw_ref[0:1, :] + u * cw_ref[1:2, :]
            + u_ref[9:9 + tm, :] * cw_ref[2:3, :] + cbias_ref[...])
    cv = (cb_ref[...].astype(F32) * conv * gc_ref[...].astype(F32)).astype(BF16)

    for c, (ma_ref, mc_ref) in enumerate(((ma0_ref, mc0_ref), (ma1_ref, mc1_ref))):
        cols = slice(c * SEG, (c + 1) * SEG)
        y_attn = jnp.dot(a, wba_ref[:, cols], preferred_element_type=F32)
        y_conv = jnp.dot(cv, wbc_ref[:, cols], preferred_element_type=F32)
        o_ref[:, cols] = (ma_ref[...].astype(F32) * y_attn
                          + mc_ref[...].astype(F32) * y_conv).astype(BF16)


def _mix(attn, z, conv_w, conv_b, wba, wbc, seq, tm):
    t, d = attn.shape[0], wba.shape[1]
    hb = tm // HALO_ROWS
    n_halo = t // HALO_ROWS
    seg = lambda c: pl.BlockSpec((tm, SEG), lambda i, c=c: (i, c))
    prev = lambda c: pl.BlockSpec((HALO_ROWS, SEG), lambda i, c=c: (jnp.maximum(i * hb - 1, 0), c))
    nxt = lambda c: pl.BlockSpec((HALO_ROWS, SEG),
                                 lambda i, c=c: (jnp.minimum((i + 1) * hb, n_halo - 1), c))
    resident = lambda shape: pl.BlockSpec(shape, lambda i: (0,) * len(shape),
                                          pipeline_mode=pl.Buffered(1))
    return pl.pallas_call(
        functools.partial(_mix_kernel, tm=tm, seq=seq),
        grid=(t // tm,),
        in_specs=[
            pl.BlockSpec((tm, attn.shape[1]), lambda i: (i, 0)),
            seg(1), seg(2), seg(3), seg(4), seg(5), seg(6), seg(7), seg(8), seg(9), seg(10),
            prev(4), prev(5), nxt(4), nxt(5),
            resident(conv_w.shape), resident(conv_b.shape),
            resident(wba.shape), resident(wbc.shape),
        ],
        out_specs=pl.BlockSpec((tm, d), lambda i: (i, 0)),
        out_shape=jax.ShapeDtypeStruct((t, d), BF16),
        scratch_shapes=[pltpu.VMEM((tm + 16, SEG), F32)],
        compiler_params=pltpu.CompilerParams(
            dimension_semantics=("arbitrary",), vmem_limit_bytes=VMEM_LIMIT),
        name="mix",
    )(attn, z, z, z, z, z, z, z, z, z, z, z, z, z, z, conv_w, conv_b, wba, wbc)


def _out_ple_kernel(x_ref, m_ref, p_ref, g_ref, wout_ref, wg_ref, wp_ref, o_ref, x1_ref):
    x1_ref[...] = x_ref[...] + jnp.dot(m_ref[...], wout_ref[...], preferred_element_type=F32)
    pb = p_ref[...].astype(BF16)
    x1 = x1_ref[...]
    h = (_rms(x1, x1.shape[-1]) * g_ref[...]).astype(BF16)
    for c in range(x1.shape[-1] // SEG):
        cols = slice(c * SEG, (c + 1) * SEG)
        proj = jnp.dot(pb, wp_ref[:, cols], preferred_element_type=F32)
        gate = _sigmoid(jnp.dot(h, wg_ref[:, cols], preferred_element_type=F32))
        o_ref[:, cols] = x1_ref[:, cols] + gate * proj


def _out_ple(x2, merged, p2, ple_g, wout, wg, wp, tm):
    t, d = x2.shape
    resident = lambda shape: pl.BlockSpec(shape, lambda i: (0,) * len(shape),
                                          pipeline_mode=pl.Buffered(1))
    return pl.pallas_call(
        _out_ple_kernel,
        grid=(t // tm,),
        in_specs=[
            pl.BlockSpec((tm, d), lambda i: (i, 0)),
            pl.BlockSpec((tm, d), lambda i: (i, 0)),
            pl.BlockSpec((tm, p2.shape[1]), lambda i: (i, 0)),
            resident((1, d)), resident(wout.shape), resident(wg.shape), resident(wp.shape),
        ],
        out_specs=pl.BlockSpec((tm, d), lambda i: (i, 0)),
        out_shape=jax.ShapeDtypeStruct((t, d), F32),
        scratch_shapes=[pltpu.VMEM((tm, d), F32)],
        compiler_params=pltpu.CompilerParams(
            dimension_semantics=("arbitrary",), vmem_limit_bytes=VMEM_LIMIT),
        name="out_ple",
    )(x2, merged, p2, ple_g, wout, wg, wp)


def _layer(x2, p2, pos, freq, b, s, norm_g, w_in, q_lat_g, kv_lat_g, w_uq, w_ukv,
           q_norm_g, k_norm_g, conv_w, conv_b, w_branch_attn, w_branch_conv,
           w_out, ple_norm_g, w_ple_gate, w_ple_proj):
    d = x2.shape[1]
    wuq_p = jnp.pad(w_uq.reshape(Q_LORA_RANK, N_HEADS, QK_HEAD_DIM),
                    ((0, 0), (0, 0), (0, HEAD_PAD - QK_HEAD_DIM))
                    ).reshape(Q_LORA_RANK, N_HEADS * HEAD_PAD).astype(BF16)
    pad_g = lambda g: jnp.pad(g, (0, HEAD_PAD - QK_HEAD_DIM)).reshape(1, HEAD_PAD)

    z = _in_proj(x2, norm_g.reshape(1, d), jnp.swapaxes(w_in, 0, 1), tm=1024)
    attn = _mla_attention(z, pos, freq, q_lat_g.reshape(1, -1), kv_lat_g.reshape(1, -1),
                          wuq_p, w_ukv.astype(BF16), pad_g(q_norm_g), pad_g(k_norm_g),
                          b, s, tq=512, tp=512)
    merged = _mix(attn, z, conv_w, conv_b.reshape(1, -1), w_branch_attn.astype(BF16),
                  w_branch_conv.astype(BF16), seq=s, tm=512)
    return _out_ple(x2, merged, p2, ple_norm_g.reshape(1, d), w_out.astype(BF16),
                    w_ple_gate.astype(BF16), w_ple_proj.astype(BF16), tm=512)


def kernel(x, p, positions, norm_g, w_in, q_lat_g, kv_lat_g, w_uq, w_ukv, q_norm_g, k_norm_g, conv_w, conv_b, w_branch_attn, w_branch_conv, w_out, ple_norm_g, w_ple_gate, w_ple_proj):
    b, s, d = x.shape
    depth = p.shape[0]
    x2 = x.reshape(b * s, d)
    pos = positions.reshape(b * s, 1)
    half = QK_ROPE_DIM // 2
    inv_freq = 1.0 / (ROPE_THETA ** (jnp.arange(0, QK_ROPE_DIM, 2, dtype=F32) / QK_ROPE_DIM))
    freq = jnp.concatenate([inv_freq, inv_freq, jnp.zeros((LANES - 2 * half,), F32)]).reshape(1, LANES)
    for i in range(depth):
        x2 = _layer(x2, p[i].reshape(b * s, -1), pos, freq, b, s, norm_g[i], w_in[i],
                    q_lat_g[i], kv_lat_g[i], w_uq[i], w_ukv[i], q_norm_g[i], k_norm_g[i],
                    conv_w[i], conv_b[i], w_branch_attn[i], w_branch_conv[i], w_out[i],
                    ple_norm_g[i], w_ple_gate[i], w_ple_proj[i])
    return x2.reshape(b, s, d)
```

```python
import functools

import jax
import jax.numpy as jnp
from jax import lax
from jax.experimental import pallas as pl
from jax.experimental.pallas import tpu as pltpu

N_HEADS = 16
QK_NOPE_DIM = 128
QK_ROPE_DIM = 64
QK_HEAD_DIM = QK_NOPE_DIM + QK_ROPE_DIM
V_HEAD_DIM = 128
Q_LORA_RANK = 512
KV_LORA_RANK = 256
ROPE_THETA = 10000.0
EPS = 1e-6
LOG2E = 1.4426950408889634

LANES = 128
HEAD_PAD = 2 * LANES
SEG = 1024
HEAD_COLS = Q_LORA_RANK + KV_LORA_RANK + QK_ROPE_DIM
HEAD_COLS_ALIGN = 64
assert HEAD_COLS % HEAD_COLS_ALIGN == 0 and SEG % HEAD_COLS_ALIGN == 0
HALO_ROWS = 16
VMEM_LIMIT = 56 * 1024 * 1024

BF16 = jnp.bfloat16
F32 = jnp.float32


def _rms(v, axis_size):
    return v * lax.rsqrt(jnp.sum(v * v, axis=-1, keepdims=True) * (1.0 / axis_size) + EPS)


def _sigmoid(v):
    return 0.5 * jnp.tanh(0.5 * v) + 0.5


def _silu(v):
    return v * _sigmoid(v)


def _identity(v):
    return v


def _keep_head_cols(v):
    col = lax.broadcasted_iota(jnp.int32, v.shape, 1)
    return jnp.where(col < HEAD_COLS, v, 0.0)


Z_TILE_ACTS = (_keep_head_cols, _silu, _silu, _identity, _identity, _identity, _silu,
               _sigmoid, _sigmoid, _sigmoid, _sigmoid)
ROT_TILE = 3


def _in_proj_kernel(x_ref, g_ref, wt_ref, pos_ref, freq_ref, z_ref, rot_ref, h_ref):
    j = pl.program_id(1)

    @pl.when(j == 0)
    def _():
        x = x_ref[...]
        h_ref[...] = (_rms(x, x.shape[-1]) * g_ref[...]).astype(BF16)

    def rotary_tables():
        half = QK_ROPE_DIM // 2
        groups = LANES // half
        rq = x_ref.shape[0] // groups
        lane = lax.broadcasted_iota(jnp.int32, (rq, LANES), 1)
        pos = jnp.zeros((rq, LANES), F32)
        for q in range(groups):
            pos = jnp.where(lane // half == q, pos_ref[q * rq:(q + 1) * rq, :].astype(F32), pos)
        ang = pos * freq_ref[...]
        cosv, sinv = jnp.cos(ang), jnp.sin(ang)
        for q in range(groups):
            rows = slice(q * rq, (q + 1) * rq)
            to_x1 = (lambda v: v) if q == 0 else (
                lambda v: pltpu.roll(v, LANES - q * half, axis=1))
            to_x2 = lambda v: pltpu.roll(v, (LANES - q * half + half) % LANES, axis=1)
            rot_ref[rows, 0:LANES] = jnp.where(lane < half, to_x1(cosv), to_x2(cosv))
            rot_ref[rows, LANES:2 * LANES] = jnp.where(lane < half, -to_x1(sinv), 0.0)
            rot_ref[rows, 2 * LANES:3 * LANES] = jnp.where(
                (lane >= half) & (lane < QK_ROPE_DIM), to_x2(sinv), 0.0)

    variants = {}
    for tile, act in enumerate(Z_TILE_ACTS):
        variants.setdefault((act, tile == ROT_TILE), []).append(tile)
    for (act, with_tables), tiles in variants.items():
        cond = functools.reduce(jnp.logical_or, [j == tile for tile in tiles])

        @pl.when(cond)
        def _(act=act, with_tables=with_tables):
            zt = lax.dot_general(h_ref[...], wt_ref[...].astype(BF16), (((1,), (1,)), ((), ())),
                                 preferred_element_type=F32)
            z_ref[...] = act(zt).astype(BF16)
            if with_tables:
                rotary_tables()


def _in_proj(x2, norm_g, w_t, pos, freq, tm):
    t, d = x2.shape
    n_tiles = len(Z_TILE_ACTS)
    assert w_t.shape == (HEAD_COLS + (n_tiles - 1) * SEG, d)
    return pl.pallas_call(
        _in_proj_kernel,
        grid=(t // tm, n_tiles),
        in_specs=[
            pl.BlockSpec((tm, d), lambda i, j: (i, 0)),
            pl.BlockSpec((1, d), lambda i, j: (0, 0)),
            pl.BlockSpec((pl.Element(SEG), pl.Element(d)),
                         lambda i, j: (pl.multiple_of(
                             jnp.maximum(j * SEG - (SEG - HEAD_COLS), 0), HEAD_COLS_ALIGN), 0)),
            pl.BlockSpec((tm, 1), lambda i, j: (i, 0)),
            pl.BlockSpec((1, LANES), lambda i, j: (0, 0)),
        ],
        out_specs=[pl.BlockSpec((tm, SEG), lambda i, j: (i, j)),
                   pl.BlockSpec((tm, 3 * LANES), lambda i, j: (i, 0))],
        out_shape=[jax.ShapeDtypeStruct((t, n_tiles * SEG), BF16),
                   jax.ShapeDtypeStruct((t, 3 * LANES), F32)],
        scratch_shapes=[pltpu.VMEM((tm, d), BF16)],
        compiler_params=pltpu.CompilerParams(
            dimension_semantics=("arbitrary", "arbitrary"),
            vmem_limit_bytes=VMEM_LIMIT),
        name="in_proj",
    )(x2, norm_g, w_t, pos, freq)


def _mla_kernel(z_ref, rot_ref, qlg_ref, kvlg_ref, qg_ref, kg_ref,
                wuq0_ref, wukv0_ref, wuq1_ref, wukv1_ref, wuq2_ref, wukv2_ref,
                o_ref,
                qn_ref, kvn_ref, krope_ref, kss_ref,
                qa_ref, ka_ref, va_ref, qb_ref, kb_ref, vb_ref, *, tq, tp):
    s_len = z_ref.shape[0]
    n_chunks = s_len // tq
    slot_a = (qa_ref, ka_ref, va_ref)
    slot_b = (qb_ref, kb_ref, vb_ref)
    half = QK_ROPE_DIM // 2
    scale = QK_HEAD_DIM ** -0.5 * LOG2E
    inv_d = 1.0 / QK_HEAD_DIM
    qg_nope, qg_rope = qg_ref[:, 0:LANES], qg_ref[:, LANES:HEAD_PAD]
    kg_nope, kg_rope = kg_ref[:, 0:LANES], kg_ref[:, LANES:HEAD_PAD]

    def rope(u, rows):
        return (u * rot_ref[rows, 0:LANES]
                + pltpu.roll(u, LANES - half, axis=1) * rot_ref[rows, LANES:2 * LANES]
                + pltpu.roll(u, half, axis=1) * rot_ref[rows, 2 * LANES:3 * LANES])

    def prep_rows(wuq_ref, wukv_ref, q_out, k_out, v_out, r):
        rows = slice(r * tq, (r + 1) * tq)
        q = jnp.dot(qn_ref[rows, :], wuq_ref[...], preferred_element_type=F32)
        kv = jnp.dot(kvn_ref[rows, :], wukv_ref[...], preferred_element_type=F32)
        q_nope, q_rope = q[:, 0:LANES], q[:, LANES:HEAD_PAD]
        ss = jnp.sum(q_nope * q_nope + q_rope * q_rope, axis=-1, keepdims=True)
        rq = lax.rsqrt(ss * inv_d + EPS) * scale
        q_out[rows, 0:LANES] = (q_nope * rq * qg_nope).astype(BF16)
        q_out[rows, LANES:HEAD_PAD] = (rope(q_rope * qg_rope, rows) * rq).astype(BF16)
        k_nope = kv[:, 0:LANES]
        ssk = jnp.sum(k_nope * k_nope, axis=-1, keepdims=True) + kss_ref[rows, :]
        rk = lax.rsqrt(ssk * inv_d + EPS)
        k_out[rows, 0:LANES] = (k_nope * rk * kg_nope).astype(BF16)
        k_out[rows, LANES:HEAD_PAD] = (krope_ref[rows, :] * rk).astype(BF16)
        v_out[rows, 0:LANES] = kv[:, LANES:HEAD_PAD].astype(BF16)

    def scores(slot, i):
        q_in, k_in, _ = slot
        return lax.dot_general(q_in[i * tq:(i + 1) * tq, :], k_in[...], (((1,), (1,)), ((), ())),
                               preferred_element_type=F32)

    def finish(sc, slot, col, i):
        m = jnp.max(sc, axis=-1, keepdims=True)
        p = jnp.exp2(sc - m).astype(BF16)
        o = jnp.dot(p, slot[2][...], preferred_element_type=F32)
        o_ref[i * tq:(i + 1) * tq, col * LANES:(col + 1) * LANES] = (
            o[:, :V_HEAD_DIM] / o[:, V_HEAD_DIM:V_HEAD_DIM + 1]).astype(BF16)

    @pl.when(pl.program_id(1) == 0)
    def _():
        for r in range(s_len // tp):
            rows = slice(r * tp, (r + 1) * tp)
            qa = z_ref[rows, 0:Q_LORA_RANK].astype(F32)
            kva = z_ref[rows, Q_LORA_RANK:Q_LORA_RANK + KV_LORA_RANK].astype(F32)
            kpe = z_ref[rows, Q_LORA_RANK + KV_LORA_RANK:Q_LORA_RANK + KV_LORA_RANK + LANES
                        ].astype(F32)
            qn_ref[rows, :] = (_rms(qa, Q_LORA_RANK) * qlg_ref[...]).astype(BF16)
            kvn_ref[rows, :] = (_rms(kva, KV_LORA_RANK) * kvlg_ref[...]).astype(BF16)
            krope_ref[rows, :] = rope(kpe * kg_rope, rows)
            kss_ref[rows, :] = jnp.sum(kpe * kpe, axis=-1, keepdims=True)
            lane = lax.broadcasted_iota(jnp.int32, (tp, LANES), 1)
            ones_col = jnp.where(lane == 0, 1.0, 0.0).astype(BF16)
            va_ref[rows, LANES:HEAD_PAD] = ones_col
            vb_ref[rows, LANES:HEAD_PAD] = ones_col
        for r in range(n_chunks):
            prep_rows(wuq0_ref, wukv0_ref, *slot_a, r)

    preps = ((wuq1_ref, wukv1_ref) + slot_b, (wuq2_ref, wukv2_ref) + slot_a)
    slots = (slot_a, slot_b)
    total = 2 * n_chunks
    sc_next = scores(slot_a, 0)
    for t in range(total):
        hd, i = divmod(t, n_chunks)
        sc_cur = sc_next
        prep_rows(*preps[hd], i)
        if t + 1 < total:
            hd2, i2 = divmod(t + 1, n_chunks)
            sc_next = scores(slots[hd2], i2)
        finish(sc_cur, slots[hd], hd, i)


def _mla_attention(z, rot, q_lat_g, kv_lat_g, wuq_p, wukv, qg_p, kg_p, b, s, tq, tp):
    pairs = N_HEADS // 2
    const = lambda shape: pl.BlockSpec(shape, lambda bi, g: (0,) * len(shape))
    head_w = lambda rank, f: pl.BlockSpec((rank, HEAD_PAD), lambda bi, g: (0, f(g)))
    first = lambda g: 0
    odd = lambda g: 2 * g + 1
    nxt = lambda g: jnp.minimum(2 * g + 2, N_HEADS - 1)
    rows = lambda w, dt: pltpu.VMEM((s, w), dt)
    return pl.pallas_call(
        functools.partial(_mla_kernel, tq=tq, tp=tp),
        grid=(b, pairs),
        in_specs=[
            pl.BlockSpec((s, SEG), lambda bi, g: (bi, 0)),
            pl.BlockSpec((s, 3 * LANES), lambda bi, g: (bi, 0)),
            const((1, Q_LORA_RANK)), const((1, KV_LORA_RANK)),
            const((1, HEAD_PAD)), const((1, HEAD_PAD)),
            head_w(Q_LORA_RANK, first), head_w(KV_LORA_RANK, first),
            head_w(Q_LORA_RANK, odd), head_w(KV_LORA_RANK, odd),
            head_w(Q_LORA_RANK, nxt), head_w(KV_LORA_RANK, nxt),
        ],
        out_specs=pl.BlockSpec((s, 2 * V_HEAD_DIM), lambda bi, g: (bi, g)),
        out_shape=jax.ShapeDtypeStruct((b * s, N_HEADS * V_HEAD_DIM), BF16),
        scratch_shapes=[
            rows(Q_LORA_RANK, BF16), rows(KV_LORA_RANK, BF16),
            rows(LANES, F32), rows(1, F32),
            rows(HEAD_PAD, BF16), rows(HEAD_PAD, BF16), rows(HEAD_PAD, BF16),
            rows(HEAD_PAD, BF16), rows(HEAD_PAD, BF16), rows(HEAD_PAD, BF16),
        ],
        compiler_params=pltpu.CompilerParams(
            dimension_semantics=("arbitrary", "arbitrary"), vmem_limit_bytes=VMEM_LIMIT),
        name="mla_attention",
    )(z, rot, q_lat_g, kv_lat_g, qg_p, kg_p, wuq_p, wukv, wuq_p, wukv, wuq_p, wukv)


def _mix_kernel(attn_ref, ga0_ref, ga1_ref, cb_ref, cc_ref, cx_ref, gc_ref,
                ma0_ref, ma1_ref, mc0_ref, mc1_ref, ccp_ref, cxp_ref, ccn_ref, cxn_ref,
                cw_ref, cbias_ref, wba_ref, wbc_ref, o_ref, u_ref, *, tm, seq):
    i = pl.program_id(0)

    a = attn_ref[...] * jnp.concatenate([ga0_ref[...], ga1_ref[...]], axis=1)

    u = cc_ref[...].astype(F32) * cx_ref[...].astype(F32)
    first = (i * tm) % seq == 0
    last = ((i + 1) * tm) % seq == 0
    prev = (ccp_ref[HALO_ROWS - 1:HALO_ROWS, :].astype(F32)
            * cxp_ref[HALO_ROWS - 1:HALO_ROWS, :].astype(F32))
    nxt = ccn_ref[0:1, :].astype(F32) * cxn_ref[0:1, :].astype(F32)
    u_ref[7:8, :] = jnp.where(first, 0.0, prev)
    u_ref[8:8 + tm, :] = u
    u_ref[8 + tm:9 + tm, :] = jnp.where(last, 0.0, nxt)
    conv = (u_ref[7:7 + tm, :] * cw_ref[0:1, :] + u * cw_ref[1:2, :]
            + u_ref[9:9 + tm, :] * cw_ref[2:3, :] + cbias_ref[...])
    cv = (cb_ref[...].astype(F32) * conv * gc_ref[...].astype(F32)).astype(BF16)

    for c, (ma_ref, mc_ref) in enumerate(((ma0_ref, mc0_ref), (ma1_ref, mc1_ref))):
        cols = slice(c * SEG, (c + 1) * SEG)
        y_attn = jnp.dot(a, wba_ref[:, cols], preferred_element_type=F32)
        y_conv = jnp.dot(cv, wbc_ref[:, cols], preferred_element_type=F32)
        o_ref[:, cols] = (ma_ref[...].astype(F32) * y_attn
                          + mc_ref[...].astype(F32) * y_conv).astype(BF16)


def _mix(attn, z, conv_w, conv_b, wba, wbc, seq, tm):
    t, d = attn.shape[0], wba.shape[1]
    hb = tm // HALO_ROWS
    n_halo = t // HALO_ROWS
    seg = lambda c: pl.BlockSpec((tm, SEG), lambda i, c=c: (i, c))
    prev = lambda c: pl.BlockSpec((HALO_ROWS, SEG), lambda i, c=c: (jnp.maximum(i * hb - 1, 0), c))
    nxt = lambda c: pl.BlockSpec((HALO_ROWS, SEG),
                                 lambda i, c=c: (jnp.minimum((i + 1) * hb, n_halo - 1), c))
    resident = lambda shape: pl.BlockSpec(shape, lambda i: (0,) * len(shape),
                                          pipeline_mode=pl.Buffered(1))
    return pl.pallas_call(
        functools.partial(_mix_kernel, tm=tm, seq=seq),
        grid=(t // tm,),
        in_specs=[
            pl.BlockSpec((tm, attn.shape[1]), lambda i: (i, 0)),
            seg(1), seg(2), seg(3), seg(4), seg(5), seg(6), seg(7), seg(8), seg(9), seg(10),
            prev(4), prev(5), nxt(4), nxt(5),
            resident(conv_w.shape), resident(conv_b.shape),
            resident(wba.shape), resident(wbc.shape),
        ],
        out_specs=pl.BlockSpec((tm, d), lambda i: (i, 0)),
        out_shape=jax.ShapeDtypeStruct((t, d), BF16),
        scratch_shapes=[pltpu.VMEM((tm + 16, SEG), F32)],
        compiler_params=pltpu.CompilerParams(
            dimension_semantics=("arbitrary",), vmem_limit_bytes=VMEM_LIMIT),
        name="mix",
    )(attn, z, z, z, z, z, z, z, z, z, z, z, z, z, z, conv_w, conv_b, wba, wbc)


def _out_ple_kernel(x_ref, m_ref, p_ref, g_ref, wout_ref, wg_ref, wp_ref, o_ref, x1_ref):
    x1_ref[...] = x_ref[...] + jnp.dot(m_ref[...], wout_ref[...], preferred_element_type=F32)
    pb = p_ref[...].astype(BF16)
    x1 = x1_ref[...]
    h = (_rms(x1, x1.shape[-1]) * g_ref[...]).astype(BF16)
    for c in range(x1.shape[-1] // SEG):
        cols = slice(c * SEG, (c + 1) * SEG)
        proj = jnp.dot(pb, wp_ref[:, cols], preferred_element_type=F32)
        gate = _sigmoid(jnp.dot(h, wg_ref[:, cols], preferred_element_type=F32))
        o_ref[:, cols] = x1_ref[:, cols] + gate * proj


def _out_ple(x2, merged, p2, ple_g, wout, wg, wp, tm):
    t, d = x2.shape
    resident = lambda shape: pl.BlockSpec(shape, lambda i: (0,) * len(shape),
                                          pipeline_mode=pl.Buffered(1))
    return pl.pallas_call(
        _out_ple_kernel,
        grid=(t // tm,),
        in_specs=[
            pl.BlockSpec((tm, d), lambda i: (i, 0)),
            pl.BlockSpec((tm, d), lambda i: (i, 0)),
            pl.BlockSpec((tm, p2.shape[1]), lambda i: (i, 0)),
            resident((1, d)), resident(wout.shape), resident(wg.shape), resident(wp.shape),
        ],
        out_specs=pl.BlockSpec((tm, d), lambda i: (i, 0)),
        out_shape=jax.ShapeDtypeStruct((t, d), F32),
        scratch_shapes=[pltpu.VMEM((tm, d), F32)],
        compiler_params=pltpu.CompilerParams(
            dimension_semantics=("arbitrary",), vmem_limit_bytes=VMEM_LIMIT),
        name="out_ple",
    )(x2, merged, p2, ple_g, wout, wg, wp)


def _layer(x2, p2, pos, freq, b, s, norm_g, w_in, q_lat_g, kv_lat_g, w_uq, w_ukv,
           q_norm_g, k_norm_g, conv_w, conv_b, w_branch_attn, w_branch_conv,
           w_out, ple_norm_g, w_ple_gate, w_ple_proj):
    d = x2.shape[1]
    wuq_p = jnp.pad(w_uq.reshape(Q_LORA_RANK, N_HEADS, QK_HEAD_DIM),
                    ((0, 0), (0, 0), (0, HEAD_PAD - QK_HEAD_DIM))
                    ).reshape(Q_LORA_RANK, N_HEADS * HEAD_PAD).astype(BF16)
    pad_g = lambda g: jnp.pad(g, (0, HEAD_PAD - QK_HEAD_DIM)).reshape(1, HEAD_PAD)

    z, rot = _in_proj(x2, norm_g.reshape(1, d), jnp.swapaxes(w_in, 0, 1), pos, freq, tm=1024)
    attn = _mla_attention(z, rot, q_lat_g.reshape(1, -1), kv_lat_g.reshape(1, -1),
                          wuq_p, w_ukv.astype(BF16), pad_g(q_norm_g), pad_g(k_norm_g),
                          b, s, tq=512, tp=512)
    merged = _mix(attn, z, conv_w, conv_b.reshape(1, -1), w_branch_attn.astype(BF16),
                  w_branch_conv.astype(BF16), seq=s, tm=512)
    return _out_ple(x2, merged, p2, ple_norm_g.reshape(1, d), w_out.astype(BF16),
                    w_ple_gate.astype(BF16), w_ple_proj.astype(BF16), tm=512)


def kernel(x, p, positions, norm_g, w_in, q_lat_g, kv_lat_g, w_uq, w_ukv, q_norm_g, k_norm_g, conv_w, conv_b, w_branch_attn, w_branch_conv, w_out, ple_norm_g, w_ple_gate, w_ple_proj):
    b, s, d = x.shape
    depth = p.shape[0]
    x2 = x.reshape(b * s, d)
    pos = positions.reshape(b * s, 1)
    half = QK_ROPE_DIM // 2
    inv_freq = 1.0 / (ROPE_THETA ** (jnp.arange(0, QK_ROPE_DIM, 2, dtype=F32) / QK_ROPE_DIM))
    freq = jnp.tile(inv_freq, LANES // half).reshape(1, LANES)
    for i in range(depth):
        x2 = _layer(x2, p[i].reshape(b * s, -1), pos, freq, b, s, norm_g[i], w_in[i],
                    q_lat_g[i], kv_lat_g[i], w_uq[i], w_ukv[i], q_norm_g[i], k_norm_g[i],
                    conv_w[i], conv_b[i], w_branch_attn[i], w_branch_conv[i], w_out[i],
                    ple_norm_g[i], w_ple_gate[i], w_ple_proj[i])
    return x2.reshape(b, s, d)
```

```python
import functools

import jax
import jax.numpy as jnp
from jax import lax
from jax.experimental import pallas as pl
from jax.experimental.pallas import tpu as pltpu

N_HEADS = 16
QK_NOPE_DIM = 128
QK_ROPE_DIM = 64
QK_HEAD_DIM = QK_NOPE_DIM + QK_ROPE_DIM
V_HEAD_DIM = 128
Q_LORA_RANK = 512
KV_LORA_RANK = 256
ROPE_THETA = 10000.0
EPS = 1e-6
LOG2E = 1.4426950408889634

LANES = 128
HEAD_PAD = 2 * LANES
SEG = 1024
HEAD_COLS = Q_LORA_RANK + KV_LORA_RANK + QK_ROPE_DIM
HEAD_COLS_ALIGN = 64
assert HEAD_COLS % HEAD_COLS_ALIGN == 0 and SEG % HEAD_COLS_ALIGN == 0
HALO_ROWS = 16
VMEM_LIMIT = 56 * 1024 * 1024

BF16 = jnp.bfloat16
F32 = jnp.float32


def _rms(v, axis_size):
    return v * lax.rsqrt(jnp.sum(v * v, axis=-1, keepdims=True) * (1.0 / axis_size) + EPS)


def _sigmoid(v):
    return 0.5 * jnp.tanh(0.5 * v) + 0.5


def _silu(v):
    return v * _sigmoid(v)


def _identity(v):
    return v


def _keep_head_cols(v):
    col = lax.broadcasted_iota(jnp.int32, v.shape, 1)
    return jnp.where(col < HEAD_COLS, v, 0.0)


Z_TILE_ACTS = (_keep_head_cols, _silu, _silu, _identity, _identity, _identity, _silu,
               _sigmoid, _sigmoid, _sigmoid, _sigmoid)
ROT_TILE = 3
ROT_GROUPS = LANES // (QK_ROPE_DIM // 2)
IN_PROJ_TM = 1024


def _in_proj_kernel(x_ref, g_ref, wt_ref, pos_ref, freq_ref, z_ref, rot_ref, h_ref):
    j = pl.program_id(1)

    @pl.when(j == 0)
    def _():
        x = x_ref[...]
        h_ref[...] = (_rms(x, x.shape[-1]) * g_ref[...]).astype(BF16)

    def rotary_tables():
        half = QK_ROPE_DIM // 2
        rq = x_ref.shape[0] // ROT_GROUPS
        lane = lax.broadcasted_iota(jnp.int32, (rq, LANES), 1)
        ang = pos_ref[...].astype(F32) * freq_ref[...]
        cosv, sinv = jnp.cos(ang), jnp.sin(ang)
        for q in range(ROT_GROUPS):
            rows = slice(q * rq, (q + 1) * rq)
            to_x1 = (lambda v: v) if q == 0 else (
                lambda v: pltpu.roll(v, LANES - q * half, axis=1))
            to_x2 = lambda v: pltpu.roll(v, (LANES - q * half + half) % LANES, axis=1)
            rot_ref[rows, 0:LANES] = jnp.where(lane < half, to_x1(cosv), to_x2(cosv))
            rot_ref[rows, LANES:2 * LANES] = jnp.where(lane < half, -to_x1(sinv), 0.0)
            rot_ref[rows, 2 * LANES:3 * LANES] = jnp.where(
                (lane >= half) & (lane < QK_ROPE_DIM), to_x2(sinv), 0.0)

    variants = {}
    for tile, act in enumerate(Z_TILE_ACTS):
        variants.setdefault((act, tile == ROT_TILE), []).append(tile)
    for (act, with_tables), tiles in variants.items():
        cond = functools.reduce(jnp.logical_or, [j == tile for tile in tiles])

        @pl.when(cond)
        def _(act=act, with_tables=with_tables):
            zt = lax.dot_general(h_ref[...], wt_ref[...].astype(BF16), (((1,), (1,)), ((), ())),
                                 preferred_element_type=F32)
            z_ref[...] = act(zt).astype(BF16)
            if with_tables:
                rotary_tables()


def _in_proj(x2, norm_g, w_t, pos, freq, tm):
    t, d = x2.shape
    n_tiles = len(Z_TILE_ACTS)
    assert w_t.shape == (HEAD_COLS + (n_tiles - 1) * SEG, d)
    return pl.pallas_call(
        _in_proj_kernel,
        grid=(t // tm, n_tiles),
        in_specs=[
            pl.BlockSpec((tm, d), lambda i, j: (i, 0)),
            pl.BlockSpec((1, d), lambda i, j: (0, 0)),
            pl.BlockSpec((pl.Element(SEG), pl.Element(d)),
                         lambda i, j: (pl.multiple_of(
                             jnp.maximum(j * SEG - (SEG - HEAD_COLS), 0), HEAD_COLS_ALIGN), 0)),
            pl.BlockSpec((tm // ROT_GROUPS, LANES), lambda i, j: (i, 0)),
            pl.BlockSpec((1, LANES), lambda i, j: (0, 0)),
        ],
        out_specs=[pl.BlockSpec((tm, SEG), lambda i, j: (i, j)),
                   pl.BlockSpec((tm, 3 * LANES), lambda i, j: (i, 0))],
        out_shape=[jax.ShapeDtypeStruct((t, n_tiles * SEG), BF16),
                   jax.ShapeDtypeStruct((t, 3 * LANES), F32)],
        scratch_shapes=[pltpu.VMEM((tm, d), BF16)],
        compiler_params=pltpu.CompilerParams(
            dimension_semantics=("arbitrary", "arbitrary"),
            vmem_limit_bytes=VMEM_LIMIT),
        name="in_proj",
    )(x2, norm_g, w_t, pos, freq)


def _mla_kernel(z_ref, rot_ref, qlg_ref, kvlg_ref, qg_ref, kg_ref,
                wuq0_ref, wukv0_ref, wuq1_ref, wukv1_ref, wuq2_ref, wukv2_ref,
                o_ref,
                qn_ref, kvn_ref, krope_ref, kss_ref,
                qa_ref, ka_ref, va_ref, qb_ref, kb_ref, vb_ref, *, tq, tp):
    s_len = z_ref.shape[0]
    n_chunks = s_len // tq
    slot_a = (qa_ref, ka_ref, va_ref)
    slot_b = (qb_ref, kb_ref, vb_ref)
    half = QK_ROPE_DIM // 2
    scale = QK_HEAD_DIM ** -0.5 * LOG2E
    inv_d = 1.0 / QK_HEAD_DIM
    qg_nope, qg_rope = qg_ref[:, 0:LANES], qg_ref[:, LANES:HEAD_PAD]
    kg_nope, kg_rope = kg_ref[:, 0:LANES], kg_ref[:, LANES:HEAD_PAD]

    def rope(u, rows):
        return (u * rot_ref[rows, 0:LANES]
                + pltpu.roll(u, LANES - half, axis=1) * rot_ref[rows, LANES:2 * LANES]
                + pltpu.roll(u, half, axis=1) * rot_ref[rows, 2 * LANES:3 * LANES])

    def prep_rows(wuq_ref, wukv_ref, q_out, k_out, v_out, r):
        rows = slice(r * tq, (r + 1) * tq)
        q = jnp.dot(qn_ref[rows, :], wuq_ref[...], preferred_element_type=F32)
        kv = jnp.dot(kvn_ref[rows, :], wukv_ref[...], preferred_element_type=F32)
        q_nope, q_rope = q[:, 0:LANES], q[:, LANES:HEAD_PAD]
        ss = jnp.sum(q_nope * q_nope + q_rope * q_rope, axis=-1, keepdims=True)
        rq = lax.rsqrt(ss * inv_d + EPS) * scale
        q_out[rows, 0:LANES] = (q_nope * rq * qg_nope).astype(BF16)
        q_out[rows, LANES:HEAD_PAD] = (rope(q_rope * qg_rope, rows) * rq).astype(BF16)
        k_nope = kv[:, 0:LANES]
        ssk = jnp.sum(k_nope * k_nope, axis=-1, keepdims=True) + kss_ref[rows, :]
        rk = lax.rsqrt(ssk * inv_d + EPS)
        k_out[rows, 0:LANES] = (k_nope * rk * kg_nope).astype(BF16)
        k_out[rows, LANES:HEAD_PAD] = (krope_ref[rows, :] * rk).astype(BF16)
        v_out[rows, 0:LANES] = kv[:, LANES:HEAD_PAD].astype(BF16)

    def scores(slot, i):
        q_in, k_in, _ = slot
        return lax.dot_general(q_in[i * tq:(i + 1) * tq, :], k_in[...], (((1,), (1,)), ((), ())),
                               preferred_element_type=F32)

    def finish(sc, slot, col, i):
        m = jnp.max(sc, axis=-1, keepdims=True)
        p = jnp.exp2(sc - m).astype(BF16)
        o = jnp.dot(p, slot[2][...], preferred_element_type=F32)
        o_ref[i * tq:(i + 1) * tq, col * LANES:(col + 1) * LANES] = (
            o[:, :V_HEAD_DIM] / o[:, V_HEAD_DIM:V_HEAD_DIM + 1]).astype(BF16)

    @pl.when(pl.program_id(1) == 0)
    def _():
        for r in range(s_len // tp):
            rows = slice(r * tp, (r + 1) * tp)
            qa = z_ref[rows, 0:Q_LORA_RANK].astype(F32)
            kva = z_ref[rows, Q_LORA_RANK:Q_LORA_RANK + KV_LORA_RANK].astype(F32)
            kpe = z_ref[rows, Q_LORA_RANK + KV_LORA_RANK:Q_LORA_RANK + KV_LORA_RANK + LANES
                        ].astype(F32)
            qn_ref[rows, :] = (_rms(qa, Q_LORA_RANK) * qlg_ref[...]).astype(BF16)
            kvn_ref[rows, :] = (_rms(kva, KV_LORA_RANK) * kvlg_ref[...]).astype(BF16)
            krope_ref[rows, :] = rope(kpe * kg_rope, rows)
            kss_ref[rows, :] = jnp.sum(kpe * kpe, axis=-1, keepdims=True)
            lane = lax.broadcasted_iota(jnp.int32, (tp, LANES), 1)
            ones_col = jnp.where(lane == 0, 1.0, 0.0).astype(BF16)
            va_ref[rows, LANES:HEAD_PAD] = ones_col
            vb_ref[rows, LANES:HEAD_PAD] = ones_col
        for r in range(n_chunks):
            prep_rows(wuq0_ref, wukv0_ref, *slot_a, r)

    preps = ((wuq1_ref, wukv1_ref) + slot_b, (wuq2_ref, wukv2_ref) + slot_a)
    slots = (slot_a, slot_b)
    total = 2 * n_chunks
    sc_next = scores(slot_a, 0)
    for t in range(total):
        hd, i = divmod(t, n_chunks)
        sc_cur = sc_next
        prep_rows(*preps[hd], i)
        if t + 1 < total:
            hd2, i2 = divmod(t + 1, n_chunks)
            sc_next = scores(slots[hd2], i2)
        finish(sc_cur, slots[hd], hd, i)


def _mla_attention(z, rot, q_lat_g, kv_lat_g, wuq_p, wukv, qg_p, kg_p, b, s, tq, tp):
    pairs = N_HEADS // 2
    const = lambda shape: pl.BlockSpec(shape, lambda bi, g: (0,) * len(shape))
    head_w = lambda rank, f: pl.BlockSpec((rank, HEAD_PAD), lambda bi, g: (0, f(g)))
    first = lambda g: 0
    odd = lambda g: 2 * g + 1
    nxt = lambda g: jnp.minimum(2 * g + 2, N_HEADS - 1)
    rows = lambda w, dt: pltpu.VMEM((s, w), dt)
    return pl.pallas_call(
        functools.partial(_mla_kernel, tq=tq, tp=tp),
        grid=(b, pairs),
        in_specs=[
            pl.BlockSpec((s, SEG), lambda bi, g: (bi, 0)),
            pl.BlockSpec((s, 3 * LANES), lambda bi, g: (bi, 0)),
            const((1, Q_LORA_RANK)), const((1, KV_LORA_RANK)),
            const((1, HEAD_PAD)), const((1, HEAD_PAD)),
            head_w(Q_LORA_RANK, first), head_w(KV_LORA_RANK, first),
            head_w(Q_LORA_RANK, odd), head_w(KV_LORA_RANK, odd),
            head_w(Q_LORA_RANK, nxt), head_w(KV_LORA_RANK, nxt),
        ],
        out_specs=pl.BlockSpec((s, 2 * V_HEAD_DIM), lambda bi, g: (bi, g)),
        out_shape=jax.ShapeDtypeStruct((b * s, N_HEADS * V_HEAD_DIM), BF16),
        scratch_shapes=[
            rows(Q_LORA_RANK, BF16), rows(KV_LORA_RANK, BF16),
            rows(LANES, F32), rows(1, F32),
            rows(HEAD_PAD, BF16), rows(HEAD_PAD, BF16), rows(HEAD_PAD, BF16),
            rows(HEAD_PAD, BF16), rows(HEAD_PAD, BF16), rows(HEAD_PAD, BF16),
        ],
        compiler_params=pltpu.CompilerParams(
            dimension_semantics=("arbitrary", "arbitrary"), vmem_limit_bytes=VMEM_LIMIT),
        name="mla_attention",
    )(z, rot, q_lat_g, kv_lat_g, qg_p, kg_p, wuq_p, wukv, wuq_p, wukv, wuq_p, wukv)


def _mix_kernel(attn_ref, ga0_ref, ga1_ref, cb_ref, cc_ref, cx_ref, gc_ref,
                ma0_ref, ma1_ref, mc0_ref, mc1_ref, ccp_ref, cxp_ref, ccn_ref, cxn_ref,
                cw_ref, cbias_ref, wba_ref, wbc_ref, o_ref, u_ref, *, tm, seq):
    i = pl.program_id(0)

    a = attn_ref[...] * jnp.concatenate([ga0_ref[...], ga1_ref[...]], axis=1)
    halves = tuple(slice(c * SEG, (c + 1) * SEG) for c in range(2))
    y_attn = [jnp.dot(a, wba_ref[:, cols], preferred_element_type=F32) for cols in halves]

    u = cc_ref[...].astype(F32) * cx_ref[...].astype(F32)
    first = (i * tm) % seq == 0
    last = ((i + 1) * tm) % seq == 0
    prev = (ccp_ref[HALO_ROWS - 1:HALO_ROWS, :].astype(F32)
            * cxp_ref[HALO_ROWS - 1:HALO_ROWS, :].astype(F32))
    nxt = ccn_ref[0:1, :].astype(F32) * cxn_ref[0:1, :].astype(F32)
    u_ref[7:8, :] = jnp.where(first, 0.0, prev)
    u_ref[8:8 + tm, :] = u
    u_ref[8 + tm:9 + tm, :] = jnp.where(last, 0.0, nxt)
    conv = (u_ref[7:7 + tm, :] * cw_ref[0:1, :] + u * cw_ref[1:2, :]
            + u_ref[9:9 + tm, :] * cw_ref[2:3, :] + cbias_ref[...])
    cv = (cb_ref[...].astype(F32) * conv * gc_ref[...].astype(F32)).astype(BF16)

    for c, (ma_ref, mc_ref) in enumerate(((ma0_ref, mc0_ref), (ma1_ref, mc1_ref))):
        y_conv = jnp.dot(cv, wbc_ref[:, halves[c]], preferred_element_type=F32)
        o_ref[:, halves[c]] = (ma_ref[...].astype(F32) * y_attn[c]
                               + mc_ref[...].astype(F32) * y_conv).astype(BF16)


def _mix(attn, z, conv_w, conv_b, wba, wbc, seq, tm):
    t, d = attn.shape[0], wba.shape[1]
    hb = tm // HALO_ROWS
    n_halo = t // HALO_ROWS
    seg = lambda c: pl.BlockSpec((tm, SEG), lambda i, c=c: (i, c))
    prev = lambda c: pl.BlockSpec((HALO_ROWS, SEG), lambda i, c=c: (jnp.maximum(i * hb - 1, 0), c))
    nxt = lambda c: pl.BlockSpec((HALO_ROWS, SEG),
                                 lambda i, c=c: (jnp.minimum((i + 1) * hb, n_halo - 1), c))
    resident = lambda shape: pl.BlockSpec(shape, lambda i: (0,) * len(shape),
                                          pipeline_mode=pl.Buffered(1))
    return pl.pallas_call(
        functools.partial(_mix_kernel, tm=tm, seq=seq),
        grid=(t // tm,),
        in_specs=[
            pl.BlockSpec((tm, attn.shape[1]), lambda i: (i, 0)),
            seg(1), seg(2), seg(3), seg(4), seg(5), seg(6), seg(7), seg(8), seg(9), seg(10),
            prev(4), prev(5), nxt(4), nxt(5),
            resident(conv_w.shape), resident(conv_b.shape),
            resident(wba.shape), resident(wbc.shape),
        ],
        out_specs=pl.BlockSpec((tm, d), lambda i: (i, 0)),
        out_shape=jax.ShapeDtypeStruct((t, d), BF16),
        scratch_shapes=[pltpu.VMEM((tm + 16, SEG), F32)],
        compiler_params=pltpu.CompilerParams(
            dimension_semantics=("arbitrary",), vmem_limit_bytes=VMEM_LIMIT),
        name="mix",
    )(attn, z, z, z, z, z, z, z, z, z, z, z, z, z, z, conv_w, conv_b, wba, wbc)


def _out_ple_kernel(x_ref, m_ref, p_ref, g_ref, wout_ref, wg_ref, wp_ref, o_ref, x1_ref):
    x1_ref[...] = x_ref[...] + jnp.dot(m_ref[...], wout_ref[...], preferred_element_type=F32)
    pb = p_ref[...].astype(BF16)
    x1 = x1_ref[...]
    h = (_rms(x1, x1.shape[-1]) * g_ref[...]).astype(BF16)
    for c in range(x1.shape[-1] // SEG):
        cols = slice(c * SEG, (c + 1) * SEG)
        proj = jnp.dot(pb, wp_ref[:, cols], preferred_element_type=F32)
        gate = _sigmoid(jnp.dot(h, wg_ref[:, cols], preferred_element_type=F32))
        o_ref[:, cols] = x1_ref[:, cols] + gate * proj


def _out_ple(x2, merged, p2, ple_g, wout, wg, wp, tm):
    t, d = x2.shape
    resident = lambda shape: pl.BlockSpec(shape, lambda i: (0,) * len(shape),
                                          pipeline_mode=pl.Buffered(1))
    return pl.pallas_call(
        _out_ple_kernel,
        grid=(t // tm,),
        in_specs=[
            pl.BlockSpec((tm, d), lambda i: (i, 0)),
            pl.BlockSpec((tm, d), lambda i: (i, 0)),
            pl.BlockSpec((tm, p2.shape[1]), lambda i: (i, 0)),
            resident((1, d)), resident(wout.shape), resident(wg.shape), resident(wp.shape),
        ],
        out_specs=pl.BlockSpec((tm, d), lambda i: (i, 0)),
        out_shape=jax.ShapeDtypeStruct((t, d), F32),
        scratch_shapes=[pltpu.VMEM((tm, d), F32)],
        compiler_params=pltpu.CompilerParams(
            dimension_semantics=("arbitrary",), vmem_limit_bytes=VMEM_LIMIT),
        name="out_ple",
    )(x2, merged, p2, ple_g, wout, wg, wp)


def _layer(x2, p2, pos, freq, b, s, norm_g, w_in, q_lat_g, kv_lat_g, w_uq, w_ukv,
           q_norm_g, k_norm_g, conv_w, conv_b, w_branch_attn, w_branch_conv,
           w_out, ple_norm_g, w_ple_gate, w_ple_proj):
    d = x2.shape[1]
    wuq_p = jnp.pad(w_uq.reshape(Q_LORA_RANK, N_HEADS, QK_HEAD_DIM),
                    ((0, 0), (0, 0), (0, HEAD_PAD - QK_HEAD_DIM))
                    ).reshape(Q_LORA_RANK, N_HEADS * HEAD_PAD).astype(BF16)
    pad_g = lambda g: jnp.pad(g, (0, HEAD_PAD - QK_HEAD_DIM)).reshape(1, HEAD_PAD)

    z, rot = _in_proj(x2, norm_g.reshape(1, d), jnp.swapaxes(w_in, 0, 1), pos, freq,
                      tm=IN_PROJ_TM)
    attn = _mla_attention(z, rot, q_lat_g.reshape(1, -1), kv_lat_g.reshape(1, -1),
                          wuq_p, w_ukv.astype(BF16), pad_g(q_norm_g), pad_g(k_norm_g),
                          b, s, tq=512, tp=512)
    merged = _mix(attn, z, conv_w, conv_b.reshape(1, -1), w_branch_attn.astype(BF16),
                  w_branch_conv.astype(BF16), seq=s, tm=512)
    return _out_ple(x2, merged, p2, ple_norm_g.reshape(1, d), w_out.astype(BF16),
                    w_ple_gate.astype(BF16), w_ple_proj.astype(BF16), tm=512)


def kernel(x, p, positions, norm_g, w_in, q_lat_g, kv_lat_g, w_uq, w_ukv, q_norm_g, k_norm_g, conv_w, conv_b, w_branch_attn, w_branch_conv, w_out, ple_norm_g, w_ple_gate, w_ple_proj):
    b, s, d = x.shape
    depth = p.shape[0]
    x2 = x.reshape(b * s, d)
    half = QK_ROPE_DIM // 2
    pos = positions.reshape(b * s // IN_PROJ_TM, ROT_GROUPS, IN_PROJ_TM // ROT_GROUPS)
    pos = jnp.repeat(jnp.swapaxes(pos, 1, 2), half, axis=2).reshape(b * s // ROT_GROUPS, LANES)
    inv_freq = 1.0 / (ROPE_THETA ** (jnp.arange(0, QK_ROPE_DIM, 2, dtype=F32) / QK_ROPE_DIM))
    freq = jnp.tile(inv_freq, ROT_GROUPS).reshape(1, LANES)
    for i in range(depth):
        x2 = _layer(x2, p[i].reshape(b * s, -1), pos, freq, b, s, norm_g[i], w_in[i],
                    q_lat_g[i], kv_lat_g[i], w_uq[i], w_ukv[i], q_norm_g[i], k_norm_g[i],
                    conv_w[i], conv_b[i], w_branch_attn[i], w_branch_conv[i], w_out[i],
                    ple_norm_g[i], w_ple_gate[i], w_ple_proj[i])
    return x2.reshape(b, s, d)
```

```python
import functools

import jax
import jax.numpy as jnp
from jax import lax
from jax.experimental import pallas as pl
from jax.experimental.pallas import tpu as pltpu

N_HEADS = 16
QK_NOPE_DIM = 128
QK_ROPE_DIM = 64
QK_HEAD_DIM = QK_NOPE_DIM + QK_ROPE_DIM
V_HEAD_DIM = 128
Q_LORA_RANK = 512
KV_LORA_RANK = 256
ROPE_THETA = 10000.0
EPS = 1e-6
LOG2E = 1.4426950408889634

LANES = 128
HEAD_PAD = 2 * LANES
SEG = 1024
HEAD_COLS = Q_LORA_RANK + KV_LORA_RANK + QK_ROPE_DIM
HEAD_COLS_ALIGN = 64
assert HEAD_COLS % HEAD_COLS_ALIGN == 0 and SEG % HEAD_COLS_ALIGN == 0
HALO_ROWS = 16
VMEM_LIMIT = 56 * 1024 * 1024

BF16 = jnp.bfloat16
F32 = jnp.float32


def _rms(v, axis_size):
    return v * lax.rsqrt(jnp.sum(v * v, axis=-1, keepdims=True) * (1.0 / axis_size) + EPS)


def _sigmoid(v):
    return 0.5 * jnp.tanh(0.5 * v) + 0.5


def _silu(v):
    return v * _sigmoid(v)


def _identity(v):
    return v


def _keep_head_cols(v):
    col = lax.broadcasted_iota(jnp.int32, v.shape, 1)
    return jnp.where(col < HEAD_COLS, v, 0.0)


Z_TILE_ACTS = (_keep_head_cols, _silu, _silu, _identity, _identity, _identity, _silu,
               _sigmoid, _sigmoid, _sigmoid, _sigmoid)
ROT_TILE = 3
ROT_GROUPS = LANES // (QK_ROPE_DIM // 2)
IN_PROJ_TM = 1024


def _in_proj_kernel(x_ref, g_ref, wt_ref, pos_ref, freq_ref, z_ref, rot_ref, h_ref):
    j = pl.program_id(1)

    @pl.when(j == 0)
    def _():
        x = x_ref[...]
        h_ref[...] = (_rms(x, x.shape[-1]) * g_ref[...]).astype(BF16)

    def rotary_tables():
        half = QK_ROPE_DIM // 2
        rq = x_ref.shape[0] // ROT_GROUPS
        lane = lax.broadcasted_iota(jnp.int32, (rq, LANES), 1)
        ang = pos_ref[...].astype(F32) * freq_ref[...]
        cosv, sinv = jnp.cos(ang), jnp.sin(ang)
        for q in range(ROT_GROUPS):
            rows = slice(q * rq, (q + 1) * rq)
            to_x1 = (lambda v: v) if q == 0 else (
                lambda v: pltpu.roll(v, LANES - q * half, axis=1))
            to_x2 = lambda v: pltpu.roll(v, (LANES - q * half + half) % LANES, axis=1)
            rot_ref[rows, 0:LANES] = jnp.where(lane < half, to_x1(cosv), to_x2(cosv))
            rot_ref[rows, LANES:2 * LANES] = jnp.where(lane < half, -to_x1(sinv), 0.0)
            rot_ref[rows, 2 * LANES:3 * LANES] = jnp.where(
                (lane >= half) & (lane < QK_ROPE_DIM), to_x2(sinv), 0.0)

    def is_act(act):
        return functools.reduce(jnp.logical_or,
                                [j == tile for tile, a in enumerate(Z_TILE_ACTS) if a is act])

    zt = lax.dot_general(h_ref[...], wt_ref[...].astype(BF16), (((1,), (1,)), ((), ())),
                         preferred_element_type=F32)
    sig = _sigmoid(zt)
    col = lax.broadcasted_iota(jnp.int32, zt.shape, 1)
    keep = jnp.logical_or(jnp.logical_not(is_act(_keep_head_cols)), col < HEAD_COLS)
    out = jnp.where(is_act(_silu), zt * sig, jnp.where(is_act(_sigmoid), sig, zt))
    z_ref[...] = jnp.where(keep, out, 0.0).astype(BF16)

    @pl.when(j == ROT_TILE)
    def _():
        rotary_tables()


def _in_proj(x2, norm_g, w_t, pos, freq, tm):
    t, d = x2.shape
    n_tiles = len(Z_TILE_ACTS)
    assert w_t.shape == (HEAD_COLS + (n_tiles - 1) * SEG, d)
    return pl.pallas_call(
        _in_proj_kernel,
        grid=(t // tm, n_tiles),
        in_specs=[
            pl.BlockSpec((tm, d), lambda i, j: (i, 0)),
            pl.BlockSpec((1, d), lambda i, j: (0, 0)),
            pl.BlockSpec((pl.Element(SEG), pl.Element(d)),
                         lambda i, j: (pl.multiple_of(
                             jnp.maximum(j * SEG - (SEG - HEAD_COLS), 0), HEAD_COLS_ALIGN), 0)),
            pl.BlockSpec((tm // ROT_GROUPS, LANES), lambda i, j: (i, 0)),
            pl.BlockSpec((1, LANES), lambda i, j: (0, 0)),
        ],
        out_specs=[pl.BlockSpec((tm, SEG), lambda i, j: (i, j)),
                   pl.BlockSpec((tm, 3 * LANES), lambda i, j: (i, 0))],
        out_shape=[jax.ShapeDtypeStruct((t, n_tiles * SEG), BF16),
                   jax.ShapeDtypeStruct((t, 3 * LANES), F32)],
        scratch_shapes=[pltpu.VMEM((tm, d), BF16)],
        compiler_params=pltpu.CompilerParams(
            dimension_semantics=("arbitrary", "arbitrary"),
            vmem_limit_bytes=VMEM_LIMIT),
        name="in_proj",
    )(x2, norm_g, w_t, pos, freq)


def _mla_kernel(z_ref, rot_ref, qlg_ref, kvlg_ref, qg_ref, kg_ref,
                wuq0_ref, wukv0_ref, wuq1_ref, wukv1_ref, wuq2_ref, wukv2_ref,
                o_ref,
                qn_ref, kvn_ref, krope_ref, kss_ref,
                qa_ref, ka_ref, va_ref, qb_ref, kb_ref, vb_ref, *, tq, tp):
    s_len = z_ref.shape[0]
    n_chunks = s_len // tq
    slot_a = (qa_ref, ka_ref, va_ref)
    slot_b = (qb_ref, kb_ref, vb_ref)
    half = QK_ROPE_DIM // 2
    scale = QK_HEAD_DIM ** -0.5 * LOG2E
    inv_d = 1.0 / QK_HEAD_DIM
    qg_nope, qg_rope = qg_ref[:, 0:LANES], qg_ref[:, LANES:HEAD_PAD]
    kg_nope, kg_rope = kg_ref[:, 0:LANES], kg_ref[:, LANES:HEAD_PAD]

    def rope(u, rows):
        return (u * rot_ref[rows, 0:LANES]
                + pltpu.roll(u, LANES - half, axis=1) * rot_ref[rows, LANES:2 * LANES]
                + pltpu.roll(u, half, axis=1) * rot_ref[rows, 2 * LANES:3 * LANES])

    def prep_rows(wuq_ref, wukv_ref, q_out, k_out, v_out, r):
        rows = slice(r * tq, (r + 1) * tq)
        q = jnp.dot(qn_ref[rows, :], wuq_ref[...], preferred_element_type=F32)
        kv = jnp.dot(kvn_ref[rows, :], wukv_ref[...], preferred_element_type=F32)
        q_nope, q_rope = q[:, 0:LANES], q[:, LANES:HEAD_PAD]
        ss = jnp.sum(q_nope * q_nope + q_rope * q_rope, axis=-1, keepdims=True)
        rq = lax.rsqrt(ss * inv_d + EPS) * scale
        q_out[rows, 0:LANES] = (q_nope * rq * qg_nope).astype(BF16)
        q_out[rows, LANES:HEAD_PAD] = (rope(q_rope * qg_rope, rows) * rq).astype(BF16)
        k_nope = kv[:, 0:LANES]
        ssk = jnp.sum(k_nope * k_nope, axis=-1, keepdims=True) + kss_ref[rows, :]
        rk = lax.rsqrt(ssk * inv_d + EPS)
        k_out[rows, 0:LANES] = (k_nope * rk * kg_nope).astype(BF16)
        k_out[rows, LANES:HEAD_PAD] = (krope_ref[rows, :] * rk).astype(BF16)
        v_out[rows, 0:LANES] = kv[:, LANES:HEAD_PAD].astype(BF16)

    def scores(slot, i):
        q_in, k_in, _ = slot
        return lax.dot_general(q_in[i * tq:(i + 1) * tq, :], k_in[...], (((1,), (1,)), ((), ())),
                               preferred_element_type=F32)

    def finish(sc, slot, col, i):
        m = jnp.max(sc, axis=-1, keepdims=True)
        p = jnp.exp2(sc - m).astype(BF16)
        o = jnp.dot(p, slot[2][...], preferred_element_type=F32)
        o_ref[i * tq:(i + 1) * tq, col * LANES:(col + 1) * LANES] = (
            o[:, :V_HEAD_DIM] / o[:, V_HEAD_DIM:V_HEAD_DIM + 1]).astype(BF16)

    @pl.when(pl.program_id(1) == 0)
    def _():
        for r in range(s_len // tp):
            rows = slice(r * tp, (r + 1) * tp)
            qa = z_ref[rows, 0:Q_LORA_RANK].astype(F32)
            kva = z_ref[rows, Q_LORA_RANK:Q_LORA_RANK + KV_LORA_RANK].astype(F32)
            kpe = z_ref[rows, Q_LORA_RANK + KV_LORA_RANK:Q_LORA_RANK + KV_LORA_RANK + LANES
                        ].astype(F32)
            qn_ref[rows, :] = (_rms(qa, Q_LORA_RANK) * qlg_ref[...]).astype(BF16)
            kvn_ref[rows, :] = (_rms(kva, KV_LORA_RANK) * kvlg_ref[...]).astype(BF16)
            krope_ref[rows, :] = rope(kpe * kg_rope, rows)
            kss_ref[rows, :] = jnp.sum(kpe * kpe, axis=-1, keepdims=True)
            lane = lax.broadcasted_iota(jnp.int32, (tp, LANES), 1)
            ones_col = jnp.where(lane == 0, 1.0, 0.0).astype(BF16)
            va_ref[rows, LANES:HEAD_PAD] = ones_col
            vb_ref[rows, LANES:HEAD_PAD] = ones_col
        for r in range(n_chunks):
            prep_rows(wuq0_ref, wukv0_ref, *slot_a, r)

    preps = ((wuq1_ref, wukv1_ref) + slot_b, (wuq2_ref, wukv2_ref) + slot_a)
    slots = (slot_a, slot_b)
    total = 2 * n_chunks
    sc_next = scores(slot_a, 0)
    for t in range(total):
        hd, i = divmod(t, n_chunks)
        sc_cur = sc_next
        prep_rows(*preps[hd], i)
        if t + 1 < total:
            hd2, i2 = divmod(t + 1, n_chunks)
            sc_next = scores(slots[hd2], i2)
        finish(sc_cur, slots[hd], hd, i)


def _mla_attention(z, rot, q_lat_g, kv_lat_g, wuq_p, wukv, qg_p, kg_p, b, s, tq, tp):
    pairs = N_HEADS // 2
    const = lambda shape: pl.BlockSpec(shape, lambda bi, g: (0,) * len(shape))
    head_w = lambda rank, f: pl.BlockSpec((rank, HEAD_PAD), lambda bi, g: (0, f(g)))
    first = lambda g: 0
    odd = lambda g: 2 * g + 1
    nxt = lambda g: jnp.minimum(2 * g + 2, N_HEADS - 1)
    rows = lambda w, dt: pltpu.VMEM((s, w), dt)
    return pl.pallas_call(
        functools.partial(_mla_kernel, tq=tq, tp=tp),
        grid=(b, pairs),
        in_specs=[
            pl.BlockSpec((s, SEG), lambda bi, g: (bi, 0)),
            pl.BlockSpec((s, 3 * LANES), lambda bi, g: (bi, 0)),
            const((1, Q_LORA_RANK)), const((1, KV_LORA_RANK)),
            const((1, HEAD_PAD)), const((1, HEAD_PAD)),
            head_w(Q_LORA_RANK, first), head_w(KV_LORA_RANK, first),
            head_w(Q_LORA_RANK, odd), head_w(KV_LORA_RANK, odd),
            head_w(Q_LORA_RANK, nxt), head_w(KV_LORA_RANK, nxt),
        ],
        out_specs=pl.BlockSpec((s, 2 * V_HEAD_DIM), lambda bi, g: (bi, g)),
        out_shape=jax.ShapeDtypeStruct((b * s, N_HEADS * V_HEAD_DIM), BF16),
        scratch_shapes=[
            rows(Q_LORA_RANK, BF16), rows(KV_LORA_RANK, BF16),
            rows(LANES, F32), rows(1, F32),
            rows(HEAD_PAD, BF16), rows(HEAD_PAD, BF16), rows(HEAD_PAD, BF16),
            rows(HEAD_PAD, BF16), rows(HEAD_PAD, BF16), rows(HEAD_PAD, BF16),
        ],
        compiler_params=pltpu.CompilerParams(
            dimension_semantics=("arbitrary", "arbitrary"), vmem_limit_bytes=VMEM_LIMIT),
        name="mla_attention",
    )(z, rot, q_lat_g, kv_lat_g, qg_p, kg_p, wuq_p, wukv, wuq_p, wukv, wuq_p, wukv)


def _mix_kernel(attn_ref, ga0_ref, ga1_ref, cb_ref, cc_ref, cx_ref, gc_ref,
                ma0_ref, ma1_ref, mc0_ref, mc1_ref, ccp_ref, cxp_ref, ccn_ref, cxn_ref,
                cw_ref, cbias_ref, wba_ref, wbc_ref, o_ref, u_ref, *, tm, seq):
    i = pl.program_id(0)

    a = attn_ref[...] * jnp.concatenate([ga0_ref[...], ga1_ref[...]], axis=1)
    halves = tuple(slice(c * SEG, (c + 1) * SEG) for c in range(2))
    y_attn = [jnp.dot(a, wba_ref[:, cols], preferred_element_type=F32) for cols in halves]

    u = cc_ref[...].astype(F32) * cx_ref[...].astype(F32)
    first = (i * tm) % seq == 0
    last = ((i + 1) * tm) % seq == 0
    prev = (ccp_ref[HALO_ROWS - 1:HALO_ROWS, :].astype(F32)
            * cxp_ref[HALO_ROWS - 1:HALO_ROWS, :].astype(F32))
    nxt = ccn_ref[0:1, :].astype(F32) * cxn_ref[0:1, :].astype(F32)
    u_ref[7:8, :] = jnp.where(first, 0.0, prev)
    u_ref[8:8 + tm, :] = u
    u_ref[8 + tm:9 + tm, :] = jnp.where(last, 0.0, nxt)
    conv = (u_ref[7:7 + tm, :] * cw_ref[0:1, :] + u * cw_ref[1:2, :]
            + u_ref[9:9 + tm, :] * cw_ref[2:3, :] + cbias_ref[...])
    cv = (cb_ref[...].astype(F32) * conv * gc_ref[...].astype(F32)).astype(BF16)

    for c, (ma_ref, mc_ref) in enumerate(((ma0_ref, mc0_ref), (ma1_ref, mc1_ref))):
        y_conv = jnp.dot(cv, wbc_ref[:, halves[c]], preferred_element_type=F32)
        o_ref[:, halves[c]] = (ma_ref[...].astype(F32) * y_attn[c]
                               + mc_ref[...].astype(F32) * y_conv).astype(BF16)


def _mix(attn, z, conv_w, conv_b, wba, wbc, seq, tm):
    t, d = attn.shape[0], wba.shape[1]
    hb = tm // HALO_ROWS
    n_halo = t // HALO_ROWS
    seg = lambda c: pl.BlockSpec((tm, SEG), lambda i, c=c: (i, c))
    prev = lambda c: pl.BlockSpec((HALO_ROWS, SEG), lambda i, c=c: (jnp.maximum(i * hb - 1, 0), c))
    nxt = lambda c: pl.BlockSpec((HALO_ROWS, SEG),
                                 lambda i, c=c: (jnp.minimum((i + 1) * hb, n_halo - 1), c))
    resident = lambda shape: pl.BlockSpec(shape, lambda i: (0,) * len(shape),
                                          pipeline_mode=pl.Buffered(1))
    return pl.pallas_call(
        functools.partial(_mix_kernel, tm=tm, seq=seq),
        grid=(t // tm,),
        in_specs=[
            pl.BlockSpec((tm, attn.shape[1]), lambda i: (i, 0)),
            seg(1), seg(2), seg(3), seg(4), seg(5), seg(6), seg(7), seg(8), seg(9), seg(10),
            prev(4), prev(5), nxt(4), nxt(5),
            resident(conv_w.shape), resident(conv_b.shape),
            resident(wba.shape), resident(wbc.shape),
        ],
        out_specs=pl.BlockSpec((tm, d), lambda i: (i, 0)),
        out_shape=jax.ShapeDtypeStruct((t, d), BF16),
        scratch_shapes=[pltpu.VMEM((tm + 16, SEG), F32)],
        compiler_params=pltpu.CompilerParams(
            dimension_semantics=("arbitrary",), vmem_limit_bytes=VMEM_LIMIT),
        name="mix",
    )(attn, z, z, z, z, z, z, z, z, z, z, z, z, z, z, conv_w, conv_b, wba, wbc)


def _out_ple_kernel(x_ref, m_ref, p_ref, g_ref, wout_ref, wg_ref, wp_ref, o_ref, x1_ref):
    x1_ref[...] = x_ref[...] + jnp.dot(m_ref[...], wout_ref[...], preferred_element_type=F32)
    pb = p_ref[...].astype(BF16)
    x1 = x1_ref[...]
    h = (_rms(x1, x1.shape[-1]) * g_ref[...]).astype(BF16)
    for c in range(x1.shape[-1] // SEG):
        cols = slice(c * SEG, (c + 1) * SEG)
        proj = jnp.dot(pb, wp_ref[:, cols], preferred_element_type=F32)
        gate = _sigmoid(jnp.dot(h, wg_ref[:, cols], preferred_element_type=F32))
        o_ref[:, cols] = x1_ref[:, cols] + gate * proj


def _out_ple(x2, merged, p2, ple_g, wout, wg, wp, tm):
    t, d = x2.shape
    resident = lambda shape: pl.BlockSpec(shape, lambda i: (0,) * len(shape),
                                          pipeline_mode=pl.Buffered(1))
    return pl.pallas_call(
        _out_ple_kernel,
        grid=(t // tm,),
        in_specs=[
            pl.BlockSpec((tm, d), lambda i: (i, 0)),
            pl.BlockSpec((tm, d), lambda i: (i, 0)),
            pl.BlockSpec((tm, p2.shape[1]), lambda i: (i, 0)),
            resident((1, d)), resident(wout.shape), resident(wg.shape), resident(wp.shape),
        ],
        out_specs=pl.BlockSpec((tm, d), lambda i: (i, 0)),
        out_shape=jax.ShapeDtypeStruct((t, d), F32),
        scratch_shapes=[pltpu.VMEM((tm, d), F32)],
        compiler_params=pltpu.CompilerParams(
            dimension_semantics=("arbitrary",), vmem_limit_bytes=VMEM_LIMIT),
        name="out_ple",
    )(x2, merged, p2, ple_g, wout, wg, wp)


def _layer(x2, p2, pos, freq, b, s, norm_g, w_in, q_lat_g, kv_lat_g, w_uq, w_ukv,
           q_norm_g, k_norm_g, conv_w, conv_b, w_branch_attn, w_branch_conv,
           w_out, ple_norm_g, w_ple_gate, w_ple_proj):
    d = x2.shape[1]
    wuq_p = jnp.pad(w_uq.reshape(Q_LORA_RANK, N_HEADS, QK_HEAD_DIM),
                    ((0, 0), (0, 0), (0, HEAD_PAD - QK_HEAD_DIM))
                    ).reshape(Q_LORA_RANK, N_HEADS * HEAD_PAD).astype(BF16)
    pad_g = lambda g: jnp.pad(g, (0, HEAD_PAD - QK_HEAD_DIM)).reshape(1, HEAD_PAD)

    z, rot = _in_proj(x2, norm_g.reshape(1, d), jnp.swapaxes(w_in, 0, 1), pos, freq,
                      tm=IN_PROJ_TM)
    attn = _mla_attention(z, rot, q_lat_g.reshape(1, -1), kv_lat_g.reshape(1, -1),
                          wuq_p, w_ukv.astype(BF16), pad_g(q_norm_g), pad_g(k_norm_g),
                          b, s, tq=512, tp=512)
    merged = _mix(attn, z, conv_w, conv_b.reshape(1, -1), w_branch_attn.astype(BF16),
                  w_branch_conv.astype(BF16), seq=s, tm=512)
    return _out_ple(x2, merged, p2, ple_norm_g.reshape(1, d), w_out.astype(BF16),
                    w_ple_gate.astype(BF16), w_ple_proj.astype(BF16), tm=512)


def kernel(x, p, positions, norm_g, w_in, q_lat_g, kv_lat_g, w_uq, w_ukv, q_norm_g, k_norm_g, conv_w, conv_b, w_branch_attn, w_branch_conv, w_out, ple_norm_g, w_ple_gate, w_ple_proj):
    b, s, d = x.shape
    depth = p.shape[0]
    x2 = x.reshape(b * s, d)
    half = QK_ROPE_DIM // 2
    pos = positions.reshape(b * s // IN_PROJ_TM, ROT_GROUPS, IN_PROJ_TM // ROT_GROUPS)
    pos = jnp.repeat(jnp.swapaxes(pos, 1, 2), half, axis=2).reshape(b * s // ROT_GROUPS, LANES)
    inv_freq = 1.0 / (ROPE_THETA ** (jnp.arange(0, QK_ROPE_DIM, 2, dtype=F32) / QK_ROPE_DIM))
    freq = jnp.tile(inv_freq, ROT_GROUPS).reshape(1, LANES)
    for i in range(depth):
        x2 = _layer(x2, p[i].reshape(b * s, -1), pos, freq, b, s, norm_g[i], w_in[i],
                    q_lat_g[i], kv_lat_g[i], w_uq[i], w_ukv[i], q_norm_g[i], k_norm_g[i],
                    conv_w[i], conv_b[i], w_branch_attn[i], w_branch_conv[i], w_out[i],
                    ple_norm_g[i], w_ple_gate[i], w_ple_proj[i])
    return x2.reshape(b, s, d)
```

```python
import functools

import jax
import jax.numpy as jnp
from jax import lax
from jax.experimental import pallas as pl
from jax.experimental.pallas import tpu as pltpu

N_HEADS = 16
QK_NOPE_DIM = 128
QK_ROPE_DIM = 64
QK_HEAD_DIM = QK_NOPE_DIM + QK_ROPE_DIM
V_HEAD_DIM = 128
Q_LORA_RANK = 512
KV_LORA_RANK = 256
ROPE_THETA = 10000.0
EPS = 1e-6
LOG2E = 1.4426950408889634

LANES = 128
HEAD_PAD = 2 * LANES
SEG = 1024
HEAD_COLS = Q_LORA_RANK + KV_LORA_RANK + QK_ROPE_DIM
HEAD_COLS_ALIGN = 64
assert HEAD_COLS % HEAD_COLS_ALIGN == 0 and SEG % HEAD_COLS_ALIGN == 0
HALO_ROWS = 16
VMEM_LIMIT = 56 * 1024 * 1024

BF16 = jnp.bfloat16
F32 = jnp.float32


def _rms(v, axis_size):
    return v * lax.rsqrt(jnp.sum(v * v, axis=-1, keepdims=True) * (1.0 / axis_size) + EPS)


def _sigmoid(v):
    return 0.5 * jnp.tanh(0.5 * v) + 0.5


def _silu(v):
    return v * _sigmoid(v)


def _identity(v):
    return v


Z_TILE_ACTS = (_identity, _silu, _silu, _identity, _identity, _identity, _silu,
               _sigmoid, _sigmoid, _sigmoid, _sigmoid)
ROT_TILE = 3
ROT_GROUPS = LANES // (QK_ROPE_DIM // 2)
IN_PROJ_TM = 1024


def _in_proj_kernel(x_ref, g_ref, wt_ref, pos_ref, freq_ref, z_ref, rot_ref, h_ref):
    j = pl.program_id(1)

    @pl.when(j == 0)
    def _():
        x = x_ref[...]
        h_ref[...] = (_rms(x, x.shape[-1]) * g_ref[...]).astype(BF16)

    def rotary_tables():
        half = QK_ROPE_DIM // 2
        rq = x_ref.shape[0] // ROT_GROUPS
        lane = lax.broadcasted_iota(jnp.int32, (rq, LANES), 1)
        ang = pos_ref[...].astype(F32) * freq_ref[...]
        cosv, sinv = jnp.cos(ang), jnp.sin(ang)
        for q in range(ROT_GROUPS):
            rows = slice(q * rq, (q + 1) * rq)
            to_x1 = (lambda v: v) if q == 0 else (
                lambda v: pltpu.roll(v, LANES - q * half, axis=1))
            to_x2 = lambda v: pltpu.roll(v, (LANES - q * half + half) % LANES, axis=1)
            rot_ref[rows, 0:LANES] = jnp.where(lane < half, to_x1(cosv), to_x2(cosv))
            rot_ref[rows, LANES:2 * LANES] = jnp.where(lane < half, -to_x1(sinv), 0.0)
            rot_ref[rows, 2 * LANES:3 * LANES] = jnp.where(
                (lane >= half) & (lane < QK_ROPE_DIM), to_x2(sinv), 0.0)

    def is_act(*acts):
        return functools.reduce(jnp.logical_or,
                                [j == tile for tile, a in enumerate(Z_TILE_ACTS) if a in acts])

    def matmul():
        return lax.dot_general(h_ref[...], wt_ref[...].astype(BF16), (((1,), (1,)), ((), ())),
                               preferred_element_type=F32)

    @pl.when(is_act(_identity))
    def _():
        z_ref[...] = matmul().astype(BF16)

    @pl.when(is_act(_silu, _sigmoid))
    def _():
        hv = 0.5 * matmul()
        a = jnp.where(is_act(_silu), hv, 0.5)
        z_ref[...] = (a * jnp.tanh(hv) + a).astype(BF16)

    @pl.when(j == ROT_TILE)
    def _():
        rotary_tables()


def _in_proj(x2, norm_g, w_t, pos, freq, tm):
    t, d = x2.shape
    n_tiles = len(Z_TILE_ACTS)
    assert w_t.shape == (HEAD_COLS + (n_tiles - 1) * SEG, d)
    return pl.pallas_call(
        _in_proj_kernel,
        grid=(t // tm, n_tiles),
        in_specs=[
            pl.BlockSpec((tm, d), lambda i, j: (i, 0)),
            pl.BlockSpec((1, d), lambda i, j: (0, 0)),
            pl.BlockSpec((pl.Element(SEG), pl.Element(d)),
                         lambda i, j: (pl.multiple_of(
                             jnp.maximum(j * SEG - (SEG - HEAD_COLS), 0), HEAD_COLS_ALIGN), 0)),
            pl.BlockSpec((tm // ROT_GROUPS, LANES), lambda i, j: (i, 0)),
            pl.BlockSpec((1, LANES), lambda i, j: (0, 0)),
        ],
        out_specs=[pl.BlockSpec((tm, SEG), lambda i, j: (i, j)),
                   pl.BlockSpec((tm, 3 * LANES), lambda i, j: (i, 0))],
        out_shape=[jax.ShapeDtypeStruct((t, n_tiles * SEG), BF16),
                   jax.ShapeDtypeStruct((t, 3 * LANES), F32)],
        scratch_shapes=[pltpu.VMEM((tm, d), BF16)],
        compiler_params=pltpu.CompilerParams(
            dimension_semantics=("arbitrary", "arbitrary"),
            vmem_limit_bytes=VMEM_LIMIT),
        name="in_proj",
    )(x2, norm_g, w_t, pos, freq)


def _mla_kernel(z_ref, rot_ref, qlg_ref, kvlg_ref, qg_ref, kg_ref,
                wuq0_ref, wukv0_ref, wuq1_ref, wukv1_ref, wuq2_ref, wukv2_ref,
                o_ref,
                qn_ref, kvn_ref, krope_ref, kss_ref,
                qa_ref, ka_ref, va_ref, qb_ref, kb_ref, vb_ref, *, tq, tp):
    s_len = z_ref.shape[0]
    n_chunks = s_len // tq
    slot_a = (qa_ref, ka_ref, va_ref)
    slot_b = (qb_ref, kb_ref, vb_ref)
    half = QK_ROPE_DIM // 2
    scale = QK_HEAD_DIM ** -0.5 * LOG2E
    inv_d = 1.0 / QK_HEAD_DIM
    qg_nope, qg_rope = qg_ref[:, 0:LANES], qg_ref[:, LANES:HEAD_PAD]
    kg_nope, kg_rope = kg_ref[:, 0:LANES], kg_ref[:, LANES:HEAD_PAD]

    def rope(u, rows):
        return (u * rot_ref[rows, 0:LANES]
                + pltpu.roll(u, LANES - half, axis=1) * rot_ref[rows, LANES:2 * LANES]
                + pltpu.roll(u, half, axis=1) * rot_ref[rows, 2 * LANES:3 * LANES])

    def prep_rows(wuq_ref, wukv_ref, q_out, k_out, v_out, r):
        rows = slice(r * tq, (r + 1) * tq)
        q = jnp.dot(qn_ref[rows, :], wuq_ref[...], preferred_element_type=F32)
        kv = jnp.dot(kvn_ref[rows, :], wukv_ref[...], preferred_element_type=F32)
        q_nope, q_rope = q[:, 0:LANES], q[:, LANES:HEAD_PAD]
        ss = jnp.sum(q_nope * q_nope + q_rope * q_rope, axis=-1, keepdims=True)
        rq = lax.rsqrt(ss * inv_d + EPS) * scale
        q_out[rows, 0:LANES] = (q_nope * rq * qg_nope).astype(BF16)
        q_out[rows, LANES:HEAD_PAD] = (rope(q_rope * qg_rope, rows) * rq).astype(BF16)
        k_nope = kv[:, 0:LANES]
        ssk = jnp.sum(k_nope * k_nope, axis=-1, keepdims=True) + kss_ref[rows, :]
        rk = lax.rsqrt(ssk * inv_d + EPS)
        k_out[rows, 0:LANES] = (k_nope * rk * kg_nope).astype(BF16)
        k_out[rows, LANES:HEAD_PAD] = (krope_ref[rows, :] * rk).astype(BF16)
        v_out[rows, 0:LANES] = kv[:, LANES:HEAD_PAD].astype(BF16)

    def scores(slot, i):
        q_in, k_in, _ = slot
        return lax.dot_general(q_in[i * tq:(i + 1) * tq, :], k_in[...], (((1,), (1,)), ((), ())),
                               preferred_element_type=F32)

    def finish(sc, slot, col, i):
        m = jnp.max(sc, axis=-1, keepdims=True)
        p = jnp.exp2(sc - m).astype(BF16)
        o = jnp.dot(p, slot[2][...], preferred_element_type=F32)
        o_ref[i * tq:(i + 1) * tq, col * LANES:(col + 1) * LANES] = (
            o[:, :V_HEAD_DIM] / o[:, V_HEAD_DIM:V_HEAD_DIM + 1]).astype(BF16)

    @pl.when(pl.program_id(1) == 0)
    def _():
        for r in range(s_len // tp):
            rows = slice(r * tp, (r + 1) * tp)
            qa = z_ref[rows, 0:Q_LORA_RANK].astype(F32)
            kva = z_ref[rows, Q_LORA_RANK:Q_LORA_RANK + KV_LORA_RANK].astype(F32)
            lane = lax.broadcasted_iota(jnp.int32, (tp, LANES), 1)
            kpe = jnp.where(lane < QK_ROPE_DIM,
                            z_ref[rows, HEAD_COLS - QK_ROPE_DIM:HEAD_COLS - QK_ROPE_DIM + LANES
                                  ].astype(F32), 0.0)
            qn_ref[rows, :] = (_rms(qa, Q_LORA_RANK) * qlg_ref[...]).astype(BF16)
            kvn_ref[rows, :] = (_rms(kva, KV_LORA_RANK) * kvlg_ref[...]).astype(BF16)
            krope_ref[rows, :] = rope(kpe * kg_rope, rows)
            kss_ref[rows, :] = jnp.sum(kpe * kpe, axis=-1, keepdims=True)
            ones_col = jnp.where(lane == 0, 1.0, 0.0).astype(BF16)
            va_ref[rows, LANES:HEAD_PAD] = ones_col
            vb_ref[rows, LANES:HEAD_PAD] = ones_col
        for r in range(n_chunks):
            prep_rows(wuq0_ref, wukv0_ref, *slot_a, r)

    preps = ((wuq1_ref, wukv1_ref) + slot_b, (wuq2_ref, wukv2_ref) + slot_a)
    slots = (slot_a, slot_b)
    total = 2 * n_chunks
    sc_next = scores(slot_a, 0)
    for t in range(total):
        hd, i = divmod(t, n_chunks)
        sc_cur = sc_next
        prep_rows(*preps[hd], i)
        if t + 1 < total:
            hd2, i2 = divmod(t + 1, n_chunks)
            sc_next = scores(slots[hd2], i2)
        finish(sc_cur, slots[hd], hd, i)


def _mla_attention(z, rot, q_lat_g, kv_lat_g, wuq_p, wukv, qg_p, kg_p, b, s, tq, tp):
    pairs = N_HEADS // 2
    const = lambda shape: pl.BlockSpec(shape, lambda bi, g: (0,) * len(shape))
    head_w = lambda rank, f: pl.BlockSpec((rank, HEAD_PAD), lambda bi, g: (0, f(g)))
    first = lambda g: 0
    odd = lambda g: 2 * g + 1
    nxt = lambda g: jnp.minimum(2 * g + 2, N_HEADS - 1)
    rows = lambda w, dt: pltpu.VMEM((s, w), dt)
    return pl.pallas_call(
        functools.partial(_mla_kernel, tq=tq, tp=tp),
        grid=(b, pairs),
        in_specs=[
            pl.BlockSpec((s, SEG), lambda bi, g: (bi, 0)),
            pl.BlockSpec((s, 3 * LANES), lambda bi, g: (bi, 0)),
            const((1, Q_LORA_RANK)), const((1, KV_LORA_RANK)),
            const((1, HEAD_PAD)), const((1, HEAD_PAD)),
            head_w(Q_LORA_RANK, first), head_w(KV_LORA_RANK, first),
            head_w(Q_LORA_RANK, odd), head_w(KV_LORA_RANK, odd),
            head_w(Q_LORA_RANK, nxt), head_w(KV_LORA_RANK, nxt),
        ],
        out_specs=pl.BlockSpec((s, 2 * V_HEAD_DIM), lambda bi, g: (bi, g)),
        out_shape=jax.ShapeDtypeStruct((b * s, N_HEADS * V_HEAD_DIM), BF16),
        scratch_shapes=[
            rows(Q_LORA_RANK, BF16), rows(KV_LORA_RANK, BF16),
            rows(LANES, F32), rows(1, F32),
            rows(HEAD_PAD, BF16), rows(HEAD_PAD, BF16), rows(HEAD_PAD, BF16),
            rows(HEAD_PAD, BF16), rows(HEAD_PAD, BF16), rows(HEAD_PAD, BF16),
        ],
        compiler_params=pltpu.CompilerParams(
            dimension_semantics=("arbitrary", "arbitrary"), vmem_limit_bytes=VMEM_LIMIT),
        name="mla_attention",
    )(z, rot, q_lat_g, kv_lat_g, qg_p, kg_p, wuq_p, wukv, wuq_p, wukv, wuq_p, wukv)


def _mix_kernel(attn_ref, ga0_ref, ga1_ref, cb_ref, cc_ref, cx_ref, gc_ref,
                ma0_ref, ma1_ref, mc0_ref, mc1_ref, ccp_ref, cxp_ref, ccn_ref, cxn_ref,
                cw_ref, cbias_ref, wba_ref, wbc_ref, o_ref, u_ref, *, tm, seq):
    i = pl.program_id(0)

    a = attn_ref[...] * jnp.concatenate([ga0_ref[...], ga1_ref[...]], axis=1)
    halves = tuple(slice(c * SEG, (c + 1) * SEG) for c in range(2))
    y_attn = [jnp.dot(a, wba_ref[:, cols], preferred_element_type=F32) for cols in halves]

    u = cc_ref[...].astype(F32) * cx_ref[...].astype(F32)
    first = (i * tm) % seq == 0
    last = ((i + 1) * tm) % seq == 0
    prev = (ccp_ref[HALO_ROWS - 1:HALO_ROWS, :].astype(F32)
            * cxp_ref[HALO_ROWS - 1:HALO_ROWS, :].astype(F32))
    nxt = ccn_ref[0:1, :].astype(F32) * cxn_ref[0:1, :].astype(F32)
    u_ref[7:8, :] = jnp.where(first, 0.0, prev)
    u_ref[8:8 + tm, :] = u
    u_ref[8 + tm:9 + tm, :] = jnp.where(last, 0.0, nxt)
    conv = (u_ref[7:7 + tm, :] * cw_ref[0:1, :] + u * cw_ref[1:2, :]
            + u_ref[9:9 + tm, :] * cw_ref[2:3, :] + cbias_ref[...])
    cv = (cb_ref[...].astype(F32) * conv * gc_ref[...].astype(F32)).astype(BF16)

    for c, (ma_ref, mc_ref) in enumerate(((ma0_ref, mc0_ref), (ma1_ref, mc1_ref))):
        y_conv = jnp.dot(cv, wbc_ref[:, halves[c]], preferred_element_type=F32)
        o_ref[:, halves[c]] = (ma_ref[...].astype(F32) * y_attn[c]
                               + mc_ref[...].astype(F32) * y_conv).astype(BF16)


def _mix(attn, z, conv_w, conv_b, wba, wbc, seq, tm):
    t, d = attn.shape[0], wba.shape[1]
    hb = tm // HALO_ROWS
    n_halo = t // HALO_ROWS
    seg = lambda c: pl.BlockSpec((tm, SEG), lambda i, c=c: (i, c))
    prev = lambda c: pl.BlockSpec((HALO_ROWS, SEG), lambda i, c=c: (jnp.maximum(i * hb - 1, 0), c))
    nxt = lambda c: pl.BlockSpec((HALO_ROWS, SEG),
                                 lambda i, c=c: (jnp.minimum((i + 1) * hb, n_halo - 1), c))
    resident = lambda shape: pl.BlockSpec(shape, lambda i: (0,) * len(shape),
                                          pipeline_mode=pl.Buffered(1))
    return pl.pallas_call(
        functools.partial(_mix_kernel, tm=tm, seq=seq),
        grid=(t // tm,),
        in_specs=[
            pl.BlockSpec((tm, attn.shape[1]), lambda i: (i, 0)),
            seg(1), seg(2), seg(3), seg(4), seg(5), seg(6), seg(7), seg(8), seg(9), seg(10),
            prev(4), prev(5), nxt(4), nxt(5),
            resident(conv_w.shape), resident(conv_b.shape),
            resident(wba.shape), resident(wbc.shape),
        ],
        out_specs=pl.BlockSpec((tm, d), lambda i: (i, 0)),
        out_shape=jax.ShapeDtypeStruct((t, d), BF16),
        scratch_shapes=[pltpu.VMEM((tm + 16, SEG), F32)],
        compiler_params=pltpu.CompilerParams(
            dimension_semantics=("arbitrary",), vmem_limit_bytes=VMEM_LIMIT),
        name="mix",
    )(attn, z, z, z, z, z, z, z, z, z, z, z, z, z, z, conv_w, conv_b, wba, wbc)


def _out_ple_kernel(x_ref, m_ref, p_ref, g_ref, wout_ref, wg_ref, wp_ref, o_ref, x1_ref):
    x1_ref[...] = x_ref[...] + jnp.dot(m_ref[...], wout_ref[...], preferred_element_type=F32)
    pb = p_ref[...].astype(BF16)
    x1 = x1_ref[...]
    h = (_rms(x1, x1.shape[-1]) * g_ref[...]).astype(BF16)
    for c in range(x1.shape[-1] // SEG):
        cols = slice(c * SEG, (c + 1) * SEG)
        proj = jnp.dot(pb, wp_ref[:, cols], preferred_element_type=F32)
        gate = _sigmoid(jnp.dot(h, wg_ref[:, cols], preferred_element_type=F32))
        o_ref[:, cols] = x1_ref[:, cols] + gate * proj


def _out_ple(x2, merged, p2, ple_g, wout, wg, wp, tm):
    t, d = x2.shape
    resident = lambda shape: pl.BlockSpec(shape, lambda i: (0,) * len(shape),
                                          pipeline_mode=pl.Buffered(1))
    return pl.pallas_call(
        _out_ple_kernel,
        grid=(t // tm,),
        in_specs=[
            pl.BlockSpec((tm, d), lambda i: (i, 0)),
            pl.BlockSpec((tm, d), lambda i: (i, 0)),
            pl.BlockSpec((tm, p2.shape[1]), lambda i: (i, 0)),
            resident((1, d)), resident(wout.shape), resident(wg.shape), resident(wp.shape),
        ],
        out_specs=pl.BlockSpec((tm, d), lambda i: (i, 0)),
        out_shape=jax.ShapeDtypeStruct((t, d), F32),
        scratch_shapes=[pltpu.VMEM((tm, d), F32)],
        compiler_params=pltpu.CompilerParams(
            dimension_semantics=("arbitrary",), vmem_limit_bytes=VMEM_LIMIT),
        name="out_ple",
    )(x2, merged, p2, ple_g, wout, wg, wp)


def _layer(x2, p2, pos, freq, b, s, norm_g, w_in, q_lat_g, kv_lat_g, w_uq, w_ukv,
           q_norm_g, k_norm_g, conv_w, conv_b, w_branch_attn, w_branch_conv,
           w_out, ple_norm_g, w_ple_gate, w_ple_proj):
    d = x2.shape[1]
    wuq_p = jnp.pad(w_uq.reshape(Q_LORA_RANK, N_HEADS, QK_HEAD_DIM),
                    ((0, 0), (0, 0), (0, HEAD_PAD - QK_HEAD_DIM))
                    ).reshape(Q_LORA_RANK, N_HEADS * HEAD_PAD).astype(BF16)
    pad_g = lambda g: jnp.pad(g, (0, HEAD_PAD - QK_HEAD_DIM)).reshape(1, HEAD_PAD)

    z, rot = _in_proj(x2, norm_g.reshape(1, d), jnp.swapaxes(w_in, 0, 1), pos, freq,
                      tm=IN_PROJ_TM)
    attn = _mla_attention(z, rot, q_lat_g.reshape(1, -1), kv_lat_g.reshape(1, -1),
                          wuq_p, w_ukv.astype(BF16), pad_g(q_norm_g), pad_g(k_norm_g),
                          b, s, tq=512, tp=512)
    merged = _mix(attn, z, conv_w, conv_b.reshape(1, -1), w_branch_attn.astype(BF16),
                  w_branch_conv.astype(BF16), seq=s, tm=512)
    return _out_ple(x2, merged, p2, ple_norm_g.reshape(1, d), w_out.astype(BF16),
                    w_ple_gate.astype(BF16), w_ple_proj.astype(BF16), tm=512)


def kernel(x, p, positions, norm_g, w_in, q_lat_g, kv_lat_g, w_uq, w_ukv, q_norm_g, k_norm_g, conv_w, conv_b, w_branch_attn, w_branch_conv, w_out, ple_norm_g, w_ple_gate, w_ple_proj):
    b, s, d = x.shape
    depth = p.shape[0]
    x2 = x.reshape(b * s, d)
    half = QK_ROPE_DIM // 2
    pos = positions.reshape(b * s // IN_PROJ_TM, ROT_GROUPS, IN_PROJ_TM // ROT_GROUPS)
    pos = jnp.repeat(jnp.swapaxes(pos, 1, 2), half, axis=2).reshape(b * s // ROT_GROUPS, LANES)
    inv_freq = 1.0 / (ROPE_THETA ** (jnp.arange(0, QK_ROPE_DIM, 2, dtype=F32) / QK_ROPE_DIM))
    freq = jnp.tile(inv_freq, ROT_GROUPS).reshape(1, LANES)
    for i in range(depth):
        x2 = _layer(x2, p[i].reshape(b * s, -1), pos, freq, b, s, norm_g[i], w_in[i],
                    q_lat_g[i], kv_lat_g[i], w_uq[i], w_ukv[i], q_norm_g[i], k_norm_g[i],
                    conv_w[i], conv_b[i], w_branch_attn[i], w_branch_conv[i], w_out[i],
                    ple_norm_g[i], w_ple_gate[i], w_ple_proj[i])
    return x2.reshape(b, s, d)
```

```python
import functools

import jax
import jax.numpy as jnp
from jax import lax
from jax.experimental import pallas as pl
from jax.experimental.pallas import tpu as pltpu

N_HEADS = 16
QK_NOPE_DIM = 128
QK_ROPE_DIM = 64
QK_HEAD_DIM = QK_NOPE_DIM + QK_ROPE_DIM
V_HEAD_DIM = 128
Q_LORA_RANK = 512
KV_LORA_RANK = 256
ROPE_THETA = 10000.0
EPS = 1e-6
LOG2E = 1.4426950408889634

LANES = 128
HEAD_PAD = 2 * LANES
SEG = 1024
HEAD_COLS = Q_LORA_RANK + KV_LORA_RANK + QK_ROPE_DIM
HEAD_COLS_ALIGN = 64
assert HEAD_COLS % HEAD_COLS_ALIGN == 0 and SEG % HEAD_COLS_ALIGN == 0
HALO_ROWS = 16
VMEM_LIMIT = 56 * 1024 * 1024

BF16 = jnp.bfloat16
F32 = jnp.float32


def _rms(v, axis_size):
    return v * lax.rsqrt(jnp.sum(v * v, axis=-1, keepdims=True) * (1.0 / axis_size) + EPS)


def _sigmoid(v):
    return 0.5 * jnp.tanh(0.5 * v) + 0.5


def _silu(v):
    return v * _sigmoid(v)


def _identity(v):
    return v


Z_TILE_ACTS = (_identity, _silu, _silu, _identity, _identity, _identity, _silu,
               _sigmoid, _sigmoid, _sigmoid, _sigmoid)
ROT_TILE = 3
ROT_GROUPS = LANES // (QK_ROPE_DIM // 2)
IN_PROJ_TM = 1024


def _in_proj_kernel(x_ref, g_ref, wt_ref, pos_ref, freq_ref, z_ref, rot_ref, h_ref):
    j = pl.program_id(1)

    @pl.when(j == 0)
    def _():
        x = x_ref[...]
        h_ref[...] = (_rms(x, x.shape[-1]) * g_ref[...]).astype(BF16)

    def rotary_tables():
        half = QK_ROPE_DIM // 2
        rq = x_ref.shape[0] // ROT_GROUPS
        lane = lax.broadcasted_iota(jnp.int32, (rq, LANES), 1)
        ang = pos_ref[...].astype(F32) * freq_ref[...]
        cosv, sinv = jnp.cos(ang), jnp.sin(ang)
        for q in range(ROT_GROUPS):
            rows = slice(q * rq, (q + 1) * rq)
            to_x1 = (lambda v: v) if q == 0 else (
                lambda v: pltpu.roll(v, LANES - q * half, axis=1))
            to_x2 = lambda v: pltpu.roll(v, (LANES - q * half + half) % LANES, axis=1)
            rot_ref[rows, 0:LANES] = jnp.where(lane < half, to_x1(cosv), to_x2(cosv))
            rot_ref[rows, LANES:2 * LANES] = jnp.where(lane < half, -to_x1(sinv), 0.0)
            rot_ref[rows, 2 * LANES:3 * LANES] = jnp.where(
                (lane >= half) & (lane < QK_ROPE_DIM), to_x2(sinv), 0.0)

    def is_act(*acts):
        return functools.reduce(jnp.logical_or,
                                [j == tile for tile, a in enumerate(Z_TILE_ACTS) if a in acts])

    def matmul():
        return lax.dot_general(h_ref[...], wt_ref[...].astype(BF16), (((1,), (1,)), ((), ())),
                               preferred_element_type=F32)

    @pl.when(is_act(_identity))
    def _():
        z_ref[...] = matmul().astype(BF16)

    @pl.when(is_act(_silu, _sigmoid))
    def _():
        hv = 0.5 * matmul()
        a = jnp.where(is_act(_silu), hv, 0.5)
        z_ref[...] = (a * jnp.tanh(hv) + a).astype(BF16)

    @pl.when(j == ROT_TILE)
    def _():
        rotary_tables()


def _in_proj(x2, norm_g, w_t, pos, freq, tm):
    t, d = x2.shape
    n_tiles = len(Z_TILE_ACTS)
    assert w_t.shape == (HEAD_COLS + (n_tiles - 1) * SEG, d)
    return pl.pallas_call(
        _in_proj_kernel,
        grid=(t // tm, n_tiles),
        in_specs=[
            pl.BlockSpec((tm, d), lambda i, j: (i, 0)),
            pl.BlockSpec((1, d), lambda i, j: (0, 0)),
            pl.BlockSpec((pl.Element(SEG), pl.Element(d)),
                         lambda i, j: (pl.multiple_of(
                             jnp.maximum(j * SEG - (SEG - HEAD_COLS), 0), HEAD_COLS_ALIGN), 0)),
            pl.BlockSpec((tm // ROT_GROUPS, LANES), lambda i, j: (i, 0)),
            pl.BlockSpec((1, LANES), lambda i, j: (0, 0)),
        ],
        out_specs=[pl.BlockSpec((tm, SEG), lambda i, j: (i, j)),
                   pl.BlockSpec((tm, 3 * LANES), lambda i, j: (i, 0))],
        out_shape=[jax.ShapeDtypeStruct((t, n_tiles * SEG), BF16),
                   jax.ShapeDtypeStruct((t, 3 * LANES), F32)],
        scratch_shapes=[pltpu.VMEM((tm, d), BF16)],
        compiler_params=pltpu.CompilerParams(
            dimension_semantics=("arbitrary", "arbitrary"),
            vmem_limit_bytes=VMEM_LIMIT),
        name="in_proj",
    )(x2, norm_g, w_t, pos, freq)


def _mla_kernel(*refs, tq, tp, hps):
    z_ref, rot_ref, qlg_ref, kvlg_ref, qg_ref, kg_ref, wuq0_ref, wukv0_ref = refs[:8]
    w_following = refs[8:8 + 2 * hps]
    o_ref = refs[8 + 2 * hps]
    (qn_ref, kvn_ref, krope_ref, kss_ref,
     qa_ref, ka_ref, va_ref, qb_ref, kb_ref, vb_ref) = refs[9 + 2 * hps:]
    s_len = z_ref.shape[0]
    n_chunks = s_len // tq
    slot_a = (qa_ref, ka_ref, va_ref)
    slot_b = (qb_ref, kb_ref, vb_ref)
    half = QK_ROPE_DIM // 2
    scale = QK_HEAD_DIM ** -0.5 * LOG2E
    inv_d = 1.0 / QK_HEAD_DIM
    qg_nope, qg_rope = qg_ref[:, 0:LANES], qg_ref[:, LANES:HEAD_PAD]
    kg_nope, kg_rope = kg_ref[:, 0:LANES], kg_ref[:, LANES:HEAD_PAD]

    def rope(u, rows):
        return (u * rot_ref[rows, 0:LANES]
                + pltpu.roll(u, LANES - half, axis=1) * rot_ref[rows, LANES:2 * LANES]
                + pltpu.roll(u, half, axis=1) * rot_ref[rows, 2 * LANES:3 * LANES])

    def prep_rows(wuq_ref, wukv_ref, q_out, k_out, v_out, r):
        rows = slice(r * tq, (r + 1) * tq)
        q = jnp.dot(qn_ref[rows, :], wuq_ref[...], preferred_element_type=F32)
        kv = jnp.dot(kvn_ref[rows, :], wukv_ref[...], preferred_element_type=F32)
        q_nope, q_rope = q[:, 0:LANES], q[:, LANES:HEAD_PAD]
        ss = jnp.sum(q_nope * q_nope + q_rope * q_rope, axis=-1, keepdims=True)
        rq = lax.rsqrt(ss * inv_d + EPS) * scale
        q_out[rows, 0:LANES] = (q_nope * rq * qg_nope).astype(BF16)
        q_out[rows, LANES:HEAD_PAD] = (rope(q_rope * qg_rope, rows) * rq).astype(BF16)
        k_nope = kv[:, 0:LANES]
        ssk = jnp.sum(k_nope * k_nope, axis=-1, keepdims=True) + kss_ref[rows, :]
        rk = lax.rsqrt(ssk * inv_d + EPS)
        k_out[rows, 0:LANES] = (k_nope * rk * kg_nope).astype(BF16)
        k_out[rows, LANES:HEAD_PAD] = (krope_ref[rows, :] * rk).astype(BF16)
        v_out[rows, 0:LANES] = kv[:, LANES:HEAD_PAD].astype(BF16)

    def scores(slot, i):
        q_in, k_in, _ = slot
        return lax.dot_general(q_in[i * tq:(i + 1) * tq, :], k_in[...], (((1,), (1,)), ((), ())),
                               preferred_element_type=F32)

    def finish(sc, slot, col, i):
        m = jnp.max(sc, axis=-1, keepdims=True)
        p = jnp.exp2(sc - m).astype(BF16)
        o = jnp.dot(p, slot[2][...], preferred_element_type=F32)
        o_ref[i * tq:(i + 1) * tq, col * LANES:(col + 1) * LANES] = (
            o[:, :V_HEAD_DIM] / o[:, V_HEAD_DIM:V_HEAD_DIM + 1]).astype(BF16)

    @pl.when(pl.program_id(1) == 0)
    def _():
        for r in range(s_len // tp):
            rows = slice(r * tp, (r + 1) * tp)
            qa = z_ref[rows, 0:Q_LORA_RANK].astype(F32)
            kva = z_ref[rows, Q_LORA_RANK:Q_LORA_RANK + KV_LORA_RANK].astype(F32)
            lane = lax.broadcasted_iota(jnp.int32, (tp, LANES), 1)
            kpe = jnp.where(lane < QK_ROPE_DIM,
                            z_ref[rows, HEAD_COLS - QK_ROPE_DIM:HEAD_COLS - QK_ROPE_DIM + LANES
                                  ].astype(F32), 0.0)
            qn_ref[rows, :] = (_rms(qa, Q_LORA_RANK) * qlg_ref[...]).astype(BF16)
            kvn_ref[rows, :] = (_rms(kva, KV_LORA_RANK) * kvlg_ref[...]).astype(BF16)
            krope_ref[rows, :] = rope(kpe * kg_rope, rows)
            kss_ref[rows, :] = jnp.sum(kpe * kpe, axis=-1, keepdims=True)
            ones_col = jnp.where(lane == 0, 1.0, 0.0).astype(BF16)
            va_ref[rows, LANES:HEAD_PAD] = ones_col
            vb_ref[rows, LANES:HEAD_PAD] = ones_col
        for r in range(n_chunks):
            prep_rows(wuq0_ref, wukv0_ref, *slot_a, r)

    slots = (slot_a, slot_b)
    total = hps * n_chunks
    sc_next = scores(slot_a, 0)
    for t in range(total):
        hd, i = divmod(t, n_chunks)
        sc_cur = sc_next
        prep_rows(*w_following[2 * hd:2 * hd + 2], *slots[(hd + 1) % 2], i)
        if t + 1 < total:
            hd2, i2 = divmod(t + 1, n_chunks)
            sc_next = scores(slots[hd2 % 2], i2)
        finish(sc_cur, slots[hd % 2], hd, i)


def _mla_attention(z, rot, q_lat_g, kv_lat_g, wuq_p, wukv, qg_p, kg_p, b, s, tq, tp, hps):
    assert hps % 2 == 0 and N_HEADS % hps == 0
    const = lambda shape: pl.BlockSpec(shape, lambda bi, g: (0,) * len(shape))
    head_w = lambda rank, f: pl.BlockSpec((rank, HEAD_PAD), lambda bi, g: (0, f(g)))
    following = lambda k: (lambda g: jnp.minimum(hps * g + k, N_HEADS - 1))
    following_specs = []
    for k in range(1, hps + 1):
        following_specs += [head_w(Q_LORA_RANK, following(k)), head_w(KV_LORA_RANK, following(k))]
    rows = lambda w, dt: pltpu.VMEM((s, w), dt)
    return pl.pallas_call(
        functools.partial(_mla_kernel, tq=tq, tp=tp, hps=hps),
        grid=(b, N_HEADS // hps),
        in_specs=[
            pl.BlockSpec((s, SEG), lambda bi, g: (bi, 0)),
            pl.BlockSpec((s, 3 * LANES), lambda bi, g: (bi, 0)),
            const((1, Q_LORA_RANK)), const((1, KV_LORA_RANK)),
            const((1, HEAD_PAD)), const((1, HEAD_PAD)),
            head_w(Q_LORA_RANK, lambda g: 0), head_w(KV_LORA_RANK, lambda g: 0),
        ] + following_specs,
        out_specs=pl.BlockSpec((s, hps * V_HEAD_DIM), lambda bi, g: (bi, g)),
        out_shape=jax.ShapeDtypeStruct((b * s, N_HEADS * V_HEAD_DIM), BF16),
        scratch_shapes=[
            rows(Q_LORA_RANK, BF16), rows(KV_LORA_RANK, BF16),
            rows(LANES, F32), rows(1, F32),
            rows(HEAD_PAD, BF16), rows(HEAD_PAD, BF16), rows(HEAD_PAD, BF16),
            rows(HEAD_PAD, BF16), rows(HEAD_PAD, BF16), rows(HEAD_PAD, BF16),
        ],
        compiler_params=pltpu.CompilerParams(
            dimension_semantics=("arbitrary", "arbitrary"), vmem_limit_bytes=VMEM_LIMIT),
        name="mla_attention",
    )(z, rot, q_lat_g, kv_lat_g, qg_p, kg_p, *([wuq_p, wukv] * (hps + 1)))


def _mix_kernel(attn_ref, ga0_ref, ga1_ref, cb_ref, cc_ref, cx_ref, gc_ref,
                ma0_ref, ma1_ref, mc0_ref, mc1_ref, ccp_ref, cxp_ref, ccn_ref, cxn_ref,
                cw_ref, cbias_ref, wba_ref, wbc_ref, o_ref, u_ref, *, tm, seq):
    i = pl.program_id(0)

    a = attn_ref[...] * jnp.concatenate([ga0_ref[...], ga1_ref[...]], axis=1)
    halves = tuple(slice(c * SEG, (c + 1) * SEG) for c in range(2))
    y_attn = [jnp.dot(a, wba_ref[:, cols], preferred_element_type=F32) for cols in halves]

    u = cc_ref[...].astype(F32) * cx_ref[...].astype(F32)
    first = (i * tm) % seq == 0
    last = ((i + 1) * tm) % seq == 0
    prev = (ccp_ref[HALO_ROWS - 1:HALO_ROWS, :].astype(F32)
            * cxp_ref[HALO_ROWS - 1:HALO_ROWS, :].astype(F32))
    nxt = ccn_ref[0:1, :].astype(F32) * cxn_ref[0:1, :].astype(F32)
    u_ref[7:8, :] = jnp.where(first, 0.0, prev)
    u_ref[8:8 + tm, :] = u
    u_ref[8 + tm:9 + tm, :] = jnp.where(last, 0.0, nxt)
    conv = (u_ref[7:7 + tm, :] * cw_ref[0:1, :] + u * cw_ref[1:2, :]
            + u_ref[9:9 + tm, :] * cw_ref[2:3, :] + cbias_ref[...])
    cv = (cb_ref[...].astype(F32) * conv * gc_ref[...].astype(F32)).astype(BF16)

    for c, (ma_ref, mc_ref) in enumerate(((ma0_ref, mc0_ref), (ma1_ref, mc1_ref))):
        y_conv = jnp.dot(cv, wbc_ref[:, halves[c]], preferred_element_type=F32)
        o_ref[:, halves[c]] = (ma_ref[...].astype(F32) * y_attn[c]
                               + mc_ref[...].astype(F32) * y_conv).astype(BF16)


def _mix(attn, z, conv_w, conv_b, wba, wbc, seq, tm):
    t, d = attn.shape[0], wba.shape[1]
    hb = tm // HALO_ROWS
    n_halo = t // HALO_ROWS
    seg = lambda c: pl.BlockSpec((tm, SEG), lambda i, c=c: (i, c))
    prev = lambda c: pl.BlockSpec((HALO_ROWS, SEG), lambda i, c=c: (jnp.maximum(i * hb - 1, 0), c))
    nxt = lambda c: pl.BlockSpec((HALO_ROWS, SEG),
                                 lambda i, c=c: (jnp.minimum((i + 1) * hb, n_halo - 1), c))
    resident = lambda shape: pl.BlockSpec(shape, lambda i: (0,) * len(shape),
                                          pipeline_mode=pl.Buffered(1))
    return pl.pallas_call(
        functools.partial(_mix_kernel, tm=tm, seq=seq),
        grid=(t // tm,),
        in_specs=[
            pl.BlockSpec((tm, attn.shape[1]), lambda i: (i, 0)),
            seg(1), seg(2), seg(3), seg(4), seg(5), seg(6), seg(7), seg(8), seg(9), seg(10),
            prev(4), prev(5), nxt(4), nxt(5),
            resident(conv_w.shape), resident(conv_b.shape),
            resident(wba.shape), resident(wbc.shape),
        ],
        out_specs=pl.BlockSpec((tm, d), lambda i: (i, 0)),
        out_shape=jax.ShapeDtypeStruct((t, d), BF16),
        scratch_shapes=[pltpu.VMEM((tm + 16, SEG), F32)],
        compiler_params=pltpu.CompilerParams(
            dimension_semantics=("arbitrary",), vmem_limit_bytes=VMEM_LIMIT),
        name="mix",
    )(attn, z, z, z, z, z, z, z, z, z, z, z, z, z, z, conv_w, conv_b, wba, wbc)


def _out_ple_kernel(x_ref, m_ref, p_ref, g_ref, wout_ref, wg_ref, wp_ref, o_ref, x1_ref):
    x1_ref[...] = x_ref[...] + jnp.dot(m_ref[...], wout_ref[...], preferred_element_type=F32)
    pb = p_ref[...].astype(BF16)
    x1 = x1_ref[...]
    h = (_rms(x1, x1.shape[-1]) * g_ref[...]).astype(BF16)
    for c in range(x1.shape[-1] // SEG):
        cols = slice(c * SEG, (c + 1) * SEG)
        proj = jnp.dot(pb, wp_ref[:, cols], preferred_element_type=F32)
        gate = _sigmoid(jnp.dot(h, wg_ref[:, cols], preferred_element_type=F32))
        o_ref[:, cols] = x1_ref[:, cols] + gate * proj


def _out_ple(x2, merged, p2, ple_g, wout, wg, wp, tm):
    t, d = x2.shape
    resident = lambda shape: pl.BlockSpec(shape, lambda i: (0,) * len(shape),
                                          pipeline_mode=pl.Buffered(1))
    return pl.pallas_call(
        _out_ple_kernel,
        grid=(t // tm,),
        in_specs=[
            pl.BlockSpec((tm, d), lambda i: (i, 0)),
            pl.BlockSpec((tm, d), lambda i: (i, 0)),
            pl.BlockSpec((tm, p2.shape[1]), lambda i: (i, 0)),
            resident((1, d)), resident(wout.shape), resident(wg.shape), resident(wp.shape),
        ],
        out_specs=pl.BlockSpec((tm, d), lambda i: (i, 0)),
        out_shape=jax.ShapeDtypeStruct((t, d), F32),
        scratch_shapes=[pltpu.VMEM((tm, d), F32)],
        compiler_params=pltpu.CompilerParams(
            dimension_semantics=("arbitrary",), vmem_limit_bytes=VMEM_LIMIT),
        name="out_ple",
    )(x2, merged, p2, ple_g, wout, wg, wp)


def _layer(x2, p2, pos, freq, b, s, norm_g, w_in, q_lat_g, kv_lat_g, w_uq, w_ukv,
           q_norm_g, k_norm_g, conv_w, conv_b, w_branch_attn, w_branch_conv,
           w_out, ple_norm_g, w_ple_gate, w_ple_proj):
    d = x2.shape[1]
    wuq_p = jnp.pad(w_uq.reshape(Q_LORA_RANK, N_HEADS, QK_HEAD_DIM),
                    ((0, 0), (0, 0), (0, HEAD_PAD - QK_HEAD_DIM))
                    ).reshape(Q_LORA_RANK, N_HEADS * HEAD_PAD).astype(BF16)
    pad_g = lambda g: jnp.pad(g, (0, HEAD_PAD - QK_HEAD_DIM)).reshape(1, HEAD_PAD)

    z, rot = _in_proj(x2, norm_g.reshape(1, d), jnp.swapaxes(w_in, 0, 1), pos, freq,
                      tm=IN_PROJ_TM)
    attn = _mla_attention(z, rot, q_lat_g.reshape(1, -1), kv_lat_g.reshape(1, -1),
                          wuq_p, w_ukv.astype(BF16), pad_g(q_norm_g), pad_g(k_norm_g),
                          b, s, tq=512, tp=512, hps=4)
    merged = _mix(attn, z, conv_w, conv_b.reshape(1, -1), w_branch_attn.astype(BF16),
                  w_branch_conv.astype(BF16), seq=s, tm=512)
    return _out_ple(x2, merged, p2, ple_norm_g.reshape(1, d), w_out.astype(BF16),
                    w_ple_gate.astype(BF16), w_ple_proj.astype(BF16), tm=512)


def kernel(x, p, positions, norm_g, w_in, q_lat_g, kv_lat_g, w_uq, w_ukv, q_norm_g, k_norm_g, conv_w, conv_b, w_branch_attn, w_branch_conv, w_out, ple_norm_g, w_ple_gate, w_ple_proj):
    b, s, d = x.shape
    depth = p.shape[0]
    x2 = x.reshape(b * s, d)
    half = QK_ROPE_DIM // 2
    pos = positions.reshape(b * s // IN_PROJ_TM, ROT_GROUPS, IN_PROJ_TM // ROT_GROUPS)
    pos = jnp.repeat(jnp.swapaxes(pos, 1, 2), half, axis=2).reshape(b * s // ROT_GROUPS, LANES)
    inv_freq = 1.0 / (ROPE_THETA ** (jnp.arange(0, QK_ROPE_DIM, 2, dtype=F32) / QK_ROPE_DIM))
    freq = jnp.tile(inv_freq, ROT_GROUPS).reshape(1, LANES)
    for i in range(depth):
        x2 = _layer(x2, p[i].reshape(b * s, -1), pos, freq, b, s, norm_g[i], w_in[i],
                    q_lat_g[i], kv_lat_g[i], w_uq[i], w_ukv[i], q_norm_g[i], k_norm_g[i],
                    conv_w[i], conv_b[i], w_branch_attn[i], w_branch_conv[i], w_out[i],
                    ple_norm_g[i], w_ple_gate[i], w_ple_proj[i])
    return x2.reshape(b, s, d)
```

```python
import functools

import jax
import jax.numpy as jnp
from jax import lax
from jax.experimental import pallas as pl
from jax.experimental.pallas import tpu as pltpu

N_HEADS = 16
QK_NOPE_DIM = 128
QK_ROPE_DIM = 64
QK_HEAD_DIM = QK_NOPE_DIM + QK_ROPE_DIM
V_HEAD_DIM = 128
Q_LORA_RANK = 512
KV_LORA_RANK = 256
ROPE_THETA = 10000.0
EPS = 1e-6
LOG2E = 1.4426950408889634

LANES = 128
HEAD_PAD = 2 * LANES
SEG = 1024
HEAD_COLS = Q_LORA_RANK + KV_LORA_RANK + QK_ROPE_DIM
HEAD_COLS_ALIGN = 64
assert HEAD_COLS % HEAD_COLS_ALIGN == 0 and SEG % HEAD_COLS_ALIGN == 0
HALO_ROWS = 16
VMEM_LIMIT = 56 * 1024 * 1024

BF16 = jnp.bfloat16
F32 = jnp.float32


def _rms(v, axis_size):
    return v * lax.rsqrt(jnp.sum(v * v, axis=-1, keepdims=True) * (1.0 / axis_size) + EPS)


def _sigmoid(v):
    return 0.5 * jnp.tanh(0.5 * v) + 0.5


def _silu(v):
    return v * _sigmoid(v)


def _identity(v):
    return v


Z_TILE_ACTS = (_identity, _silu, _silu, _identity, _identity, _identity, _silu,
               _sigmoid, _sigmoid, _sigmoid, _sigmoid)
ROT_TILE = 3
ROT_GROUPS = LANES // (QK_ROPE_DIM // 2)
IN_PROJ_TM = 1024


def _in_proj_kernel(x_ref, g_ref, wt_ref, pos_ref, freq_ref, z_ref, rot_ref, h_ref):
    j = pl.program_id(1)

    @pl.when(j == 0)
    def _():
        x = x_ref[...]
        h_ref[...] = (_rms(x, x.shape[-1]) * g_ref[...]).astype(BF16)

    def rotary_tables():
        half = QK_ROPE_DIM // 2
        rq = x_ref.shape[0] // ROT_GROUPS
        lane = lax.broadcasted_iota(jnp.int32, (rq, LANES), 1)
        ang = pos_ref[...].astype(F32) * freq_ref[...]
        cosv, sinv = jnp.cos(ang), jnp.sin(ang)
        for q in range(ROT_GROUPS):
            rows = slice(q * rq, (q + 1) * rq)
            to_x1 = (lambda v: v) if q == 0 else (
                lambda v: pltpu.roll(v, LANES - q * half, axis=1))
            to_x2 = lambda v: pltpu.roll(v, (LANES - q * half + half) % LANES, axis=1)
            rot_ref[rows, 0:LANES] = jnp.where(lane < half, to_x1(cosv), to_x2(cosv))
            rot_ref[rows, LANES:2 * LANES] = jnp.where(lane < half, -to_x1(sinv), 0.0)
            rot_ref[rows, 2 * LANES:3 * LANES] = jnp.where(
                (lane >= half) & (lane < QK_ROPE_DIM), to_x2(sinv), 0.0)

    def is_act(*acts):
        return functools.reduce(jnp.logical_or,
                                [j == tile for tile, a in enumerate(Z_TILE_ACTS) if a in acts])

    def matmul():
        return lax.dot_general(h_ref[...], wt_ref[...].astype(BF16), (((1,), (1,)), ((), ())),
                               preferred_element_type=F32)

    @pl.when(is_act(_identity))
    def _():
        z_ref[...] = matmul().astype(BF16)

    @pl.when(is_act(_silu, _sigmoid))
    def _():
        hv = 0.5 * matmul()
        a = jnp.where(is_act(_silu), hv, 0.5)
        z_ref[...] = (a * jnp.tanh(hv) + a).astype(BF16)

    @pl.when(j == ROT_TILE)
    def _():
        rotary_tables()


def _in_proj(x2, norm_g, w_t, pos, freq, tm):
    t, d = x2.shape
    n_tiles = len(Z_TILE_ACTS)
    assert w_t.shape == (HEAD_COLS + (n_tiles - 1) * SEG, d)
    return pl.pallas_call(
        _in_proj_kernel,
        grid=(t // tm, n_tiles),
        in_specs=[
            pl.BlockSpec((tm, d), lambda i, j: (i, 0)),
            pl.BlockSpec((1, d), lambda i, j: (0, 0)),
            pl.BlockSpec((pl.Element(SEG), pl.Element(d)),
                         lambda i, j: (pl.multiple_of(
                             jnp.maximum(j * SEG - (SEG - HEAD_COLS), 0), HEAD_COLS_ALIGN), 0)),
            pl.BlockSpec((tm // ROT_GROUPS, LANES), lambda i, j: (i, 0)),
            pl.BlockSpec((1, LANES), lambda i, j: (0, 0)),
        ],
        out_specs=[pl.BlockSpec((tm, SEG), lambda i, j: (i, j)),
                   pl.BlockSpec((tm, 3 * LANES), lambda i, j: (i, 0))],
        out_shape=[jax.ShapeDtypeStruct((t, n_tiles * SEG), BF16),
                   jax.ShapeDtypeStruct((t, 3 * LANES), F32)],
        scratch_shapes=[pltpu.VMEM((tm, d), BF16)],
        compiler_params=pltpu.CompilerParams(
            dimension_semantics=("arbitrary", "arbitrary"),
            vmem_limit_bytes=VMEM_LIMIT),
        name="in_proj",
    )(x2, norm_g, w_t, pos, freq)


def _mla_kernel(*refs, tq, tp, hps):
    z_ref, rot_ref, qlg_ref, kvlg_ref, qg_ref, kg_ref, wuq0_ref, wukv0_ref = refs[:8]
    w_following = refs[8:8 + 2 * hps]
    o_ref = refs[8 + 2 * hps]
    (qn_ref, kvn_ref, krope_ref, kss_ref,
     qa_ref, ka_ref, va_ref, qb_ref, kb_ref, vb_ref) = refs[9 + 2 * hps:]
    s_len = z_ref.shape[0]
    n_chunks = s_len // tq
    slot_a = (qa_ref, ka_ref, va_ref)
    slot_b = (qb_ref, kb_ref, vb_ref)
    half = QK_ROPE_DIM // 2
    scale = QK_HEAD_DIM ** -0.5 * LOG2E
    inv_d = 1.0 / QK_HEAD_DIM
    qg_nope, qg_rope = qg_ref[:, 0:LANES], qg_ref[:, LANES:HEAD_PAD]
    kg_nope, kg_rope = kg_ref[:, 0:LANES], kg_ref[:, LANES:HEAD_PAD]

    def rope(u, rows):
        return (u * rot_ref[rows, 0:LANES]
                + pltpu.roll(u, LANES - half, axis=1) * rot_ref[rows, LANES:2 * LANES]
                + pltpu.roll(u, half, axis=1) * rot_ref[rows, 2 * LANES:3 * LANES])

    def prep_rows(wuq_ref, wukv_ref, q_out, k_out, v_out, r):
        rows = slice(r * tq, (r + 1) * tq)
        q = jnp.dot(qn_ref[rows, :], wuq_ref[...], preferred_element_type=F32)
        kv = jnp.dot(kvn_ref[rows, :], wukv_ref[...], preferred_element_type=F32)
        q_nope, q_rope = q[:, 0:LANES], q[:, LANES:HEAD_PAD]
        ss = jnp.sum(q_nope * q_nope + q_rope * q_rope, axis=-1, keepdims=True)
        rq = lax.rsqrt(ss * inv_d + EPS) * scale
        q_out[rows, 0:LANES] = (q_nope * rq * qg_nope).astype(BF16)
        q_out[rows, LANES:HEAD_PAD] = (rope(q_rope * qg_rope, rows) * rq).astype(BF16)
        k_nope = kv[:, 0:LANES]
        ssk = jnp.sum(k_nope * k_nope, axis=-1, keepdims=True) + kss_ref[rows, :]
        rk = lax.rsqrt(ssk * inv_d + EPS)
        k_out[rows, 0:LANES] = (k_nope * rk * kg_nope).astype(BF16)
        k_out[rows, LANES:HEAD_PAD] = (krope_ref[rows, :] * rk).astype(BF16)
        v_out[rows, 0:LANES] = kv[:, LANES:HEAD_PAD].astype(BF16)

    def scores(slot, i):
        q_in, k_in, _ = slot
        return lax.dot_general(q_in[i * tq:(i + 1) * tq, :], k_in[...], (((1,), (1,)), ((), ())),
                               preferred_element_type=F32)

    def finish(sc, slot, col, i):
        m = jnp.max(sc, axis=-1, keepdims=True)
        p = jnp.exp2(sc - m).astype(BF16)
        o = jnp.dot(p, slot[2][...], preferred_element_type=F32)
        o_ref[i * tq:(i + 1) * tq, col * LANES:(col + 1) * LANES] = (
            o[:, :V_HEAD_DIM] / o[:, V_HEAD_DIM:V_HEAD_DIM + 1]).astype(BF16)

    @pl.when(pl.program_id(1) == 0)
    def _():
        for r in range(s_len // tp):
            rows = slice(r * tp, (r + 1) * tp)
            qa = z_ref[rows, 0:Q_LORA_RANK].astype(F32)
            kva = z_ref[rows, Q_LORA_RANK:Q_LORA_RANK + KV_LORA_RANK].astype(F32)
            lane = lax.broadcasted_iota(jnp.int32, (tp, LANES), 1)
            kpe = jnp.where(lane < QK_ROPE_DIM,
                            z_ref[rows, HEAD_COLS - QK_ROPE_DIM:HEAD_COLS - QK_ROPE_DIM + LANES
                                  ].astype(F32), 0.0)
            qn_ref[rows, :] = (_rms(qa, Q_LORA_RANK) * qlg_ref[...]).astype(BF16)
            kvn_ref[rows, :] = (_rms(kva, KV_LORA_RANK) * kvlg_ref[...]).astype(BF16)
            krope_ref[rows, :] = rope(kpe * kg_rope, rows)
            kss_ref[rows, :] = jnp.sum(kpe * kpe, axis=-1, keepdims=True)
            ones_col = jnp.where(lane == 0, 1.0, 0.0).astype(BF16)
            va_ref[rows, LANES:HEAD_PAD] = ones_col
            vb_ref[rows, LANES:HEAD_PAD] = ones_col
        for r in range(n_chunks):
            prep_rows(wuq0_ref, wukv0_ref, *slot_a, r)

    slots = (slot_a, slot_b)
    total = hps * n_chunks
    sc_next = scores(slot_a, 0)
    for t in range(total):
        hd, i = divmod(t, n_chunks)
        sc_cur = sc_next
        prep_rows(*w_following[2 * hd:2 * hd + 2], *slots[(hd + 1) % 2], i)
        if t + 1 < total:
            hd2, i2 = divmod(t + 1, n_chunks)
            sc_next = scores(slots[hd2 % 2], i2)
        finish(sc_cur, slots[hd % 2], hd, i)


def _mla_attention(z, rot, q_lat_g, kv_lat_g, wuq_p, wukv, qg_p, kg_p, b, s, tq, tp, hps):
    assert hps % 2 == 0 and N_HEADS % hps == 0
    const = lambda shape: pl.BlockSpec(shape, lambda bi, g: (0,) * len(shape))
    head_w = lambda rank, f: pl.BlockSpec((rank, HEAD_PAD), lambda bi, g: (0, f(g)))
    following = lambda k: (lambda g: jnp.minimum(hps * g + k, N_HEADS - 1))
    following_specs = []
    for k in range(1, hps + 1):
        following_specs += [head_w(Q_LORA_RANK, following(k)), head_w(KV_LORA_RANK, following(k))]
    rows = lambda w, dt: pltpu.VMEM((s, w), dt)
    return pl.pallas_call(
        functools.partial(_mla_kernel, tq=tq, tp=tp, hps=hps),
        grid=(b, N_HEADS // hps),
        in_specs=[
            pl.BlockSpec((s, SEG), lambda bi, g: (bi, 0)),
            pl.BlockSpec((s, 3 * LANES), lambda bi, g: (bi, 0)),
            const((1, Q_LORA_RANK)), const((1, KV_LORA_RANK)),
            const((1, HEAD_PAD)), const((1, HEAD_PAD)),
            head_w(Q_LORA_RANK, lambda g: 0), head_w(KV_LORA_RANK, lambda g: 0),
        ] + following_specs,
        out_specs=pl.BlockSpec((s, hps * V_HEAD_DIM), lambda bi, g: (bi, g)),
        out_shape=jax.ShapeDtypeStruct((b * s, N_HEADS * V_HEAD_DIM), BF16),
        scratch_shapes=[
            rows(Q_LORA_RANK, BF16), rows(KV_LORA_RANK, BF16),
            rows(LANES, F32), rows(1, F32),
            rows(HEAD_PAD, BF16), rows(HEAD_PAD, BF16), rows(HEAD_PAD, BF16),
            rows(HEAD_PAD, BF16), rows(HEAD_PAD, BF16), rows(HEAD_PAD, BF16),
        ],
        compiler_params=pltpu.CompilerParams(
            dimension_semantics=("arbitrary", "arbitrary"), vmem_limit_bytes=VMEM_LIMIT),
        name="mla_attention",
    )(z, rot, q_lat_g, kv_lat_g, qg_p, kg_p, *([wuq_p, wukv] * (hps + 1)))


def _mix_kernel(attn_ref, ga_ref, conv_ref, mrg_ref, prev_ref, next_ref,
                cw_ref, cbias_ref, wba_ref, wbc_ref, o_ref, u_ref, *, tm, seq):
    i = pl.program_id(0)
    seg = lambda ref, c: ref[:, c * SEG:(c + 1) * SEG]

    a = attn_ref[...] * ga_ref[...]
    halves = tuple(slice(c * SEG, (c + 1) * SEG) for c in range(2))
    y_attn = [jnp.dot(a, wba_ref[:, cols], preferred_element_type=F32) for cols in halves]

    u = seg(conv_ref, 1).astype(F32) * seg(conv_ref, 2).astype(F32)
    first = (i * tm) % seq == 0
    last = ((i + 1) * tm) % seq == 0
    prev = (prev_ref[HALO_ROWS - 1:HALO_ROWS, 0:SEG].astype(F32)
            * prev_ref[HALO_ROWS - 1:HALO_ROWS, SEG:2 * SEG].astype(F32))
    nxt = next_ref[0:1, 0:SEG].astype(F32) * next_ref[0:1, SEG:2 * SEG].astype(F32)
    u_ref[7:8, :] = jnp.where(first, 0.0, prev)
    u_ref[8:8 + tm, :] = u
    u_ref[8 + tm:9 + tm, :] = jnp.where(last, 0.0, nxt)
    conv = (u_ref[7:7 + tm, :] * cw_ref[0:1, :] + u * cw_ref[1:2, :]
            + u_ref[9:9 + tm, :] * cw_ref[2:3, :] + cbias_ref[...])
    cv = (seg(conv_ref, 0).astype(F32) * conv * seg(conv_ref, 3).astype(F32)).astype(BF16)

    for c in range(2):
        y_conv = jnp.dot(cv, wbc_ref[:, halves[c]], preferred_element_type=F32)
        o_ref[:, halves[c]] = (seg(mrg_ref, c).astype(F32) * y_attn[c]
                               + seg(mrg_ref, 2 + c).astype(F32) * y_conv).astype(BF16)


def _mix(attn, z, conv_w, conv_b, wba, wbc, seq, tm):
    t, d = attn.shape[0], wba.shape[1]

    def window(rows, first_seg, n_seg, row_start):
        return pl.BlockSpec(
            (pl.Element(rows), pl.Element(n_seg * SEG)),
            lambda i: (pl.multiple_of(row_start(i), HALO_ROWS), first_seg * SEG))

    tile = lambda i: i * tm
    prev_rows = lambda i: jnp.maximum(i * tm - HALO_ROWS, 0)
    next_rows = lambda i: jnp.minimum((i + 1) * tm, t - HALO_ROWS)
    resident = lambda shape: pl.BlockSpec(shape, lambda i: (0,) * len(shape),
                                          pipeline_mode=pl.Buffered(1))
    return pl.pallas_call(
        functools.partial(_mix_kernel, tm=tm, seq=seq),
        grid=(t // tm,),
        in_specs=[
            pl.BlockSpec((tm, attn.shape[1]), lambda i: (i, 0)),
            window(tm, 1, 2, tile), window(tm, 3, 4, tile), window(tm, 7, 4, tile),
            window(HALO_ROWS, 4, 2, prev_rows), window(HALO_ROWS, 4, 2, next_rows),
            resident(conv_w.shape), resident(conv_b.shape),
            resident(wba.shape), resident(wbc.shape),
        ],
        out_specs=pl.BlockSpec((tm, d), lambda i: (i, 0)),
        out_shape=jax.ShapeDtypeStruct((t, d), BF16),
        scratch_shapes=[pltpu.VMEM((tm + 16, SEG), F32)],
        compiler_params=pltpu.CompilerParams(
            dimension_semantics=("arbitrary",), vmem_limit_bytes=VMEM_LIMIT),
        name="mix",
    )(attn, z, z, z, z, z, conv_w, conv_b, wba, wbc)


def _out_ple_kernel(x_ref, m_ref, p_ref, g_ref, wout_ref, wg_ref, wp_ref, o_ref, x1_ref):
    x1_ref[...] = x_ref[...] + jnp.dot(m_ref[...], wout_ref[...], preferred_element_type=F32)
    pb = p_ref[...].astype(BF16)
    x1 = x1_ref[...]
    h = (_rms(x1, x1.shape[-1]) * g_ref[...]).astype(BF16)
    for c in range(x1.shape[-1] // SEG):
        cols = slice(c * SEG, (c + 1) * SEG)
        proj = jnp.dot(pb, wp_ref[:, cols], preferred_element_type=F32)
        gate = _sigmoid(jnp.dot(h, wg_ref[:, cols], preferred_element_type=F32))
        o_ref[:, cols] = x1_ref[:, cols] + gate * proj


def _out_ple(x2, merged, p2, ple_g, wout, wg, wp, tm):
    t, d = x2.shape
    resident = lambda shape: pl.BlockSpec(shape, lambda i: (0,) * len(shape),
                                          pipeline_mode=pl.Buffered(1))
    return pl.pallas_call(
        _out_ple_kernel,
        grid=(t // tm,),
        in_specs=[
            pl.BlockSpec((tm, d), lambda i: (i, 0)),
            pl.BlockSpec((tm, d), lambda i: (i, 0)),
            pl.BlockSpec((tm, p2.shape[1]), lambda i: (i, 0)),
            resident((1, d)), resident(wout.shape), resident(wg.shape), resident(wp.shape),
        ],
        out_specs=pl.BlockSpec((tm, d), lambda i: (i, 0)),
        out_shape=jax.ShapeDtypeStruct((t, d), F32),
        scratch_shapes=[pltpu.VMEM((tm, d), F32)],
        compiler_params=pltpu.CompilerParams(
            dimension_semantics=("arbitrary",), vmem_limit_bytes=VMEM_LIMIT),
        name="out_ple",
    )(x2, merged, p2, ple_g, wout, wg, wp)


def _layer(x2, p2, pos, freq, b, s, norm_g, w_in, q_lat_g, kv_lat_g, w_uq, w_ukv,
           q_norm_g, k_norm_g, conv_w, conv_b, w_branch_attn, w_branch_conv,
           w_out, ple_norm_g, w_ple_gate, w_ple_proj):
    d = x2.shape[1]
    wuq_p = jnp.pad(w_uq.reshape(Q_LORA_RANK, N_HEADS, QK_HEAD_DIM),
                    ((0, 0), (0, 0), (0, HEAD_PAD - QK_HEAD_DIM))
                    ).reshape(Q_LORA_RANK, N_HEADS * HEAD_PAD).astype(BF16)
    pad_g = lambda g: jnp.pad(g, (0, HEAD_PAD - QK_HEAD_DIM)).reshape(1, HEAD_PAD)

    z, rot = _in_proj(x2, norm_g.reshape(1, d), jnp.swapaxes(w_in, 0, 1), pos, freq,
                      tm=IN_PROJ_TM)
    attn = _mla_attention(z, rot, q_lat_g.reshape(1, -1), kv_lat_g.reshape(1, -1),
                          wuq_p, w_ukv.astype(BF16), pad_g(q_norm_g), pad_g(k_norm_g),
                          b, s, tq=512, tp=512, hps=2)
    merged = _mix(attn, z, conv_w, conv_b.reshape(1, -1), w_branch_attn.astype(BF16),
                  w_branch_conv.astype(BF16), seq=s, tm=512)
    return _out_ple(x2, merged, p2, ple_norm_g.reshape(1, d), w_out.astype(BF16),
                    w_ple_gate.astype(BF16), w_ple_proj.astype(BF16), tm=512)


def kernel(x, p, positions, norm_g, w_in, q_lat_g, kv_lat_g, w_uq, w_ukv, q_norm_g, k_norm_g, conv_w, conv_b, w_branch_attn, w_branch_conv, w_out, ple_norm_g, w_ple_gate, w_ple_proj):
    b, s, d = x.shape
    depth = p.shape[0]
    x2 = x.reshape(b * s, d)
    half = QK_ROPE_DIM // 2
    pos = positions.reshape(b * s // IN_PROJ_TM, ROT_GROUPS, IN_PROJ_TM // ROT_GROUPS)
    pos = jnp.repeat(jnp.swapaxes(pos, 1, 2), half, axis=2).reshape(b * s // ROT_GROUPS, LANES)
    inv_freq = 1.0 / (ROPE_THETA ** (jnp.arange(0, QK_ROPE_DIM, 2, dtype=F32) / QK_ROPE_DIM))
    freq = jnp.tile(inv_freq, ROT_GROUPS).reshape(1, LANES)
    for i in range(depth):
        x2 = _layer(x2, p[i].reshape(b * s, -1), pos, freq, b, s, norm_g[i], w_in[i],
                    q_lat_g[i], kv_lat_g[i], w_uq[i], w_ukv[i], q_norm_g[i], k_norm_g[i],
                    conv_w[i], conv_b[i], w_branch_attn[i], w_branch_conv[i], w_out[i],
                    ple_norm_g[i], w_ple_gate[i], w_ple_proj[i])
    return x2.reshape(b, s, d)
```

```python
import functools

import jax
import jax.numpy as jnp
from jax import lax
from jax.experimental import pallas as pl
from jax.experimental.pallas import tpu as pltpu

N_HEADS = 16
QK_NOPE_DIM = 128
QK_ROPE_DIM = 64
QK_HEAD_DIM = QK_NOPE_DIM + QK_ROPE_DIM
V_HEAD_DIM = 128
Q_LORA_RANK = 512
KV_LORA_RANK = 256
ROPE_THETA = 10000.0
EPS = 1e-6
LOG2E = 1.4426950408889634

LANES = 128
HEAD_PAD = 2 * LANES
SEG = 1024
HEAD_COLS = Q_LORA_RANK + KV_LORA_RANK + QK_ROPE_DIM
HEAD_COLS_ALIGN = 64
assert HEAD_COLS % HEAD_COLS_ALIGN == 0 and SEG % HEAD_COLS_ALIGN == 0
HALO_ROWS = 16
VMEM_LIMIT = 56 * 1024 * 1024

BF16 = jnp.bfloat16
F32 = jnp.float32


def _rms(v, axis_size):
    return v * lax.rsqrt(jnp.sum(v * v, axis=-1, keepdims=True) * (1.0 / axis_size) + EPS)


def _sigmoid(v):
    return 0.5 * jnp.tanh(0.5 * v) + 0.5


def _silu(v):
    return v * _sigmoid(v)


def _identity(v):
    return v


Z_TILE_ACTS = (_identity, _silu, _silu, _identity, _identity, _identity, _silu,
               _sigmoid, _sigmoid, _sigmoid, _sigmoid)
ROT_TILE = 3
ROT_GROUPS = LANES // (QK_ROPE_DIM // 2)
IN_PROJ_TM = 1024


def _in_proj_kernel(x_ref, g_ref, wt_ref, pos_ref, freq_ref, z_ref, rot_ref, h_ref):
    j = pl.program_id(1)

    @pl.when(j == 0)
    def _():
        x = x_ref[...]
        h_ref[...] = (_rms(x, x.shape[-1]) * g_ref[...]).astype(BF16)

    def rotary_tables():
        half = QK_ROPE_DIM // 2
        rq = x_ref.shape[0] // ROT_GROUPS
        lane = lax.broadcasted_iota(jnp.int32, (rq, LANES), 1)
        ang = pos_ref[...].astype(F32) * freq_ref[...]
        cosv, sinv = jnp.cos(ang), jnp.sin(ang)
        for q in range(ROT_GROUPS):
            rows = slice(q * rq, (q + 1) * rq)
            to_x1 = (lambda v: v) if q == 0 else (
                lambda v: pltpu.roll(v, LANES - q * half, axis=1))
            to_x2 = lambda v: pltpu.roll(v, (LANES - q * half + half) % LANES, axis=1)
            rot_ref[rows, 0:LANES] = jnp.where(lane < half, to_x1(cosv), to_x2(cosv))
            rot_ref[rows, LANES:2 * LANES] = jnp.where(lane < half, -to_x1(sinv), 0.0)
            rot_ref[rows, 2 * LANES:3 * LANES] = jnp.where(
                (lane >= half) & (lane < QK_ROPE_DIM), to_x2(sinv), 0.0)

    def is_act(*acts):
        return functools.reduce(jnp.logical_or,
                                [j == tile for tile, a in enumerate(Z_TILE_ACTS) if a in acts])

    def matmul():
        return lax.dot_general(h_ref[...], wt_ref[...].astype(BF16), (((1,), (1,)), ((), ())),
                               preferred_element_type=F32)

    @pl.when(is_act(_identity))
    def _():
        z_ref[...] = matmul().astype(BF16)

    @pl.when(is_act(_silu, _sigmoid))
    def _():
        hv = 0.5 * matmul()
        a = jnp.where(is_act(_silu), hv, 0.5)
        z_ref[...] = (a * jnp.tanh(hv) + a).astype(BF16)

    @pl.when(j == ROT_TILE)
    def _():
        rotary_tables()


def _in_proj(x2, norm_g, w_t, pos, freq, tm):
    t, d = x2.shape
    n_tiles = len(Z_TILE_ACTS)
    assert w_t.shape == (HEAD_COLS + (n_tiles - 1) * SEG, d)
    return pl.pallas_call(
        _in_proj_kernel,
        grid=(t // tm, n_tiles),
        in_specs=[
            pl.BlockSpec((tm, d), lambda i, j: (i, 0)),
            pl.BlockSpec((1, d), lambda i, j: (0, 0)),
            pl.BlockSpec((pl.Element(SEG), pl.Element(d)),
                         lambda i, j: (pl.multiple_of(
                             jnp.maximum(j * SEG - (SEG - HEAD_COLS), 0), HEAD_COLS_ALIGN), 0)),
            pl.BlockSpec((tm // ROT_GROUPS, LANES), lambda i, j: (i, 0)),
            pl.BlockSpec((1, LANES), lambda i, j: (0, 0)),
        ],
        out_specs=[pl.BlockSpec((tm, SEG), lambda i, j: (i, j)),
                   pl.BlockSpec((tm, 3 * LANES), lambda i, j: (i, 0))],
        out_shape=[jax.ShapeDtypeStruct((t, n_tiles * SEG), BF16),
                   jax.ShapeDtypeStruct((t, 3 * LANES), F32)],
        scratch_shapes=[pltpu.VMEM((tm, d), BF16)],
        compiler_params=pltpu.CompilerParams(
            dimension_semantics=("arbitrary", "arbitrary"),
            vmem_limit_bytes=VMEM_LIMIT),
        name="in_proj",
    )(x2, norm_g, w_t, pos, freq)


def _mla_kernel(*refs, tq, tp, hps):
    z_ref, rot_ref, qlg_ref, kvlg_ref, qg_ref, kg_ref, wuq0_ref, wukv0_ref = refs[:8]
    w_following = refs[8:8 + 2 * hps]
    o_ref = refs[8 + 2 * hps]
    (qn_ref, kvn_ref, krope_ref, kss_ref,
     qa_ref, ka_ref, va_ref, qb_ref, kb_ref, vb_ref) = refs[9 + 2 * hps:]
    s_len = z_ref.shape[0]
    n_chunks = s_len // tq
    slot_a = (qa_ref, ka_ref, va_ref)
    slot_b = (qb_ref, kb_ref, vb_ref)
    half = QK_ROPE_DIM // 2
    scale = QK_HEAD_DIM ** -0.5 * LOG2E
    inv_d = 1.0 / QK_HEAD_DIM
    qg_nope, qg_rope = qg_ref[:, 0:LANES], qg_ref[:, LANES:HEAD_PAD]
    kg_nope, kg_rope = kg_ref[:, 0:LANES], kg_ref[:, LANES:HEAD_PAD]

    def rope(u, rows):
        return (u * rot_ref[rows, 0:LANES]
                + pltpu.roll(u, LANES - half, axis=1) * rot_ref[rows, LANES:2 * LANES]
                + pltpu.roll(u, half, axis=1) * rot_ref[rows, 2 * LANES:3 * LANES])

    def head_weights(wuq_pair_ref, wukv_ref, odd):
        lane = lax.broadcasted_iota(jnp.int32, (Q_LORA_RANK, LANES), 1)
        if odd:
            mid = pltpu.roll(wuq_pair_ref[:, LANES:2 * LANES], QK_ROPE_DIM, axis=1)
            end = pltpu.roll(wuq_pair_ref[:, 2 * LANES:3 * LANES], QK_ROPE_DIM, axis=1)
            nope = jnp.where(lane < QK_ROPE_DIM, mid, end)
            rope = jnp.where(lane < QK_ROPE_DIM, end, 0.0)
        else:
            nope = wuq_pair_ref[:, 0:LANES]
            rope = jnp.where(lane < QK_ROPE_DIM, wuq_pair_ref[:, LANES:2 * LANES], 0.0)
        return (jnp.concatenate([nope, rope], axis=1).astype(BF16), wukv_ref[...].astype(BF16))

    def prep_rows(wuq, wukv, q_out, k_out, v_out, r):
        rows = slice(r * tq, (r + 1) * tq)
        q = jnp.dot(qn_ref[rows, :], wuq, preferred_element_type=F32)
        kv = jnp.dot(kvn_ref[rows, :], wukv, preferred_element_type=F32)
        q_nope, q_rope = q[:, 0:LANES], q[:, LANES:HEAD_PAD]
        ss = jnp.sum(q_nope * q_nope + q_rope * q_rope, axis=-1, keepdims=True)
        rq = lax.rsqrt(ss * inv_d + EPS) * scale
        q_out[rows, 0:LANES] = (q_nope * rq * qg_nope).astype(BF16)
        q_out[rows, LANES:HEAD_PAD] = (rope(q_rope * qg_rope, rows) * rq).astype(BF16)
        k_nope = kv[:, 0:LANES]
        ssk = jnp.sum(k_nope * k_nope, axis=-1, keepdims=True) + kss_ref[rows, :]
        rk = lax.rsqrt(ssk * inv_d + EPS)
        k_out[rows, 0:LANES] = (k_nope * rk * kg_nope).astype(BF16)
        k_out[rows, LANES:HEAD_PAD] = (krope_ref[rows, :] * rk).astype(BF16)
        v_out[rows, 0:LANES] = kv[:, LANES:HEAD_PAD].astype(BF16)

    def scores(slot, i):
        q_in, k_in, _ = slot
        return lax.dot_general(q_in[i * tq:(i + 1) * tq, :], k_in[...], (((1,), (1,)), ((), ())),
                               preferred_element_type=F32)

    def finish(sc, slot, col, i):
        m = jnp.max(sc, axis=-1, keepdims=True)
        p = jnp.exp2(sc - m).astype(BF16)
        o = jnp.dot(p, slot[2][...], preferred_element_type=F32)
        o_ref[i * tq:(i + 1) * tq, col * LANES:(col + 1) * LANES] = (
            o[:, :V_HEAD_DIM] / o[:, V_HEAD_DIM:V_HEAD_DIM + 1]).astype(BF16)

    @pl.when(pl.program_id(1) == 0)
    def _():
        for r in range(s_len // tp):
            rows = slice(r * tp, (r + 1) * tp)
            qa = z_ref[rows, 0:Q_LORA_RANK].astype(F32)
            kva = z_ref[rows, Q_LORA_RANK:Q_LORA_RANK + KV_LORA_RANK].astype(F32)
            lane = lax.broadcasted_iota(jnp.int32, (tp, LANES), 1)
            kpe = jnp.where(lane < QK_ROPE_DIM,
                            z_ref[rows, HEAD_COLS - QK_ROPE_DIM:HEAD_COLS - QK_ROPE_DIM + LANES
                                  ].astype(F32), 0.0)
            qn_ref[rows, :] = (_rms(qa, Q_LORA_RANK) * qlg_ref[...]).astype(BF16)
            kvn_ref[rows, :] = (_rms(kva, KV_LORA_RANK) * kvlg_ref[...]).astype(BF16)
            krope_ref[rows, :] = rope(kpe * kg_rope, rows)
            kss_ref[rows, :] = jnp.sum(kpe * kpe, axis=-1, keepdims=True)
            ones_col = jnp.where(lane == 0, 1.0, 0.0).astype(BF16)
            va_ref[rows, LANES:HEAD_PAD] = ones_col
            vb_ref[rows, LANES:HEAD_PAD] = ones_col
        w_first = head_weights(wuq0_ref, wukv0_ref, odd=False)
        for r in range(n_chunks):
            prep_rows(*w_first, *slot_a, r)

    slots = (slot_a, slot_b)
    total = hps * n_chunks
    sc_next = scores(slot_a, 0)
    for t in range(total):
        hd, i = divmod(t, n_chunks)
        sc_cur = sc_next
        if i == 0:
            w_head = head_weights(*w_following[2 * hd:2 * hd + 2], odd=(hd + 1) % 2 == 1)
        prep_rows(*w_head, *slots[(hd + 1) % 2], i)
        if t + 1 < total:
            hd2, i2 = divmod(t + 1, n_chunks)
            sc_next = scores(slots[hd2 % 2], i2)
        finish(sc_cur, slots[hd % 2], hd, i)


def _mla_attention(z, rot, q_lat_g, kv_lat_g, w_uq, w_ukv, qg_p, kg_p, b, s, tq, tp, hps):
    assert hps % 2 == 0 and N_HEADS % hps == 0
    const = lambda shape: pl.BlockSpec(shape, lambda bi, g: (0,) * len(shape))
    wuq_pair = lambda f: pl.BlockSpec((Q_LORA_RANK, 2 * QK_HEAD_DIM), lambda bi, g: (0, f(g) // 2))
    wukv_head = lambda f: pl.BlockSpec((KV_LORA_RANK, HEAD_PAD), lambda bi, g: (0, f(g)))
    following = lambda k: (lambda g: jnp.minimum(hps * g + k, N_HEADS - 2 + k % 2))
    following_specs = []
    for k in range(1, hps + 1):
        following_specs += [wuq_pair(following(k)), wukv_head(following(k))]
    rows = lambda w, dt: pltpu.VMEM((s, w), dt)
    return pl.pallas_call(
        functools.partial(_mla_kernel, tq=tq, tp=tp, hps=hps),
        grid=(b, N_HEADS // hps),
        in_specs=[
            pl.BlockSpec((s, SEG), lambda bi, g: (bi, 0)),
            pl.BlockSpec((s, 3 * LANES), lambda bi, g: (bi, 0)),
            const((1, Q_LORA_RANK)), const((1, KV_LORA_RANK)),
            const((1, HEAD_PAD)), const((1, HEAD_PAD)),
            wuq_pair(lambda g: 0), wukv_head(lambda g: 0),
        ] + following_specs,
        out_specs=pl.BlockSpec((s, hps * V_HEAD_DIM), lambda bi, g: (bi, g)),
        out_shape=jax.ShapeDtypeStruct((b * s, N_HEADS * V_HEAD_DIM), BF16),
        scratch_shapes=[
            rows(Q_LORA_RANK, BF16), rows(KV_LORA_RANK, BF16),
            rows(LANES, F32), rows(1, F32),
            rows(HEAD_PAD, BF16), rows(HEAD_PAD, BF16), rows(HEAD_PAD, BF16),
            rows(HEAD_PAD, BF16), rows(HEAD_PAD, BF16), rows(HEAD_PAD, BF16),
        ],
        compiler_params=pltpu.CompilerParams(
            dimension_semantics=("arbitrary", "arbitrary"), vmem_limit_bytes=VMEM_LIMIT),
        name="mla_attention",
    )(z, rot, q_lat_g, kv_lat_g, qg_p, kg_p, *([w_uq, w_ukv] * (hps + 1)))


def _mix_kernel(attn_ref, ga_ref, conv_ref, mrg_ref, prev_ref, next_ref,
                cw_ref, cbias_ref, wba_ref, wbc_ref, o_ref, u_ref, *, tm, seq):
    i = pl.program_id(0)
    seg = lambda ref, c: ref[:, c * SEG:(c + 1) * SEG]

    a = attn_ref[...] * ga_ref[...]
    halves = tuple(slice(c * SEG, (c + 1) * SEG) for c in range(2))
    y_attn = [jnp.dot(a, wba_ref[:, cols], preferred_element_type=F32) for cols in halves]

    u = seg(conv_ref, 1).astype(F32) * seg(conv_ref, 2).astype(F32)
    first = (i * tm) % seq == 0
    last = ((i + 1) * tm) % seq == 0
    prev = (prev_ref[HALO_ROWS - 1:HALO_ROWS, 0:SEG].astype(F32)
            * prev_ref[HALO_ROWS - 1:HALO_ROWS, SEG:2 * SEG].astype(F32))
    nxt = next_ref[0:1, 0:SEG].astype(F32) * next_ref[0:1, SEG:2 * SEG].astype(F32)
    u_ref[7:8, :] = jnp.where(first, 0.0, prev)
    u_ref[8:8 + tm, :] = u
    u_ref[8 + tm:9 + tm, :] = jnp.where(last, 0.0, nxt)
    conv = (u_ref[7:7 + tm, :] * cw_ref[0:1, :] + u * cw_ref[1:2, :]
            + u_ref[9:9 + tm, :] * cw_ref[2:3, :] + cbias_ref[...])
    cv = (seg(conv_ref, 0).astype(F32) * conv * seg(conv_ref, 3).astype(F32)).astype(BF16)

    for c in range(2):
        y_conv = jnp.dot(cv, wbc_ref[:, halves[c]], preferred_element_type=F32)
        o_ref[:, halves[c]] = (seg(mrg_ref, c).astype(F32) * y_attn[c]
                               + seg(mrg_ref, 2 + c).astype(F32) * y_conv).astype(BF16)


def _mix(attn, z, conv_w, conv_b, wba, wbc, seq, tm):
    t, d = attn.shape[0], wba.shape[1]

    def window(rows, first_seg, n_seg, row_start):
        return pl.BlockSpec(
            (pl.Element(rows), pl.Element(n_seg * SEG)),
            lambda i: (pl.multiple_of(row_start(i), HALO_ROWS), first_seg * SEG))

    tile = lambda i: i * tm
    prev_rows = lambda i: jnp.maximum(i * tm - HALO_ROWS, 0)
    next_rows = lambda i: jnp.minimum((i + 1) * tm, t - HALO_ROWS)
    resident = lambda shape: pl.BlockSpec(shape, lambda i: (0,) * len(shape),
                                          pipeline_mode=pl.Buffered(1))
    return pl.pallas_call(
        functools.partial(_mix_kernel, tm=tm, seq=seq),
        grid=(t // tm,),
        in_specs=[
            pl.BlockSpec((tm, attn.shape[1]), lambda i: (i, 0)),
            window(tm, 1, 2, tile), window(tm, 3, 4, tile), window(tm, 7, 4, tile),
            window(HALO_ROWS, 4, 2, prev_rows), window(HALO_ROWS, 4, 2, next_rows),
            resident(conv_w.shape), resident(conv_b.shape),
            resident(wba.shape), resident(wbc.shape),
        ],
        out_specs=pl.BlockSpec((tm, d), lambda i: (i, 0)),
        out_shape=jax.ShapeDtypeStruct((t, d), BF16),
        scratch_shapes=[pltpu.VMEM((tm + 16, SEG), F32)],
        compiler_params=pltpu.CompilerParams(
            dimension_semantics=("arbitrary",), vmem_limit_bytes=VMEM_LIMIT),
        name="mix",
    )(attn, z, z, z, z, z, conv_w, conv_b, wba, wbc)


def _out_ple_kernel(x_ref, m_ref, p_ref, g_ref, wout_ref, wg_ref, wp_ref, o_ref, x1_ref):
    x1_ref[...] = x_ref[...] + jnp.dot(m_ref[...], wout_ref[...], preferred_element_type=F32)
    pb = p_ref[...].astype(BF16)
    x1 = x1_ref[...]
    h = (_rms(x1, x1.shape[-1]) * g_ref[...]).astype(BF16)
    for c in range(x1.shape[-1] // SEG):
        cols = slice(c * SEG, (c + 1) * SEG)
        proj = jnp.dot(pb, wp_ref[:, cols], preferred_element_type=F32)
        gate = _sigmoid(jnp.dot(h, wg_ref[:, cols], preferred_element_type=F32))
        o_ref[:, cols] = x1_ref[:, cols] + gate * proj


def _out_ple(x2, merged, p2, ple_g, wout, wg, wp, tm):
    t, d = x2.shape
    resident = lambda shape: pl.BlockSpec(shape, lambda i: (0,) * len(shape),
                                          pipeline_mode=pl.Buffered(1))
    return pl.pallas_call(
        _out_ple_kernel,
        grid=(t // tm,),
        in_specs=[
            pl.BlockSpec((tm, d), lambda i: (i, 0)),
            pl.BlockSpec((tm, d), lambda i: (i, 0)),
            pl.BlockSpec((tm, p2.shape[1]), lambda i: (i, 0)),
            resident((1, d)), resident(wout.shape), resident(wg.shape), resident(wp.shape),
        ],
        out_specs=pl.BlockSpec((tm, d), lambda i: (i, 0)),
        out_shape=jax.ShapeDtypeStruct((t, d), F32),
        scratch_shapes=[pltpu.VMEM((tm, d), F32)],
        compiler_params=pltpu.CompilerParams(
            dimension_semantics=("arbitrary",), vmem_limit_bytes=VMEM_LIMIT),
        name="out_ple",
    )(x2, merged, p2, ple_g, wout, wg, wp)


def _layer(x2, p2, pos, freq, b, s, norm_g, w_in, q_lat_g, kv_lat_g, w_uq, w_ukv,
           q_norm_g, k_norm_g, conv_w, conv_b, w_branch_attn, w_branch_conv,
           w_out, ple_norm_g, w_ple_gate, w_ple_proj):
    d = x2.shape[1]
    pad_g = lambda g: jnp.pad(g, (0, HEAD_PAD - QK_HEAD_DIM)).reshape(1, HEAD_PAD)

    z, rot = _in_proj(x2, norm_g.reshape(1, d), jnp.swapaxes(w_in, 0, 1), pos, freq,
                      tm=IN_PROJ_TM)
    attn = _mla_attention(z, rot, q_lat_g.reshape(1, -1), kv_lat_g.reshape(1, -1),
                          w_uq, w_ukv, pad_g(q_norm_g), pad_g(k_norm_g),
                          b, s, tq=512, tp=512, hps=2)
    merged = _mix(attn, z, conv_w, conv_b.reshape(1, -1), w_branch_attn.astype(BF16),
                  w_branch_conv.astype(BF16), seq=s, tm=512)
    return _out_ple(x2, merged, p2, ple_norm_g.reshape(1, d), w_out.astype(BF16),
                    w_ple_gate.astype(BF16), w_ple_proj.astype(BF16), tm=512)


def kernel(x, p, positions, norm_g, w_in, q_lat_g, kv_lat_g, w_uq, w_ukv, q_norm_g, k_norm_g, conv_w, conv_b, w_branch_attn, w_branch_conv, w_out, ple_norm_g, w_ple_gate, w_ple_proj):
    b, s, d = x.shape
    depth = p.shape[0]
    x2 = x.reshape(b * s, d)
    half = QK_ROPE_DIM // 2
    pos = positions.reshape(b * s // IN_PROJ_TM, ROT_GROUPS, IN_PROJ_TM // ROT_GROUPS)
    pos = jnp.repeat(jnp.swapaxes(pos, 1, 2), half, axis=2).reshape(b * s // ROT_GROUPS, LANES)
    inv_freq = 1.0 / (ROPE_THETA ** (jnp.arange(0, QK_ROPE_DIM, 2, dtype=F32) / QK_ROPE_DIM))
    freq = jnp.tile(inv_freq, ROT_GROUPS).reshape(1, LANES)
    for i in range(depth):
        x2 = _layer(x2, p[i].reshape(b * s, -1), pos, freq, b, s, norm_g[i], w_in[i],
                    q_lat_g[i], kv_lat_g[i], w_uq[i], w_ukv[i], q_norm_g[i], k_norm_g[i],
                    conv_w[i], conv_b[i], w_branch_attn[i], w_branch_conv[i], w_out[i],
                    ple_norm_g[i], w_ple_gate[i], w_ple_proj[i])
    return x2.reshape(b, s, d)
```

```python
import functools

import jax
import jax.numpy as jnp
from jax import lax
from jax.experimental import pallas as pl
from jax.experimental.pallas import tpu as pltpu

N_HEADS = 16
QK_NOPE_DIM = 128
QK_ROPE_DIM = 64
QK_HEAD_DIM = QK_NOPE_DIM + QK_ROPE_DIM
V_HEAD_DIM = 128
Q_LORA_RANK = 512
KV_LORA_RANK = 256
ROPE_THETA = 10000.0
EPS = 1e-6
LOG2E = 1.4426950408889634

LANES = 128
HEAD_PAD = 2 * LANES
SEG = 1024
HEAD_COLS = Q_LORA_RANK + KV_LORA_RANK + QK_ROPE_DIM
HEAD_COLS_ALIGN = 64
assert HEAD_COLS % HEAD_COLS_ALIGN == 0 and SEG % HEAD_COLS_ALIGN == 0
HALO_ROWS = 16
VMEM_LIMIT = 56 * 1024 * 1024

BF16 = jnp.bfloat16
F32 = jnp.float32


def _rms(v, axis_size):
    return v * lax.rsqrt(jnp.sum(v * v, axis=-1, keepdims=True) * (1.0 / axis_size) + EPS)


def _sigmoid(v):
    return 0.5 * jnp.tanh(0.5 * v) + 0.5


def _silu(v):
    return v * _sigmoid(v)


def _identity(v):
    return v


Z_TILE_ACTS = (_identity, _silu, _silu, _identity, _identity, _identity, _silu,
               _sigmoid, _sigmoid, _sigmoid, _sigmoid)
ROT_TILE = 3
ROT_GROUPS = LANES // (QK_ROPE_DIM // 2)
IN_PROJ_TM = 1024


def _in_proj_kernel(x_ref, g_ref, wt_ref, pos_ref, freq_ref, z_ref, rot_ref, h_ref):
    j = pl.program_id(1)

    def rotary_tables():
        half = QK_ROPE_DIM // 2
        rq = x_ref.shape[0] // ROT_GROUPS
        lane = lax.broadcasted_iota(jnp.int32, (rq, LANES), 1)
        ang = pos_ref[...].astype(F32) * freq_ref[...]
        cosv, sinv = jnp.cos(ang), jnp.sin(ang)
        for q in range(ROT_GROUPS):
            rows = slice(q * rq, (q + 1) * rq)
            to_x1 = (lambda v: v) if q == 0 else (
                lambda v: pltpu.roll(v, LANES - q * half, axis=1))
            to_x2 = lambda v: pltpu.roll(v, (LANES - q * half + half) % LANES, axis=1)
            rot_ref[rows, 0:LANES] = jnp.where(lane < half, to_x1(cosv), to_x2(cosv))
            rot_ref[rows, LANES:2 * LANES] = jnp.where(lane < half, -to_x1(sinv), 0.0)
            rot_ref[rows, 2 * LANES:3 * LANES] = jnp.where(
                (lane >= half) & (lane < QK_ROPE_DIM), to_x2(sinv), 0.0)

    def is_act(*acts):
        return functools.reduce(jnp.logical_or,
                                [j == tile for tile, a in enumerate(Z_TILE_ACTS) if a in acts])

    def matmul(h=None):
        return lax.dot_general(h_ref[...] if h is None else h, wt_ref[...].astype(BF16),
                               (((1,), (1,)), ((), ())), preferred_element_type=F32)

    assert Z_TILE_ACTS[0] is _identity and Z_TILE_ACTS[ROT_TILE] is _identity and ROT_TILE != 0

    @pl.when(j == 0)
    def _():
        x = x_ref[...]
        h = (_rms(x, x.shape[-1]) * g_ref[...]).astype(BF16)
        h_ref[...] = h
        z_ref[...] = matmul(h).astype(BF16)

    @pl.when(j == ROT_TILE)
    def _():
        z_ref[...] = matmul().astype(BF16)
        rotary_tables()

    @pl.when(is_act(_identity) & (j != 0) & (j != ROT_TILE))
    def _():
        z_ref[...] = matmul().astype(BF16)

    @pl.when(is_act(_silu, _sigmoid))
    def _():
        hv = 0.5 * matmul()
        a = jnp.where(is_act(_silu), hv, 0.5)
        z_ref[...] = (a * jnp.tanh(hv) + a).astype(BF16)


def _in_proj(x2, norm_g, w_t, pos, freq, tm):
    t, d = x2.shape
    n_tiles = len(Z_TILE_ACTS)
    assert w_t.shape == (HEAD_COLS + (n_tiles - 1) * SEG, d)
    return pl.pallas_call(
        _in_proj_kernel,
        grid=(t // tm, n_tiles),
        in_specs=[
            pl.BlockSpec((tm, d), lambda i, j: (i, 0)),
            pl.BlockSpec((1, d), lambda i, j: (0, 0)),
            pl.BlockSpec((pl.Element(SEG), pl.Element(d)),
                         lambda i, j: (pl.multiple_of(
                             jnp.maximum(j * SEG - (SEG - HEAD_COLS), 0), HEAD_COLS_ALIGN), 0)),
            pl.BlockSpec((tm // ROT_GROUPS, LANES), lambda i, j: (i, 0)),
            pl.BlockSpec((1, LANES), lambda i, j: (0, 0)),
        ],
        out_specs=[pl.BlockSpec((tm, SEG), lambda i, j: (i, j)),
                   pl.BlockSpec((tm, 3 * LANES), lambda i, j: (i, 0))],
        out_shape=[jax.ShapeDtypeStruct((t, n_tiles * SEG), BF16),
                   jax.ShapeDtypeStruct((t, 3 * LANES), F32)],
        scratch_shapes=[pltpu.VMEM((tm, d), BF16)],
        compiler_params=pltpu.CompilerParams(
            dimension_semantics=("arbitrary", "arbitrary"),
            vmem_limit_bytes=VMEM_LIMIT),
        name="in_proj",
    )(x2, norm_g, w_t, pos, freq)


def _mla_kernel(*refs, tq, tp, hps):
    z_ref, rot_ref, qlg_ref, kvlg_ref, qg_ref, kg_ref, wuq0_ref, wukv0_ref = refs[:8]
    w_following = refs[8:8 + 2 * hps]
    o_ref = refs[8 + 2 * hps]
    (qn_ref, kvn_ref, krope_ref, kss_ref,
     qa_ref, ka_ref, va_ref, qb_ref, kb_ref, vb_ref) = refs[9 + 2 * hps:]
    s_len = z_ref.shape[0]
    n_chunks = s_len // tq
    slot_a = (qa_ref, ka_ref, va_ref)
    slot_b = (qb_ref, kb_ref, vb_ref)
    half = QK_ROPE_DIM // 2
    scale = QK_HEAD_DIM ** -0.5 * LOG2E
    inv_d = 1.0 / QK_HEAD_DIM
    qg_nope, qg_rope = qg_ref[:, 0:LANES], qg_ref[:, LANES:HEAD_PAD]
    kg_nope, kg_rope = kg_ref[:, 0:LANES], kg_ref[:, LANES:HEAD_PAD]

    def rope(u, rows):
        return (u * rot_ref[rows, 0:LANES]
                + pltpu.roll(u, LANES - half, axis=1) * rot_ref[rows, LANES:2 * LANES]
                + pltpu.roll(u, half, axis=1) * rot_ref[rows, 2 * LANES:3 * LANES])

    def head_weights(wuq_pair_ref, wukv_ref, odd):
        lane = lax.broadcasted_iota(jnp.int32, (Q_LORA_RANK, LANES), 1)
        if odd:
            mid = pltpu.roll(wuq_pair_ref[:, LANES:2 * LANES], QK_ROPE_DIM, axis=1)
            end = pltpu.roll(wuq_pair_ref[:, 2 * LANES:3 * LANES], QK_ROPE_DIM, axis=1)
            nope = jnp.where(lane < QK_ROPE_DIM, mid, end)
            rope = jnp.where(lane < QK_ROPE_DIM, end, 0.0)
        else:
            nope = wuq_pair_ref[:, 0:LANES]
            rope = jnp.where(lane < QK_ROPE_DIM, wuq_pair_ref[:, LANES:2 * LANES], 0.0)
        return (jnp.concatenate([nope, rope], axis=1).astype(BF16), wukv_ref[...].astype(BF16))

    def prep_rows(wuq, wukv, q_out, k_out, v_out, r):
        rows = slice(r * tq, (r + 1) * tq)
        q = jnp.dot(qn_ref[rows, :], wuq, preferred_element_type=F32)
        kv = jnp.dot(kvn_ref[rows, :], wukv, preferred_element_type=F32)
        q_nope, q_rope = q[:, 0:LANES], q[:, LANES:HEAD_PAD]
        ss = jnp.sum(q_nope * q_nope + q_rope * q_rope, axis=-1, keepdims=True)
        rq = lax.rsqrt(ss * inv_d + EPS) * scale
        q_out[rows, 0:LANES] = (q_nope * rq * qg_nope).astype(BF16)
        q_out[rows, LANES:HEAD_PAD] = (rope(q_rope * qg_rope, rows) * rq).astype(BF16)
        k_nope = kv[:, 0:LANES]
        ssk = jnp.sum(k_nope * k_nope, axis=-1, keepdims=True) + kss_ref[rows, :]
        rk = lax.rsqrt(ssk * inv_d + EPS)
        k_out[rows, 0:LANES] = (k_nope * rk * kg_nope).astype(BF16)
        k_out[rows, LANES:HEAD_PAD] = (krope_ref[rows, :] * rk).astype(BF16)
        v_out[rows, 0:LANES] = kv[:, LANES:HEAD_PAD].astype(BF16)

    def scores(slot, i):
        q_in, k_in, _ = slot
        return lax.dot_general(q_in[i * tq:(i + 1) * tq, :], k_in[...], (((1,), (1,)), ((), ())),
                               preferred_element_type=F32)

    def finish(sc, slot, col, i):
        m = jnp.max(sc, axis=-1, keepdims=True)
        p = jnp.exp2(sc - m).astype(BF16)
        o = jnp.dot(p, slot[2][...], preferred_element_type=F32)
        o_ref[i * tq:(i + 1) * tq, col * LANES:(col + 1) * LANES] = (
            o[:, :V_HEAD_DIM] / o[:, V_HEAD_DIM:V_HEAD_DIM + 1]).astype(BF16)

    @pl.when(pl.program_id(1) == 0)
    def _():
        for r in range(s_len // tp):
            rows = slice(r * tp, (r + 1) * tp)
            qa = z_ref[rows, 0:Q_LORA_RANK].astype(F32)
            kva = z_ref[rows, Q_LORA_RANK:Q_LORA_RANK + KV_LORA_RANK].astype(F32)
            lane = lax.broadcasted_iota(jnp.int32, (tp, LANES), 1)
            kpe = jnp.where(lane < QK_ROPE_DIM,
                            z_ref[rows, HEAD_COLS - QK_ROPE_DIM:HEAD_COLS - QK_ROPE_DIM + LANES
                                  ].astype(F32), 0.0)
            qn_ref[rows, :] = (_rms(qa, Q_LORA_RANK) * qlg_ref[...]).astype(BF16)
            kvn_ref[rows, :] = (_rms(kva, KV_LORA_RANK) * kvlg_ref[...]).astype(BF16)
            krope_ref[rows, :] = rope(kpe * kg_rope, rows)
            kss_ref[rows, :] = jnp.sum(kpe * kpe, axis=-1, keepdims=True)
            ones_col = jnp.where(lane == 0, 1.0, 0.0).astype(BF16)
            va_ref[rows, LANES:HEAD_PAD] = ones_col
            vb_ref[rows, LANES:HEAD_PAD] = ones_col
        w_first = head_weights(wuq0_ref, wukv0_ref, odd=False)
        for r in range(n_chunks):
            prep_rows(*w_first, *slot_a, r)

    slots = (slot_a, slot_b)
    total = hps * n_chunks
    sc_next = scores(slot_a, 0)
    for t in range(total):
        hd, i = divmod(t, n_chunks)
        sc_cur = sc_next
        if i == 0:
            w_head = head_weights(*w_following[2 * hd:2 * hd + 2], odd=(hd + 1) % 2 == 1)
        prep_rows(*w_head, *slots[(hd + 1) % 2], i)
        if t + 1 < total:
            hd2, i2 = divmod(t + 1, n_chunks)
            sc_next = scores(slots[hd2 % 2], i2)
        finish(sc_cur, slots[hd % 2], hd, i)


def _mla_attention(z, rot, q_lat_g, kv_lat_g, w_uq, w_ukv, qg_p, kg_p, b, s, tq, tp, hps):
    assert hps % 2 == 0 and N_HEADS % hps == 0
    const = lambda shape: pl.BlockSpec(shape, lambda bi, g: (0,) * len(shape))
    wuq_pair = lambda f: pl.BlockSpec((Q_LORA_RANK, 2 * QK_HEAD_DIM), lambda bi, g: (0, f(g) // 2))
    wukv_head = lambda f: pl.BlockSpec((KV_LORA_RANK, HEAD_PAD), lambda bi, g: (0, f(g)))
    following = lambda k: (lambda g: jnp.minimum(hps * g + k, N_HEADS - 2 + k % 2))
    following_specs = []
    for k in range(1, hps + 1):
        following_specs += [wuq_pair(following(k)), wukv_head(following(k))]
    rows = lambda w, dt: pltpu.VMEM((s, w), dt)
    return pl.pallas_call(
        functools.partial(_mla_kernel, tq=tq, tp=tp, hps=hps),
        grid=(b, N_HEADS // hps),
        in_specs=[
            pl.BlockSpec((s, SEG), lambda bi, g: (bi, 0)),
            pl.BlockSpec((s, 3 * LANES), lambda bi, g: (bi, 0)),
            const((1, Q_LORA_RANK)), const((1, KV_LORA_RANK)),
            const((1, HEAD_PAD)), const((1, HEAD_PAD)),
            wuq_pair(lambda g: 0), wukv_head(lambda g: 0),
        ] + following_specs,
        out_specs=pl.BlockSpec((s, hps * V_HEAD_DIM), lambda bi, g: (bi, g)),
        out_shape=jax.ShapeDtypeStruct((b * s, N_HEADS * V_HEAD_DIM), BF16),
        scratch_shapes=[
            rows(Q_LORA_RANK, BF16), rows(KV_LORA_RANK, BF16),
            rows(LANES, F32), rows(1, F32),
            rows(HEAD_PAD, BF16), rows(HEAD_PAD, BF16), rows(HEAD_PAD, BF16),
            rows(HEAD_PAD, BF16), rows(HEAD_PAD, BF16), rows(HEAD_PAD, BF16),
        ],
        compiler_params=pltpu.CompilerParams(
            dimension_semantics=("arbitrary", "arbitrary"), vmem_limit_bytes=VMEM_LIMIT),
        name="mla_attention",
    )(z, rot, q_lat_g, kv_lat_g, qg_p, kg_p, *([w_uq, w_ukv] * (hps + 1)))


def _mix_kernel(attn_ref, ga_ref, conv_ref, mrg_ref, prev_ref, next_ref,
                cw_ref, cbias_ref, wba_ref, wbc_ref, o_ref, u_ref, *, tm, seq):
    i = pl.program_id(0)
    seg = lambda ref, c: ref[:, c * SEG:(c + 1) * SEG]

    a = attn_ref[...] * ga_ref[...]
    halves = tuple(slice(c * SEG, (c + 1) * SEG) for c in range(2))
    y_attn = [jnp.dot(a, wba_ref[:, cols], preferred_element_type=F32) for cols in halves]

    u = seg(conv_ref, 1).astype(F32) * seg(conv_ref, 2).astype(F32)
    first = (i * tm) % seq == 0
    last = ((i + 1) * tm) % seq == 0
    prev = (prev_ref[HALO_ROWS - 1:HALO_ROWS, 0:SEG].astype(F32)
            * prev_ref[HALO_ROWS - 1:HALO_ROWS, SEG:2 * SEG].astype(F32))
    nxt = next_ref[0:1, 0:SEG].astype(F32) * next_ref[0:1, SEG:2 * SEG].astype(F32)
    u_ref[7:8, :] = jnp.where(first, 0.0, prev)
    u_ref[8:8 + tm, :] = u
    u_ref[8 + tm:9 + tm, :] = jnp.where(last, 0.0, nxt)
    conv = (u_ref[7:7 + tm, :] * cw_ref[0:1, :] + u * cw_ref[1:2, :]
            + u_ref[9:9 + tm, :] * cw_ref[2:3, :] + cbias_ref[...])
    cv = (seg(conv_ref, 0).astype(F32) * conv * seg(conv_ref, 3).astype(F32)).astype(BF16)

    for c in range(2):
        y_conv = jnp.dot(cv, wbc_ref[:, halves[c]], preferred_element_type=F32)
        o_ref[:, halves[c]] = (seg(mrg_ref, c).astype(F32) * y_attn[c]
                               + seg(mrg_ref, 2 + c).astype(F32) * y_conv).astype(BF16)


def _mix(attn, z, conv_w, conv_b, wba, wbc, seq, tm):
    t, d = attn.shape[0], wba.shape[1]

    def window(rows, first_seg, n_seg, row_start):
        return pl.BlockSpec(
            (pl.Element(rows), pl.Element(n_seg * SEG)),
            lambda i: (pl.multiple_of(row_start(i), HALO_ROWS), first_seg * SEG))

    tile = lambda i: i * tm
    prev_rows = lambda i: jnp.maximum(i * tm - HALO_ROWS, 0)
    next_rows = lambda i: jnp.minimum((i + 1) * tm, t - HALO_ROWS)
    resident = lambda shape: pl.BlockSpec(shape, lambda i: (0,) * len(shape),
                                          pipeline_mode=pl.Buffered(1))
    return pl.pallas_call(
        functools.partial(_mix_kernel, tm=tm, seq=seq),
        grid=(t // tm,),
        in_specs=[
            pl.BlockSpec((tm, attn.shape[1]), lambda i: (i, 0)),
            window(tm, 1, 2, tile), window(tm, 3, 4, tile), window(tm, 7, 4, tile),
            window(HALO_ROWS, 4, 2, prev_rows), window(HALO_ROWS, 4, 2, next_rows),
            resident(conv_w.shape), resident(conv_b.shape),
            resident(wba.shape), resident(wbc.shape),
        ],
        out_specs=pl.BlockSpec((tm, d), lambda i: (i, 0)),
        out_shape=jax.ShapeDtypeStruct((t, d), BF16),
        scratch_shapes=[pltpu.VMEM((tm + 16, SEG), F32)],
        compiler_params=pltpu.CompilerParams(
            dimension_semantics=("arbitrary",), vmem_limit_bytes=VMEM_LIMIT),
        name="mix",
    )(attn, z, z, z, z, z, conv_w, conv_b, wba, wbc)


def _out_ple_kernel(x_ref, m_ref, p_ref, g_ref, wout_ref, wg_ref, wp_ref, o_ref, x1_ref):
    x1_ref[...] = x_ref[...] + jnp.dot(m_ref[...], wout_ref[...], preferred_element_type=F32)
    pb = p_ref[...].astype(BF16)
    x1 = x1_ref[...]
    h = (_rms(x1, x1.shape[-1]) * g_ref[...]).astype(BF16)
    for c in range(x1.shape[-1] // SEG):
        cols = slice(c * SEG, (c + 1) * SEG)
        proj = jnp.dot(pb, wp_ref[:, cols], preferred_element_type=F32)
        gate = _sigmoid(jnp.dot(h, wg_ref[:, cols], preferred_element_type=F32))
        o_ref[:, cols] = x1_ref[:, cols] + gate * proj


def _out_ple(x2, merged, p2, ple_g, wout, wg, wp, tm):
    t, d = x2.shape
    resident = lambda shape: pl.BlockSpec(shape, lambda i: (0,) * len(shape),
                                          pipeline_mode=pl.Buffered(1))
    return pl.pallas_call(
        _out_ple_kernel,
        grid=(t // tm,),
        in_specs=[
            pl.BlockSpec((tm, d), lambda i: (i, 0)),
            pl.BlockSpec((tm, d), lambda i: (i, 0)),
            pl.BlockSpec((tm, p2.shape[1]), lambda i: (i, 0)),
            resident((1, d)), resident(wout.shape), resident(wg.shape), resident(wp.shape),
        ],
        out_specs=pl.BlockSpec((tm, d), lambda i: (i, 0)),
        out_shape=jax.ShapeDtypeStruct((t, d), F32),
        scratch_shapes=[pltpu.VMEM((tm, d), F32)],
        compiler_params=pltpu.CompilerParams(
            dimension_semantics=("arbitrary",), vmem_limit_bytes=VMEM_LIMIT),
        name="out_ple",
    )(x2, merged, p2, ple_g, wout, wg, wp)


def _layer(x2, p2, pos, freq, b, s, norm_g, w_in, q_lat_g, kv_lat_g, w_uq, w_ukv,
           q_norm_g, k_norm_g, conv_w, conv_b, w_branch_attn, w_branch_conv,
           w_out, ple_norm_g, w_ple_gate, w_ple_proj):
    d = x2.shape[1]
    pad_g = lambda g: jnp.pad(g, (0, HEAD_PAD - QK_HEAD_DIM)).reshape(1, HEAD_PAD)

    z, rot = _in_proj(x2, norm_g.reshape(1, d), jnp.swapaxes(w_in, 0, 1), pos, freq,
                      tm=IN_PROJ_TM)
    attn = _mla_attention(z, rot, q_lat_g.reshape(1, -1), kv_lat_g.reshape(1, -1),
                          w_uq, w_ukv, pad_g(q_norm_g), pad_g(k_norm_g),
                          b, s, tq=512, tp=512, hps=2)
    merged = _mix(attn, z, conv_w, conv_b.reshape(1, -1), w_branch_attn.astype(BF16),
                  w_branch_conv.astype(BF16), seq=s, tm=512)
    return _out_ple(x2, merged, p2, ple_norm_g.reshape(1, d), w_out.astype(BF16),
                    w_ple_gate.astype(BF16), w_ple_proj.astype(BF16), tm=512)


def kernel(x, p, positions, norm_g, w_in, q_lat_g, kv_lat_g, w_uq, w_ukv, q_norm_g, k_norm_g, conv_w, conv_b, w_branch_attn, w_branch_conv, w_out, ple_norm_g, w_ple_gate, w_ple_proj):
    b, s, d = x.shape
    depth = p.shape[0]
    x2 = x.reshape(b * s, d)
    half = QK_ROPE_DIM // 2
    pos = positions.reshape(b * s // IN_PROJ_TM, ROT_GROUPS, IN_PROJ_TM // ROT_GROUPS)
    pos = jnp.repeat(jnp.swapaxes(pos, 1, 2), half, axis=2).reshape(b * s // ROT_GROUPS, LANES)
    inv_freq = 1.0 / (ROPE_THETA ** (jnp.arange(0, QK_ROPE_DIM, 2, dtype=F32) / QK_ROPE_DIM))
    freq = jnp.tile(inv_freq, ROT_GROUPS).reshape(1, LANES)
    for i in range(depth):
        x2 = _layer(x2, p[i].reshape(b * s, -1), pos, freq, b, s, norm_g[i], w_in[i],
                    q_lat_g[i], kv_lat_g[i], w_uq[i], w_ukv[i], q_norm_g[i], k_norm_g[i],
                    conv_w[i], conv_b[i], w_branch_attn[i], w_branch_conv[i], w_out[i],
                    ple_norm_g[i], w_ple_gate[i], w_ple_proj[i])
    return x2.reshape(b, s, d)
```

```python
import functools

import jax
import jax.numpy as jnp
from jax import lax
from jax.experimental import pallas as pl
from jax.experimental.pallas import tpu as pltpu

N_HEADS = 16
QK_NOPE_DIM = 128
QK_ROPE_DIM = 64
QK_HEAD_DIM = QK_NOPE_DIM + QK_ROPE_DIM
V_HEAD_DIM = 128
Q_LORA_RANK = 512
KV_LORA_RANK = 256
ROPE_THETA = 10000.0
EPS = 1e-6
LOG2E = 1.4426950408889634

LANES = 128
HEAD_PAD = 2 * LANES
SEG = 1024
HEAD_COLS = Q_LORA_RANK + KV_LORA_RANK + QK_ROPE_DIM
HEAD_COLS_ALIGN = 64
assert HEAD_COLS % HEAD_COLS_ALIGN == 0 and SEG % HEAD_COLS_ALIGN == 0
HALO_ROWS = 16
VMEM_LIMIT = 56 * 1024 * 1024

BF16 = jnp.bfloat16
F32 = jnp.float32


def _rms(v, axis_size):
    return v * lax.rsqrt(jnp.sum(v * v, axis=-1, keepdims=True) * (1.0 / axis_size) + EPS)


def _sigmoid(v):
    return 0.5 * jnp.tanh(0.5 * v) + 0.5


Z_TILE_ACTS = ("plain", "silu", "silu", "plain", "plain", "plain", "silu",
               "sigmoid", "sigmoid", "sigmoid", "sigmoid")
ROT_TILE = 3
ROT_GROUPS = LANES // (QK_ROPE_DIM // 2)
IN_PROJ_TM = 1024


def _in_proj_kernel(x_ref, g_ref, wt_ref, pos_ref, freq_ref, z_ref, rot_ref, h_ref):
    j = pl.program_id(1)

    def rotary_tables():
        half = QK_ROPE_DIM // 2
        rq = x_ref.shape[0] // ROT_GROUPS
        lane = lax.broadcasted_iota(jnp.int32, (rq, LANES), 1)
        ang = pos_ref[...].astype(F32) * freq_ref[...]
        cosv, sinv = jnp.cos(ang), jnp.sin(ang)
        for q in range(ROT_GROUPS):
            rows = slice(q * rq, (q + 1) * rq)
            to_x1 = (lambda v: v) if q == 0 else (
                lambda v: pltpu.roll(v, LANES - q * half, axis=1))
            to_x2 = lambda v: pltpu.roll(v, (LANES - q * half + half) % LANES, axis=1)
            rot_ref[rows, 0:LANES] = jnp.where(lane < half, to_x1(cosv), to_x2(cosv))
            rot_ref[rows, LANES:2 * LANES] = jnp.where(lane < half, -to_x1(sinv), 0.0)
            rot_ref[rows, 2 * LANES:3 * LANES] = jnp.where(
                (lane >= half) & (lane < QK_ROPE_DIM), to_x2(sinv), 0.0)

    def is_act(*acts):
        return functools.reduce(jnp.logical_or,
                                [j == tile for tile, a in enumerate(Z_TILE_ACTS) if a in acts])

    def matmul(h=None):
        return lax.dot_general(h_ref[...] if h is None else h, wt_ref[...].astype(BF16),
                               (((1,), (1,)), ((), ())), preferred_element_type=F32)

    assert Z_TILE_ACTS[0] == Z_TILE_ACTS[ROT_TILE] == "plain" and ROT_TILE != 0

    @pl.when(j == 0)
    def _():
        x = x_ref[...]
        h = (_rms(x, x.shape[-1]) * g_ref[...]).astype(BF16)
        h_ref[...] = h
        z_ref[...] = matmul(h).astype(BF16)

    @pl.when(j == ROT_TILE)
    def _():
        z_ref[...] = matmul().astype(BF16)
        rotary_tables()

    @pl.when(is_act("plain") & (j != 0) & (j != ROT_TILE))
    def _():
        z_ref[...] = matmul().astype(BF16)

    @pl.when(is_act("silu", "sigmoid"))
    def _():
        hv = 0.5 * matmul()
        a = jnp.where(is_act("silu"), hv, 0.5)
        z_ref[...] = (a * jnp.tanh(hv) + a).astype(BF16)


def _in_proj(x2, norm_g, w_t, pos, freq, tm):
    t, d = x2.shape
    n_tiles = len(Z_TILE_ACTS)
    assert w_t.shape == (HEAD_COLS + (n_tiles - 1) * SEG, d)
    return pl.pallas_call(
        _in_proj_kernel,
        grid=(t // tm, n_tiles),
        in_specs=[
            pl.BlockSpec((tm, d), lambda i, j: (i, 0)),
            pl.BlockSpec((1, d), lambda i, j: (0, 0)),
            pl.BlockSpec((pl.Element(SEG), pl.Element(d)),
                         lambda i, j: (pl.multiple_of(
                             jnp.maximum(j * SEG - (SEG - HEAD_COLS), 0), HEAD_COLS_ALIGN), 0)),
            pl.BlockSpec((tm // ROT_GROUPS, LANES), lambda i, j: (i, 0)),
            pl.BlockSpec((1, LANES), lambda i, j: (0, 0)),
        ],
        out_specs=[pl.BlockSpec((tm, SEG), lambda i, j: (i, j)),
                   pl.BlockSpec((tm, 3 * LANES), lambda i, j: (i, 0))],
        out_shape=[jax.ShapeDtypeStruct((t, n_tiles * SEG), BF16),
                   jax.ShapeDtypeStruct((t, 3 * LANES), F32)],
        scratch_shapes=[pltpu.VMEM((tm, d), BF16)],
        compiler_params=pltpu.CompilerParams(
            dimension_semantics=("arbitrary", "arbitrary"),
            vmem_limit_bytes=VMEM_LIMIT),
        name="in_proj",
    )(x2, norm_g, w_t, pos, freq)


def _mla_kernel(*refs, tq, hps):
    z_ref, rot_ref, qlg_ref, kvlg_ref, qg_ref, kg_ref, wuq0_ref, wukv0_ref = refs[:8]
    w_following = refs[8:8 + 2 * hps]
    o_ref = refs[8 + 2 * hps]
    (qn_ref, kvn_ref, krope_ref, kss_ref,
     qa_ref, ka_ref, va_ref, qb_ref, kb_ref, vb_ref) = refs[9 + 2 * hps:]
    s_len = z_ref.shape[0]
    n_chunks = s_len // tq
    slot_a = (qa_ref, ka_ref, va_ref)
    slot_b = (qb_ref, kb_ref, vb_ref)
    half = QK_ROPE_DIM // 2
    scale = QK_HEAD_DIM ** -0.5 * LOG2E
    inv_d = 1.0 / QK_HEAD_DIM
    qg_nope, qg_rope = qg_ref[:, 0:LANES], qg_ref[:, LANES:HEAD_PAD]
    kg_nope, kg_rope = kg_ref[:, 0:LANES], kg_ref[:, LANES:HEAD_PAD]

    def rope(u, rows):
        return (u * rot_ref[rows, 0:LANES]
                + pltpu.roll(u, LANES - half, axis=1) * rot_ref[rows, LANES:2 * LANES]
                + pltpu.roll(u, half, axis=1) * rot_ref[rows, 2 * LANES:3 * LANES])

    def head_weights(wuq_pair_ref, wukv_ref, odd):
        lane = lax.broadcasted_iota(jnp.int32, (Q_LORA_RANK, LANES), 1)
        if odd:
            mid = pltpu.roll(wuq_pair_ref[:, LANES:2 * LANES], QK_ROPE_DIM, axis=1)
            end = pltpu.roll(wuq_pair_ref[:, 2 * LANES:3 * LANES], QK_ROPE_DIM, axis=1)
            nope = jnp.where(lane < QK_ROPE_DIM, mid, end)
            rope = jnp.where(lane < QK_ROPE_DIM, end, 0.0)
        else:
            nope = wuq_pair_ref[:, 0:LANES]
            rope = jnp.where(lane < QK_ROPE_DIM, wuq_pair_ref[:, LANES:2 * LANES], 0.0)
        return (jnp.concatenate([nope, rope], axis=1).astype(BF16), wukv_ref[...].astype(BF16))

    def prep_rows(wuq, wukv, q_out, k_out, v_out, r):
        rows = slice(r * tq, (r + 1) * tq)
        q = jnp.dot(qn_ref[rows, :], wuq, preferred_element_type=F32)
        kv = jnp.dot(kvn_ref[rows, :], wukv, preferred_element_type=F32)
        q_nope, q_rope = q[:, 0:LANES], q[:, LANES:HEAD_PAD]
        ss = jnp.sum(q_nope * q_nope + q_rope * q_rope, axis=-1, keepdims=True)
        rq = lax.rsqrt(ss * inv_d + EPS) * scale
        q_out[rows, 0:LANES] = (q_nope * rq * qg_nope).astype(BF16)
        q_out[rows, LANES:HEAD_PAD] = (rope(q_rope * qg_rope, rows) * rq).astype(BF16)
        k_nope = kv[:, 0:LANES]
        ssk = jnp.sum(k_nope * k_nope, axis=-1, keepdims=True) + kss_ref[rows, :]
        rk = lax.rsqrt(ssk * inv_d + EPS)
        k_out[rows, 0:LANES] = (k_nope * rk * kg_nope).astype(BF16)
        k_out[rows, LANES:HEAD_PAD] = (krope_ref[rows, :] * rk).astype(BF16)
        v_out[rows, 0:LANES] = kv[:, LANES:HEAD_PAD].astype(BF16)

    def scores(slot, i):
        q_in, k_in, _ = slot
        return lax.dot_general(q_in[i * tq:(i + 1) * tq, :], k_in[...], (((1,), (1,)), ((), ())),
                               preferred_element_type=F32)

    def finish(sc, slot, col, i):
        m = jnp.max(sc, axis=-1, keepdims=True)
        p = jnp.exp2(sc - m).astype(BF16)
        o = jnp.dot(p, slot[2][...], preferred_element_type=F32)
        o_ref[i * tq:(i + 1) * tq, col * LANES:(col + 1) * LANES] = (
            o[:, :V_HEAD_DIM] / o[:, V_HEAD_DIM:V_HEAD_DIM + 1]).astype(BF16)

    @pl.when(pl.program_id(1) == 0)
    def _():
        for r in range(n_chunks):
            rows = slice(r * tq, (r + 1) * tq)
            qa = z_ref[rows, 0:Q_LORA_RANK].astype(F32)
            kva = z_ref[rows, Q_LORA_RANK:Q_LORA_RANK + KV_LORA_RANK].astype(F32)
            lane = lax.broadcasted_iota(jnp.int32, (tq, LANES), 1)
            kpe = jnp.where(lane < QK_ROPE_DIM,
                            z_ref[rows, HEAD_COLS - QK_ROPE_DIM:HEAD_COLS - QK_ROPE_DIM + LANES
                                  ].astype(F32), 0.0)
            qn_ref[rows, :] = (_rms(qa, Q_LORA_RANK) * qlg_ref[...]).astype(BF16)
            kvn_ref[rows, :] = (_rms(kva, KV_LORA_RANK) * kvlg_ref[...]).astype(BF16)
            krope_ref[rows, :] = rope(kpe * kg_rope, rows)
            kss_ref[rows, :] = jnp.sum(kpe * kpe, axis=-1, keepdims=True)
            ones_col = jnp.where(lane == 0, 1.0, 0.0).astype(BF16)
            va_ref[rows, LANES:HEAD_PAD] = ones_col
            vb_ref[rows, LANES:HEAD_PAD] = ones_col
        w_first = head_weights(wuq0_ref, wukv0_ref, odd=False)
        for r in range(n_chunks):
            prep_rows(*w_first, *slot_a, r)

    slots = (slot_a, slot_b)
    total = hps * n_chunks
    sc_next = scores(slot_a, 0)
    for t in range(total):
        hd, i = divmod(t, n_chunks)
        sc_cur = sc_next
        if i == 0:
            w_head = head_weights(*w_following[2 * hd:2 * hd + 2], odd=(hd + 1) % 2 == 1)
        prep_rows(*w_head, *slots[(hd + 1) % 2], i)
        if t + 1 < total:
            hd2, i2 = divmod(t + 1, n_chunks)
            sc_next = scores(slots[hd2 % 2], i2)
        finish(sc_cur, slots[hd % 2], hd, i)


def _mla_attention(z, rot, q_lat_g, kv_lat_g, w_uq, w_ukv, qg_p, kg_p, b, s, tq, hps):
    assert hps % 2 == 0 and N_HEADS % hps == 0
    const = lambda shape: pl.BlockSpec(shape, lambda bi, g: (0,) * len(shape))
    wuq_pair = lambda f: pl.BlockSpec((Q_LORA_RANK, 2 * QK_HEAD_DIM), lambda bi, g: (0, f(g) // 2))
    wukv_head = lambda f: pl.BlockSpec((KV_LORA_RANK, HEAD_PAD), lambda bi, g: (0, f(g)))
    following = lambda k: (lambda g: jnp.minimum(hps * g + k, N_HEADS - 2 + k % 2))
    following_specs = []
    for k in range(1, hps + 1):
        following_specs += [wuq_pair(following(k)), wukv_head(following(k))]
    rows = lambda w, dt: pltpu.VMEM((s, w), dt)
    return pl.pallas_call(
        functools.partial(_mla_kernel, tq=tq, hps=hps),
        grid=(b, N_HEADS // hps),
        in_specs=[
            pl.BlockSpec((s, SEG), lambda bi, g: (bi, 0)),
            pl.BlockSpec((s, 3 * LANES), lambda bi, g: (bi, 0)),
            const((1, Q_LORA_RANK)), const((1, KV_LORA_RANK)),
            const((1, HEAD_PAD)), const((1, HEAD_PAD)),
            wuq_pair(lambda g: 0), wukv_head(lambda g: 0),
        ] + following_specs,
        out_specs=pl.BlockSpec((s, hps * V_HEAD_DIM), lambda bi, g: (bi, g)),
        out_shape=jax.ShapeDtypeStruct((b * s, N_HEADS * V_HEAD_DIM), BF16),
        scratch_shapes=[
            rows(Q_LORA_RANK, BF16), rows(KV_LORA_RANK, BF16),
            rows(LANES, F32), rows(1, F32),
            rows(HEAD_PAD, BF16), rows(HEAD_PAD, BF16), rows(HEAD_PAD, BF16),
            rows(HEAD_PAD, BF16), rows(HEAD_PAD, BF16), rows(HEAD_PAD, BF16),
        ],
        compiler_params=pltpu.CompilerParams(
            dimension_semantics=("arbitrary", "arbitrary"), vmem_limit_bytes=VMEM_LIMIT),
        name="mla_attention",
    )(z, rot, q_lat_g, kv_lat_g, qg_p, kg_p, *([w_uq, w_ukv] * (hps + 1)))


def _mix_kernel(attn_ref, ga_ref, conv_ref, mrg_ref, prev_ref, next_ref,
                cw_ref, cbias_ref, wba_ref, wbc_ref, o_ref, u_ref, *, tm, seq):
    i = pl.program_id(0)
    seg = lambda ref, c: ref[:, c * SEG:(c + 1) * SEG]

    a = attn_ref[...] * ga_ref[...]
    halves = tuple(slice(c * SEG, (c + 1) * SEG) for c in range(2))
    y_attn = [jnp.dot(a, wba_ref[:, cols], preferred_element_type=F32) for cols in halves]

    u = seg(conv_ref, 1).astype(F32) * seg(conv_ref, 2).astype(F32)
    first = (i * tm) % seq == 0
    last = ((i + 1) * tm) % seq == 0
    prev = (prev_ref[HALO_ROWS - 1:HALO_ROWS, 0:SEG].astype(F32)
            * prev_ref[HALO_ROWS - 1:HALO_ROWS, SEG:2 * SEG].astype(F32))
    nxt = next_ref[0:1, 0:SEG].astype(F32) * next_ref[0:1, SEG:2 * SEG].astype(F32)
    u_ref[7:8, :] = jnp.where(first, 0.0, prev)
    u_ref[8:8 + tm, :] = u
    u_ref[8 + tm:9 + tm, :] = jnp.where(last, 0.0, nxt)
    conv = (u_ref[7:7 + tm, :] * cw_ref[0:1, :] + u * cw_ref[1:2, :]
            + u_ref[9:9 + tm, :] * cw_ref[2:3, :] + cbias_ref[...])
    cv = (seg(conv_ref, 0).astype(F32) * conv * seg(conv_ref, 3).astype(F32)).astype(BF16)

    for c in range(2):
        y_conv = jnp.dot(cv, wbc_ref[:, halves[c]], preferred_element_type=F32)
        o_ref[:, halves[c]] = (seg(mrg_ref, c).astype(F32) * y_attn[c]
                               + seg(mrg_ref, 2 + c).astype(F32) * y_conv).astype(BF16)


def _mix(attn, z, conv_w, conv_b, wba, wbc, seq, tm):
    t, d = attn.shape[0], wba.shape[1]

    def window(rows, first_seg, n_seg, row_start):
        return pl.BlockSpec(
            (pl.Element(rows), pl.Element(n_seg * SEG)),
            lambda i: (pl.multiple_of(row_start(i), HALO_ROWS), first_seg * SEG))

    tile = lambda i: i * tm
    prev_rows = lambda i: jnp.maximum(i * tm - HALO_ROWS, 0)
    next_rows = lambda i: jnp.minimum((i + 1) * tm, t - HALO_ROWS)
    resident = lambda shape: pl.BlockSpec(shape, lambda i: (0,) * len(shape),
                                          pipeline_mode=pl.Buffered(1))
    return pl.pallas_call(
        functools.partial(_mix_kernel, tm=tm, seq=seq),
        grid=(t // tm,),
        in_specs=[
            pl.BlockSpec((tm, attn.shape[1]), lambda i: (i, 0)),
            window(tm, 1, 2, tile), window(tm, 3, 4, tile), window(tm, 7, 4, tile),
            window(HALO_ROWS, 4, 2, prev_rows), window(HALO_ROWS, 4, 2, next_rows),
            resident(conv_w.shape), resident(conv_b.shape),
            resident(wba.shape), resident(wbc.shape),
        ],
        out_specs=pl.BlockSpec((tm, d), lambda i: (i, 0)),
        out_shape=jax.ShapeDtypeStruct((t, d), BF16),
        scratch_shapes=[pltpu.VMEM((tm + 16, SEG), F32)],
        compiler_params=pltpu.CompilerParams(
            dimension_semantics=("arbitrary",), vmem_limit_bytes=VMEM_LIMIT),
        name="mix",
    )(attn, z, z, z, z, z, conv_w, conv_b, wba, wbc)


def _out_ple_kernel(x_ref, m_ref, p_ref, g_ref, wout_ref, wg_ref, wp_ref, o_ref, x1_ref):
    x1_ref[...] = x_ref[...] + jnp.dot(m_ref[...], wout_ref[...], preferred_element_type=F32)
    pb = p_ref[...].astype(BF16)
    x1 = x1_ref[...]
    h = (_rms(x1, x1.shape[-1]) * g_ref[...]).astype(BF16)
    for c in range(x1.shape[-1] // SEG):
        cols = slice(c * SEG, (c + 1) * SEG)
        proj = jnp.dot(pb, wp_ref[:, cols], preferred_element_type=F32)
        gate = _sigmoid(jnp.dot(h, wg_ref[:, cols], preferred_element_type=F32))
        o_ref[:, cols] = x1_ref[:, cols] + gate * proj


def _out_ple(x2, merged, p2, ple_g, wout, wg, wp, tm):
    t, d = x2.shape
    resident = lambda shape: pl.BlockSpec(shape, lambda i: (0,) * len(shape),
                                          pipeline_mode=pl.Buffered(1))
    return pl.pallas_call(
        _out_ple_kernel,
        grid=(t // tm,),
        in_specs=[
            pl.BlockSpec((tm, d), lambda i: (i, 0)),
            pl.BlockSpec((tm, d), lambda i: (i, 0)),
            pl.BlockSpec((tm, p2.shape[1]), lambda i: (i, 0)),
            resident((1, d)), resident(wout.shape), resident(wg.shape), resident(wp.shape),
        ],
        out_specs=pl.BlockSpec((tm, d), lambda i: (i, 0)),
        out_shape=jax.ShapeDtypeStruct((t, d), F32),
        scratch_shapes=[pltpu.VMEM((tm, d), F32)],
        compiler_params=pltpu.CompilerParams(
            dimension_semantics=("arbitrary",), vmem_limit_bytes=VMEM_LIMIT),
        name="out_ple",
    )(x2, merged, p2, ple_g, wout, wg, wp)


def _layer(x2, p2, pos, freq, b, s, norm_g, w_in, q_lat_g, kv_lat_g, w_uq, w_ukv,
           q_norm_g, k_norm_g, conv_w, conv_b, w_branch_attn, w_branch_conv,
           w_out, ple_norm_g, w_ple_gate, w_ple_proj):
    d = x2.shape[1]
    pad_g = lambda g: jnp.pad(g, (0, HEAD_PAD - QK_HEAD_DIM)).reshape(1, HEAD_PAD)

    z, rot = _in_proj(x2, norm_g.reshape(1, d), jnp.swapaxes(w_in, 0, 1), pos, freq,
                      tm=IN_PROJ_TM)
    attn = _mla_attention(z, rot, q_lat_g.reshape(1, -1), kv_lat_g.reshape(1, -1),
                          w_uq, w_ukv, pad_g(q_norm_g), pad_g(k_norm_g),
                          b, s, tq=512, hps=2)
    merged = _mix(attn, z, conv_w, conv_b.reshape(1, -1), w_branch_attn.astype(BF16),
                  w_branch_conv.astype(BF16), seq=s, tm=512)
    return _out_ple(x2, merged, p2, ple_norm_g.reshape(1, d), w_out.astype(BF16),
                    w_ple_gate.astype(BF16), w_ple_proj.astype(BF16), tm=512)


def kernel(x, p, positions, norm_g, w_in, q_lat_g, kv_lat_g, w_uq, w_ukv, q_norm_g, k_norm_g, conv_w, conv_b, w_branch_attn, w_branch_conv, w_out, ple_norm_g, w_ple_gate, w_ple_proj):
    b, s, d = x.shape
    depth = p.shape[0]
    x2 = x.reshape(b * s, d)
    half = QK_ROPE_DIM // 2
    pos = positions.reshape(b * s // IN_PROJ_TM, ROT_GROUPS, IN_PROJ_TM // ROT_GROUPS)
    pos = jnp.repeat(jnp.swapaxes(pos, 1, 2), half, axis=2).reshape(b * s // ROT_GROUPS, LANES)
    inv_freq = 1.0 / (ROPE_THETA ** (jnp.arange(0, QK_ROPE_DIM, 2, dtype=F32) / QK_ROPE_DIM))
    freq = jnp.tile(inv_freq, ROT_GROUPS).reshape(1, LANES)
    for i in range(depth):
        x2 = _layer(x2, p[i].reshape(b * s, -1), pos, freq, b, s, norm_g[i], w_in[i],
                    q_lat_g[i], kv_lat_g[i], w_uq[i], w_ukv[i], q_norm_g[i], k_norm_g[i],
                    conv_w[i], conv_b[i], w_branch_attn[i], w_branch_conv[i], w_out[i],
                    ple_norm_g[i], w_ple_gate[i], w_ple_proj[i])
    return x2.reshape(b, s, d)
```

```python
import functools

import jax
import jax.numpy as jnp
from jax import lax
from jax.experimental import pallas as pl
from jax.experimental.pallas import tpu as pltpu

N_HEADS = 16
QK_NOPE_DIM = 128
QK_ROPE_DIM = 64
QK_HEAD_DIM = QK_NOPE_DIM + QK_ROPE_DIM
V_HEAD_DIM = 128
Q_LORA_RANK = 512
KV_LORA_RANK = 256
ROPE_THETA = 10000.0
EPS = 1e-6
LOG2E = 1.4426950408889634

LANES = 128
HEAD_PAD = 2 * LANES
SEG = 1024
HEAD_COLS = Q_LORA_RANK + KV_LORA_RANK + QK_ROPE_DIM
HEAD_COLS_ALIGN = 64
assert HEAD_COLS % HEAD_COLS_ALIGN == 0 and SEG % HEAD_COLS_ALIGN == 0
HALO_ROWS = 16
VMEM_LIMIT = 56 * 1024 * 1024

BF16 = jnp.bfloat16
F32 = jnp.float32


def _rms(v, axis_size):
    return v * lax.rsqrt(jnp.sum(v * v, axis=-1, keepdims=True) * (1.0 / axis_size) + EPS)


def _sigmoid(v):
    return 0.5 * jnp.tanh(0.5 * v) + 0.5


WEIGHT_CHUNK_ROWS = 256


def _weight_staging(cols):
    return [pltpu.VMEM((2, WEIGHT_CHUNK_ROWS, cols), F32), pltpu.SemaphoreType.DMA((2,))]


def _load_weights_bf16(pairs, stage_ref, sem_ref):
    chunks = []
    for w_hbm, w_vmem in pairs:
        assert w_hbm.shape[0] % WEIGHT_CHUNK_ROWS == 0 and w_hbm.shape[1] == stage_ref.shape[2]
        for r in range(0, w_hbm.shape[0], WEIGHT_CHUNK_ROWS):
            chunks.append((w_hbm, w_vmem, r))

    def copy(c):
        w_hbm, _, r = chunks[c]
        return pltpu.make_async_copy(w_hbm.at[pl.ds(r, WEIGHT_CHUNK_ROWS)],
                                     stage_ref.at[c % 2], sem_ref.at[c % 2])

    copy(0).start()
    for c, (_, w_vmem, r) in enumerate(chunks):
        if c + 1 < len(chunks):
            copy(c + 1).start()
        copy(c).wait()
        w_vmem[r:r + WEIGHT_CHUNK_ROWS, :] = stage_ref[c % 2].astype(BF16)


Z_TILE_ACTS = ("plain", "silu", "silu", "plain", "plain", "plain", "silu",
               "sigmoid", "sigmoid", "sigmoid", "sigmoid")
ROT_TILE = 3
ROT_GROUPS = LANES // (QK_ROPE_DIM // 2)
IN_PROJ_TM = 1024


def _in_proj_kernel(x_ref, g_ref, wt_ref, pos_ref, freq_ref, z_ref, rot_ref, h_ref):
    j = pl.program_id(1)

    def rotary_tables():
        half = QK_ROPE_DIM // 2
        rq = x_ref.shape[0] // ROT_GROUPS
        lane = lax.broadcasted_iota(jnp.int32, (rq, LANES), 1)
        ang = pos_ref[...].astype(F32) * freq_ref[...]
        cosv, sinv = jnp.cos(ang), jnp.sin(ang)
        for q in range(ROT_GROUPS):
            rows = slice(q * rq, (q + 1) * rq)
            to_x1 = (lambda v: v) if q == 0 else (
                lambda v: pltpu.roll(v, LANES - q * half, axis=1))
            to_x2 = lambda v: pltpu.roll(v, (LANES - q * half + half) % LANES, axis=1)
            rot_ref[rows, 0:LANES] = jnp.where(lane < half, to_x1(cosv), to_x2(cosv))
            rot_ref[rows, LANES:2 * LANES] = jnp.where(lane < half, -to_x1(sinv), 0.0)
            rot_ref[rows, 2 * LANES:3 * LANES] = jnp.where(
                (lane >= half) & (lane < QK_ROPE_DIM), to_x2(sinv), 0.0)

    def is_act(*acts):
        return functools.reduce(jnp.logical_or,
                                [j == tile for tile, a in enumerate(Z_TILE_ACTS) if a in acts])

    def matmul(h=None):
        return lax.dot_general(h_ref[...] if h is None else h, wt_ref[...].astype(BF16),
                               (((1,), (1,)), ((), ())), preferred_element_type=F32)

    assert Z_TILE_ACTS[0] == Z_TILE_ACTS[ROT_TILE] == "plain" and ROT_TILE != 0

    @pl.when(j == 0)
    def _():
        x = x_ref[...]
        h = (_rms(x, x.shape[-1]) * g_ref[...]).astype(BF16)
        h_ref[...] = h
        z_ref[...] = matmul(h).astype(BF16)

    @pl.when(j == ROT_TILE)
    def _():
        z_ref[...] = matmul().astype(BF16)
        rotary_tables()

    @pl.when(is_act("plain") & (j != 0) & (j != ROT_TILE))
    def _():
        z_ref[...] = matmul().astype(BF16)

    @pl.when(is_act("silu", "sigmoid"))
    def _():
        hv = 0.5 * matmul()
        a = jnp.where(is_act("silu"), hv, 0.5)
        z_ref[...] = (a * jnp.tanh(hv) + a).astype(BF16)


def _in_proj(x2, norm_g, w_t, pos, freq, tm):
    t, d = x2.shape
    n_tiles = len(Z_TILE_ACTS)
    assert w_t.shape == (HEAD_COLS + (n_tiles - 1) * SEG, d)
    return pl.pallas_call(
        _in_proj_kernel,
        grid=(t // tm, n_tiles),
        in_specs=[
            pl.BlockSpec((tm, d), lambda i, j: (i, 0)),
            pl.BlockSpec((1, d), lambda i, j: (0, 0)),
            pl.BlockSpec((pl.Element(SEG), pl.Element(d)),
                         lambda i, j: (pl.multiple_of(
                             jnp.maximum(j * SEG - (SEG - HEAD_COLS), 0), HEAD_COLS_ALIGN), 0)),
            pl.BlockSpec((tm // ROT_GROUPS, LANES), lambda i, j: (i, 0)),
            pl.BlockSpec((1, LANES), lambda i, j: (0, 0)),
        ],
        out_specs=[pl.BlockSpec((tm, SEG), lambda i, j: (i, j)),
                   pl.BlockSpec((tm, 3 * LANES), lambda i, j: (i, 0))],
        out_shape=[jax.ShapeDtypeStruct((t, n_tiles * SEG), BF16),
                   jax.ShapeDtypeStruct((t, 3 * LANES), F32)],
        scratch_shapes=[pltpu.VMEM((tm, d), BF16)],
        compiler_params=pltpu.CompilerParams(
            dimension_semantics=("arbitrary", "arbitrary"),
            vmem_limit_bytes=VMEM_LIMIT),
        name="in_proj",
    )(x2, norm_g, w_t, pos, freq)


def _mla_kernel(*refs, tq, hps):
    z_ref, rot_ref, qlg_ref, kvlg_ref, qg_ref, kg_ref, wuq0_ref, wukv0_ref = refs[:8]
    w_following = refs[8:8 + 2 * hps]
    o_ref = refs[8 + 2 * hps]
    (qn_ref, kvn_ref, krope_ref, kss_ref,
     qa_ref, ka_ref, va_ref, qb_ref, kb_ref, vb_ref) = refs[9 + 2 * hps:]
    s_len = z_ref.shape[0]
    n_chunks = s_len // tq
    slot_a = (qa_ref, ka_ref, va_ref)
    slot_b = (qb_ref, kb_ref, vb_ref)
    half = QK_ROPE_DIM // 2
    scale = QK_HEAD_DIM ** -0.5 * LOG2E
    inv_d = 1.0 / QK_HEAD_DIM
    qg_nope, qg_rope = qg_ref[:, 0:LANES], qg_ref[:, LANES:HEAD_PAD]
    kg_nope, kg_rope = kg_ref[:, 0:LANES], kg_ref[:, LANES:HEAD_PAD]

    def rope(u, rows):
        return (u * rot_ref[rows, 0:LANES]
                + pltpu.roll(u, LANES - half, axis=1) * rot_ref[rows, LANES:2 * LANES]
                + pltpu.roll(u, half, axis=1) * rot_ref[rows, 2 * LANES:3 * LANES])

    def head_weights(wuq_pair_ref, wukv_ref, odd):
        lane = lax.broadcasted_iota(jnp.int32, (Q_LORA_RANK, LANES), 1)
        if odd:
            mid = pltpu.roll(wuq_pair_ref[:, LANES:2 * LANES], QK_ROPE_DIM, axis=1)
            end = pltpu.roll(wuq_pair_ref[:, 2 * LANES:3 * LANES], QK_ROPE_DIM, axis=1)
            nope = jnp.where(lane < QK_ROPE_DIM, mid, end)
            rope = jnp.where(lane < QK_ROPE_DIM, end, 0.0)
        else:
            nope = wuq_pair_ref[:, 0:LANES]
            rope = jnp.where(lane < QK_ROPE_DIM, wuq_pair_ref[:, LANES:2 * LANES], 0.0)
        return (jnp.concatenate([nope, rope], axis=1).astype(BF16), wukv_ref[...].astype(BF16))

    def prep_rows(wuq, wukv, q_out, k_out, v_out, r):
        rows = slice(r * tq, (r + 1) * tq)
        q = jnp.dot(qn_ref[rows, :], wuq, preferred_element_type=F32)
        kv = jnp.dot(kvn_ref[rows, :], wukv, preferred_element_type=F32)
        q_nope, q_rope = q[:, 0:LANES], q[:, LANES:HEAD_PAD]
        ss = jnp.sum(q_nope * q_nope + q_rope * q_rope, axis=-1, keepdims=True)
        rq = lax.rsqrt(ss * inv_d + EPS) * scale
        q_out[rows, 0:LANES] = (q_nope * rq * qg_nope).astype(BF16)
        q_out[rows, LANES:HEAD_PAD] = (rope(q_rope * qg_rope, rows) * rq).astype(BF16)
        k_nope = kv[:, 0:LANES]
        ssk = jnp.sum(k_nope * k_nope, axis=-1, keepdims=True) + kss_ref[rows, :]
        rk = lax.rsqrt(ssk * inv_d + EPS)
        k_out[rows, 0:LANES] = (k_nope * rk * kg_nope).astype(BF16)
        k_out[rows, LANES:HEAD_PAD] = (krope_ref[rows, :] * rk).astype(BF16)
        v_out[rows, 0:LANES] = kv[:, LANES:HEAD_PAD].astype(BF16)

    def scores(slot, i):
        q_in, k_in, _ = slot
        return lax.dot_general(q_in[i * tq:(i + 1) * tq, :], k_in[...], (((1,), (1,)), ((), ())),
                               preferred_element_type=F32)

    def finish(sc, slot, col, i):
        m = jnp.max(sc, axis=-1, keepdims=True)
        p = jnp.exp2(sc - m).astype(BF16)
        o = jnp.dot(p, slot[2][...], preferred_element_type=F32)
        o_ref[i * tq:(i + 1) * tq, col * LANES:(col + 1) * LANES] = (
            o[:, :V_HEAD_DIM] / o[:, V_HEAD_DIM:V_HEAD_DIM + 1]).astype(BF16)

    @pl.when(pl.program_id(1) == 0)
    def _():
        for r in range(n_chunks):
            rows = slice(r * tq, (r + 1) * tq)
            qa = z_ref[rows, 0:Q_LORA_RANK].astype(F32)
            kva = z_ref[rows, Q_LORA_RANK:Q_LORA_RANK + KV_LORA_RANK].astype(F32)
            lane = lax.broadcasted_iota(jnp.int32, (tq, LANES), 1)
            kpe = jnp.where(lane < QK_ROPE_DIM,
                            z_ref[rows, HEAD_COLS - QK_ROPE_DIM:HEAD_COLS - QK_ROPE_DIM + LANES
                                  ].astype(F32), 0.0)
            qn_ref[rows, :] = (_rms(qa, Q_LORA_RANK) * qlg_ref[...]).astype(BF16)
            kvn_ref[rows, :] = (_rms(kva, KV_LORA_RANK) * kvlg_ref[...]).astype(BF16)
            krope_ref[rows, :] = rope(kpe * kg_rope, rows)
            kss_ref[rows, :] = jnp.sum(kpe * kpe, axis=-1, keepdims=True)
            ones_col = jnp.where(lane == 0, 1.0, 0.0).astype(BF16)
            va_ref[rows, LANES:HEAD_PAD] = ones_col
            vb_ref[rows, LANES:HEAD_PAD] = ones_col
        w_first = head_weights(wuq0_ref, wukv0_ref, odd=False)
        for r in range(n_chunks):
            prep_rows(*w_first, *slot_a, r)

    slots = (slot_a, slot_b)
    total = hps * n_chunks
    sc_next = scores(slot_a, 0)
    for t in range(total):
        hd, i = divmod(t, n_chunks)
        sc_cur = sc_next
        if i == 0:
            w_head = head_weights(*w_following[2 * hd:2 * hd + 2], odd=(hd + 1) % 2 == 1)
        prep_rows(*w_head, *slots[(hd + 1) % 2], i)
        if t + 1 < total:
            hd2, i2 = divmod(t + 1, n_chunks)
            sc_next = scores(slots[hd2 % 2], i2)
        finish(sc_cur, slots[hd % 2], hd, i)


def _mla_attention(z, rot, q_lat_g, kv_lat_g, w_uq, w_ukv, qg_p, kg_p, b, s, tq, hps):
    assert hps % 2 == 0 and N_HEADS % hps == 0
    const = lambda shape: pl.BlockSpec(shape, lambda bi, g: (0,) * len(shape))
    wuq_pair = lambda f: pl.BlockSpec((Q_LORA_RANK, 2 * QK_HEAD_DIM), lambda bi, g: (0, f(g) // 2))
    wukv_head = lambda f: pl.BlockSpec((KV_LORA_RANK, HEAD_PAD), lambda bi, g: (0, f(g)))
    following = lambda k: (lambda g: jnp.minimum(hps * g + k, N_HEADS - 2 + k % 2))
    following_specs = []
    for k in range(1, hps + 1):
        following_specs += [wuq_pair(following(k)), wukv_head(following(k))]
    rows = lambda w, dt: pltpu.VMEM((s, w), dt)
    return pl.pallas_call(
        functools.partial(_mla_kernel, tq=tq, hps=hps),
        grid=(b, N_HEADS // hps),
        in_specs=[
            pl.BlockSpec((s, SEG), lambda bi, g: (bi, 0)),
            pl.BlockSpec((s, 3 * LANES), lambda bi, g: (bi, 0)),
            const((1, Q_LORA_RANK)), const((1, KV_LORA_RANK)),
            const((1, HEAD_PAD)), const((1, HEAD_PAD)),
            wuq_pair(lambda g: 0), wukv_head(lambda g: 0),
        ] + following_specs,
        out_specs=pl.BlockSpec((s, hps * V_HEAD_DIM), lambda bi, g: (bi, g)),
        out_shape=jax.ShapeDtypeStruct((b * s, N_HEADS * V_HEAD_DIM), BF16),
        scratch_shapes=[
            rows(Q_LORA_RANK, BF16), rows(KV_LORA_RANK, BF16),
            rows(LANES, F32), rows(1, F32),
            rows(HEAD_PAD, BF16), rows(HEAD_PAD, BF16), rows(HEAD_PAD, BF16),
            rows(HEAD_PAD, BF16), rows(HEAD_PAD, BF16), rows(HEAD_PAD, BF16),
        ],
        compiler_params=pltpu.CompilerParams(
            dimension_semantics=("arbitrary", "arbitrary"), vmem_limit_bytes=VMEM_LIMIT),
        name="mla_attention",
    )(z, rot, q_lat_g, kv_lat_g, qg_p, kg_p, *([w_uq, w_ukv] * (hps + 1)))


def _mix_kernel(attn_ref, ga_ref, conv_ref, mrg_ref, prev_ref, next_ref,
                cw_ref, cbias_ref, wba_hbm, wbc_hbm, o_ref,
                u_ref, wba_ref, wbc_ref, stage_ref, sem_ref, *, tm, seq):
    i = pl.program_id(0)

    @pl.when(i == 0)
    def _():
        _load_weights_bf16(((wba_hbm, wba_ref), (wbc_hbm, wbc_ref)), stage_ref, sem_ref)

    seg = lambda ref, c: ref[:, c * SEG:(c + 1) * SEG]

    a = attn_ref[...] * ga_ref[...]
    halves = tuple(slice(c * SEG, (c + 1) * SEG) for c in range(2))
    y_attn = [jnp.dot(a, wba_ref[:, cols], preferred_element_type=F32) for cols in halves]

    u = seg(conv_ref, 1).astype(F32) * seg(conv_ref, 2).astype(F32)
    first = (i * tm) % seq == 0
    last = ((i + 1) * tm) % seq == 0
    prev = (prev_ref[HALO_ROWS - 1:HALO_ROWS, 0:SEG].astype(F32)
            * prev_ref[HALO_ROWS - 1:HALO_ROWS, SEG:2 * SEG].astype(F32))
    nxt = next_ref[0:1, 0:SEG].astype(F32) * next_ref[0:1, SEG:2 * SEG].astype(F32)
    u_ref[7:8, :] = jnp.where(first, 0.0, prev)
    u_ref[8:8 + tm, :] = u
    u_ref[8 + tm:9 + tm, :] = jnp.where(last, 0.0, nxt)
    conv = (u_ref[7:7 + tm, :] * cw_ref[0:1, :] + u * cw_ref[1:2, :]
            + u_ref[9:9 + tm, :] * cw_ref[2:3, :] + cbias_ref[...])
    cv = (seg(conv_ref, 0).astype(F32) * conv * seg(conv_ref, 3).astype(F32)).astype(BF16)

    for c in range(2):
        y_conv = jnp.dot(cv, wbc_ref[:, halves[c]], preferred_element_type=F32)
        o_ref[:, halves[c]] = (seg(mrg_ref, c).astype(F32) * y_attn[c]
                               + seg(mrg_ref, 2 + c).astype(F32) * y_conv).astype(BF16)


def _mix(attn, z, conv_w, conv_b, wba, wbc, seq, tm):
    t, d = attn.shape[0], wba.shape[1]

    def window(rows, first_seg, n_seg, row_start):
        return pl.BlockSpec(
            (pl.Element(rows), pl.Element(n_seg * SEG)),
            lambda i: (pl.multiple_of(row_start(i), HALO_ROWS), first_seg * SEG))

    tile = lambda i: i * tm
    prev_rows = lambda i: jnp.maximum(i * tm - HALO_ROWS, 0)
    next_rows = lambda i: jnp.minimum((i + 1) * tm, t - HALO_ROWS)
    resident = lambda shape: pl.BlockSpec(shape, lambda i: (0,) * len(shape),
                                          pipeline_mode=pl.Buffered(1))
    return pl.pallas_call(
        functools.partial(_mix_kernel, tm=tm, seq=seq),
        grid=(t // tm,),
        in_specs=[
            pl.BlockSpec((tm, attn.shape[1]), lambda i: (i, 0)),
            window(tm, 1, 2, tile), window(tm, 3, 4, tile), window(tm, 7, 4, tile),
            window(HALO_ROWS, 4, 2, prev_rows), window(HALO_ROWS, 4, 2, next_rows),
            resident(conv_w.shape), resident(conv_b.shape),
            pl.BlockSpec(memory_space=pl.ANY), pl.BlockSpec(memory_space=pl.ANY),
        ],
        out_specs=pl.BlockSpec((tm, d), lambda i: (i, 0)),
        out_shape=jax.ShapeDtypeStruct((t, d), BF16),
        scratch_shapes=[pltpu.VMEM((tm + 16, SEG), F32), pltpu.VMEM(wba.shape, BF16),
                        pltpu.VMEM(wbc.shape, BF16)] + _weight_staging(d),
        compiler_params=pltpu.CompilerParams(
            dimension_semantics=("arbitrary",), vmem_limit_bytes=VMEM_LIMIT),
        name="mix",
    )(attn, z, z, z, z, z, conv_w, conv_b, wba, wbc)


def _out_ple_kernel(x_ref, m_ref, p_ref, g_ref, wout_hbm, wg_hbm, wp_hbm, o_ref,
                    x1_ref, wout_ref, wg_ref, wp_ref, stage_ref, sem_ref):
    @pl.when(pl.program_id(0) == 0)
    def _():
        _load_weights_bf16(((wout_hbm, wout_ref), (wg_hbm, wg_ref), (wp_hbm, wp_ref)),
                           stage_ref, sem_ref)

    x1_ref[...] = x_ref[...] + jnp.dot(m_ref[...], wout_ref[...], preferred_element_type=F32)
    pb = p_ref[...].astype(BF16)
    x1 = x1_ref[...]
    h = (_rms(x1, x1.shape[-1]) * g_ref[...]).astype(BF16)
    for c in range(x1.shape[-1] // SEG):
        cols = slice(c * SEG, (c + 1) * SEG)
        proj = jnp.dot(pb, wp_ref[:, cols], preferred_element_type=F32)
        gate = _sigmoid(jnp.dot(h, wg_ref[:, cols], preferred_element_type=F32))
        o_ref[:, cols] = x1_ref[:, cols] + gate * proj


def _out_ple(x2, merged, p2, ple_g, wout, wg, wp, tm):
    t, d = x2.shape
    resident = lambda shape: pl.BlockSpec(shape, lambda i: (0,) * len(shape),
                                          pipeline_mode=pl.Buffered(1))
    in_hbm = pl.BlockSpec(memory_space=pl.ANY)
    return pl.pallas_call(
        _out_ple_kernel,
        grid=(t // tm,),
        in_specs=[
            pl.BlockSpec((tm, d), lambda i: (i, 0)),
            pl.BlockSpec((tm, d), lambda i: (i, 0)),
            pl.BlockSpec((tm, p2.shape[1]), lambda i: (i, 0)),
            resident((1, d)), in_hbm, in_hbm, in_hbm,
        ],
        out_specs=pl.BlockSpec((tm, d), lambda i: (i, 0)),
        out_shape=jax.ShapeDtypeStruct((t, d), F32),
        scratch_shapes=[pltpu.VMEM((tm, d), F32),
                        pltpu.VMEM(wout.shape, BF16), pltpu.VMEM(wg.shape, BF16),
                        pltpu.VMEM(wp.shape, BF16)] + _weight_staging(d),
        compiler_params=pltpu.CompilerParams(
            dimension_semantics=("arbitrary",), vmem_limit_bytes=VMEM_LIMIT),
        name="out_ple",
    )(x2, merged, p2, ple_g, wout, wg, wp)


def _layer(x2, p2, pos, freq, b, s, norm_g, w_in, q_lat_g, kv_lat_g, w_uq, w_ukv,
           q_norm_g, k_norm_g, conv_w, conv_b, w_branch_attn, w_branch_conv,
           w_out, ple_norm_g, w_ple_gate, w_ple_proj):
    d = x2.shape[1]
    pad_g = lambda g: jnp.pad(g, (0, HEAD_PAD - QK_HEAD_DIM)).reshape(1, HEAD_PAD)

    z, rot = _in_proj(x2, norm_g.reshape(1, d), jnp.swapaxes(w_in, 0, 1), pos, freq,
                      tm=IN_PROJ_TM)
    attn = _mla_attention(z, rot, q_lat_g.reshape(1, -1), kv_lat_g.reshape(1, -1),
                          w_uq, w_ukv, pad_g(q_norm_g), pad_g(k_norm_g),
                          b, s, tq=512, hps=2)
    merged = _mix(attn, z, conv_w, conv_b.reshape(1, -1), w_branch_attn, w_branch_conv,
                  seq=s, tm=512)
    return _out_ple(x2, merged, p2, ple_norm_g.reshape(1, d), w_out, w_ple_gate, w_ple_proj,
                    tm=512)


def kernel(x, p, positions, norm_g, w_in, q_lat_g, kv_lat_g, w_uq, w_ukv, q_norm_g, k_norm_g, conv_w, conv_b, w_branch_attn, w_branch_conv, w_out, ple_norm_g, w_ple_gate, w_ple_proj):
    b, s, d = x.shape
    depth = p.shape[0]
    x2 = x.reshape(b * s, d)
    half = QK_ROPE_DIM // 2
    pos = positions.reshape(b * s // IN_PROJ_TM, ROT_GROUPS, IN_PROJ_TM // ROT_GROUPS)
    pos = jnp.repeat(jnp.swapaxes(pos, 1, 2), half, axis=2).reshape(b * s // ROT_GROUPS, LANES)
    inv_freq = 1.0 / (ROPE_THETA ** (jnp.arange(0, QK_ROPE_DIM, 2, dtype=F32) / QK_ROPE_DIM))
    freq = jnp.tile(inv_freq, ROT_GROUPS).reshape(1, LANES)
    for i in range(depth):
        x2 = _layer(x2, p[i].reshape(b * s, -1), pos, freq, b, s, norm_g[i], w_in[i],
                    q_lat_g[i], kv_lat_g[i], w_uq[i], w_ukv[i], q_norm_g[i], k_norm_g[i],
                    conv_w[i], conv_b[i], w_branch_attn[i], w_branch_conv[i], w_out[i],
                    ple_norm_g[i], w_ple_gate[i], w_ple_proj[i])
    return x2.reshape(b, s, d)
```

```python
import functools

import jax
import jax.numpy as jnp
from jax import lax
from jax.experimental import pallas as pl
from jax.experimental.pallas import tpu as pltpu

N_HEADS = 16
QK_NOPE_DIM = 128
QK_ROPE_DIM = 64
QK_HEAD_DIM = QK_NOPE_DIM + QK_ROPE_DIM
V_HEAD_DIM = 128
Q_LORA_RANK = 512
KV_LORA_RANK = 256
ROPE_THETA = 10000.0
EPS = 1e-6
LOG2E = 1.4426950408889634

LANES = 128
HEAD_PAD = 2 * LANES
SEG = 1024
HEAD_COLS = Q_LORA_RANK + KV_LORA_RANK + QK_ROPE_DIM
HEAD_COLS_ALIGN = 64
assert HEAD_COLS % HEAD_COLS_ALIGN == 0 and SEG % HEAD_COLS_ALIGN == 0
HALO_ROWS = 16
VMEM_LIMIT = 56 * 1024 * 1024

BF16 = jnp.bfloat16
F32 = jnp.float32


def _rms(v, axis_size):
    return v * lax.rsqrt(jnp.sum(v * v, axis=-1, keepdims=True) * (1.0 / axis_size) + EPS)


def _sigmoid(v):
    return 0.5 * jnp.tanh(0.5 * v) + 0.5


WEIGHT_CHUNK_ROWS = 256


def _weight_staging(cols):
    return [pltpu.VMEM((2, WEIGHT_CHUNK_ROWS, cols), F32), pltpu.SemaphoreType.DMA((2,))]


def _load_weights_bf16(pairs, stage_ref, sem_ref):
    chunks = []
    for w_hbm, w_vmem in pairs:
        assert w_hbm.shape[0] % WEIGHT_CHUNK_ROWS == 0 and w_hbm.shape[1] == stage_ref.shape[2]
        for r in range(0, w_hbm.shape[0], WEIGHT_CHUNK_ROWS):
            chunks.append((w_hbm, w_vmem, r))

    def copy(c):
        w_hbm, _, r = chunks[c]
        return pltpu.make_async_copy(w_hbm.at[pl.ds(r, WEIGHT_CHUNK_ROWS)],
                                     stage_ref.at[c % 2], sem_ref.at[c % 2])

    copy(0).start()
    for c, (_, w_vmem, r) in enumerate(chunks):
        if c + 1 < len(chunks):
            copy(c + 1).start()
        copy(c).wait()
        w_vmem[r:r + WEIGHT_CHUNK_ROWS, :] = stage_ref[c % 2].astype(BF16)


Z_TILE_ACTS = ("plain", "silu", "silu", "plain", "plain", "plain", "silu",
               "sigmoid", "sigmoid", "sigmoid", "sigmoid")
ROT_TILE = 3
ROT_GROUPS = LANES // (QK_ROPE_DIM // 2)
IN_PROJ_TM = 1024


def _in_proj_kernel(x_ref, g_ref, wt_ref, pos_ref, freq_ref, z_ref, rot_ref, h_ref):
    j = pl.program_id(1)

    def rotary_tables():
        half = QK_ROPE_DIM // 2
        rq = x_ref.shape[0] // ROT_GROUPS
        lane = lax.broadcasted_iota(jnp.int32, (rq, LANES), 1)
        ang = pos_ref[...].astype(F32) * freq_ref[...]
        cosv, sinv = jnp.cos(ang), jnp.sin(ang)
        for q in range(ROT_GROUPS):
            rows = slice(q * rq, (q + 1) * rq)
            to_x1 = (lambda v: v) if q == 0 else (
                lambda v: pltpu.roll(v, LANES - q * half, axis=1))
            to_x2 = lambda v: pltpu.roll(v, (LANES - q * half + half) % LANES, axis=1)
            rot_ref[rows, 0:LANES] = jnp.where(lane < half, to_x1(cosv), to_x2(cosv))
            rot_ref[rows, LANES:2 * LANES] = jnp.where(lane < half, -to_x1(sinv), 0.0)
            rot_ref[rows, 2 * LANES:3 * LANES] = jnp.where(
                (lane >= half) & (lane < QK_ROPE_DIM), to_x2(sinv), 0.0)

    def is_act(*acts):
        return functools.reduce(jnp.logical_or,
                                [j == tile for tile, a in enumerate(Z_TILE_ACTS) if a in acts])

    def matmul(h=None):
        return lax.dot_general(h_ref[...] if h is None else h, wt_ref[...].astype(BF16),
                               (((1,), (1,)), ((), ())), preferred_element_type=F32)

    assert Z_TILE_ACTS[0] == Z_TILE_ACTS[ROT_TILE] == "plain" and ROT_TILE != 0

    @pl.when(j == 0)
    def _():
        x = x_ref[...]
        h = (_rms(x, x.shape[-1]) * g_ref[...]).astype(BF16)
        h_ref[...] = h
        z_ref[...] = matmul(h).astype(BF16)

    @pl.when(j == ROT_TILE)
    def _():
        z_ref[...] = matmul().astype(BF16)
        rotary_tables()

    @pl.when(is_act("plain") & (j != 0) & (j != ROT_TILE))
    def _():
        z_ref[...] = matmul().astype(BF16)

    @pl.when(is_act("silu", "sigmoid"))
    def _():
        hv = 0.5 * matmul()
        a = jnp.where(is_act("silu"), hv, 0.5)
        z_ref[...] = (a * jnp.tanh(hv) + a).astype(BF16)


def _in_proj(x2, norm_g, w_t, pos, freq, tm):
    t, d = x2.shape
    n_tiles = len(Z_TILE_ACTS)
    assert w_t.shape == (HEAD_COLS + (n_tiles - 1) * SEG, d)
    return pl.pallas_call(
        _in_proj_kernel,
        grid=(t // tm, n_tiles),
        in_specs=[
            pl.BlockSpec((tm, d), lambda i, j: (i, 0)),
            pl.BlockSpec((1, d), lambda i, j: (0, 0)),
            pl.BlockSpec((pl.Element(SEG), pl.Element(d)),
                         lambda i, j: (pl.multiple_of(
                             jnp.maximum(j * SEG - (SEG - HEAD_COLS), 0), HEAD_COLS_ALIGN), 0)),
            pl.BlockSpec((tm // ROT_GROUPS, LANES), lambda i, j: (i, 0)),
            pl.BlockSpec((1, LANES), lambda i, j: (0, 0)),
        ],
        out_specs=[pl.BlockSpec((tm, SEG), lambda i, j: (i, j)),
                   pl.BlockSpec((tm, 3 * LANES), lambda i, j: (i, 0))],
        out_shape=[jax.ShapeDtypeStruct((t, n_tiles * SEG), BF16),
                   jax.ShapeDtypeStruct((t, 3 * LANES), F32)],
        scratch_shapes=[pltpu.VMEM((tm, d), BF16)],
        compiler_params=pltpu.CompilerParams(
            dimension_semantics=("arbitrary", "arbitrary"),
            vmem_limit_bytes=VMEM_LIMIT),
        name="in_proj",
    )(x2, norm_g, w_t, pos, freq)


def _mla_kernel(*refs, tq, hps):
    z_ref, rot_ref, qlg_ref, kvlg_ref, qg_ref, kg_ref, wuq0_ref, wukv0_ref = refs[:8]
    w_following = refs[8:8 + 2 * hps]
    o_ref = refs[8 + 2 * hps]
    (qn_ref, kvn_ref, krope_ref, kss_ref,
     qa_ref, ka_ref, va_ref, qb_ref, kb_ref, vb_ref) = refs[9 + 2 * hps:]
    s_len = z_ref.shape[0]
    n_chunks = s_len // tq
    slot_a = (qa_ref, ka_ref, va_ref)
    slot_b = (qb_ref, kb_ref, vb_ref)
    half = QK_ROPE_DIM // 2
    scale = QK_HEAD_DIM ** -0.5 * LOG2E
    inv_d = 1.0 / QK_HEAD_DIM
    qg_nope, qg_rope = qg_ref[:, 0:LANES], qg_ref[:, LANES:HEAD_PAD]
    kg_nope, kg_rope = kg_ref[:, 0:LANES], kg_ref[:, LANES:HEAD_PAD]

    def rope(u, rows):
        return (u * rot_ref[rows, 0:LANES]
                + pltpu.roll(u, LANES - half, axis=1) * rot_ref[rows, LANES:2 * LANES]
                + pltpu.roll(u, half, axis=1) * rot_ref[rows, 2 * LANES:3 * LANES])

    def head_weights(wuq_pair_ref, wukv_ref, odd):
        lane = lax.broadcasted_iota(jnp.int32, (Q_LORA_RANK, LANES), 1)
        if odd:
            mid = pltpu.roll(wuq_pair_ref[:, LANES:2 * LANES], QK_ROPE_DIM, axis=1)
            end = pltpu.roll(wuq_pair_ref[:, 2 * LANES:3 * LANES], QK_ROPE_DIM, axis=1)
            nope = jnp.where(lane < QK_ROPE_DIM, mid, end)
            rope = jnp.where(lane < QK_ROPE_DIM, end, 0.0)
        else:
            nope = wuq_pair_ref[:, 0:LANES]
            rope = jnp.where(lane < QK_ROPE_DIM, wuq_pair_ref[:, LANES:2 * LANES], 0.0)
        return (jnp.concatenate([nope, rope], axis=1).astype(BF16), wukv_ref[...].astype(BF16))

    def prep_rows(wuq, wukv, q_out, k_out, v_out, r):
        rows = slice(r * tq, (r + 1) * tq)
        q = jnp.dot(qn_ref[rows, :], wuq, preferred_element_type=F32)
        kv = jnp.dot(kvn_ref[rows, :], wukv, preferred_element_type=F32)
        q_nope, q_rope = q[:, 0:LANES], q[:, LANES:HEAD_PAD]
        ss = jnp.sum(q_nope * q_nope + q_rope * q_rope, axis=-1, keepdims=True)
        rq = lax.rsqrt(ss * inv_d + EPS) * scale
        q_out[rows, 0:LANES] = (q_nope * rq * qg_nope).astype(BF16)
        q_out[rows, LANES:HEAD_PAD] = (rope(q_rope * qg_rope, rows) * rq).astype(BF16)
        k_nope = kv[:, 0:LANES]
        ssk = jnp.sum(k_nope * k_nope, axis=-1, keepdims=True) + kss_ref[rows, :]
        rk = lax.rsqrt(ssk * inv_d + EPS)
        k_out[rows, 0:LANES] = (k_nope * rk * kg_nope).astype(BF16)
        k_out[rows, LANES:HEAD_PAD] = (krope_ref[rows, :] * rk).astype(BF16)
        v_out[rows, 0:LANES] = kv[:, LANES:HEAD_PAD].astype(BF16)

    def scores(slot, i):
        q_in, k_in, _ = slot
        return lax.dot_general(q_in[i * tq:(i + 1) * tq, :], k_in[...], (((1,), (1,)), ((), ())),
                               preferred_element_type=F32)

    def finish(sc, slot, col, i):
        m = jnp.max(sc, axis=-1, keepdims=True)
        p = jnp.exp2((sc - m).astype(BF16))
        o = jnp.dot(p, slot[2][...], preferred_element_type=F32)
        o_ref[i * tq:(i + 1) * tq, col * LANES:(col + 1) * LANES] = (
            o[:, :V_HEAD_DIM] / o[:, V_HEAD_DIM:V_HEAD_DIM + 1]).astype(BF16)

    @pl.when(pl.program_id(1) == 0)
    def _():
        for r in range(n_chunks):
            rows = slice(r * tq, (r + 1) * tq)
            qa = z_ref[rows, 0:Q_LORA_RANK].astype(F32)
            kva = z_ref[rows, Q_LORA_RANK:Q_LORA_RANK + KV_LORA_RANK].astype(F32)
            lane = lax.broadcasted_iota(jnp.int32, (tq, LANES), 1)
            kpe = jnp.where(lane < QK_ROPE_DIM,
                            z_ref[rows, HEAD_COLS - QK_ROPE_DIM:HEAD_COLS - QK_ROPE_DIM + LANES
                                  ].astype(F32), 0.0)
            qn_ref[rows, :] = (_rms(qa, Q_LORA_RANK) * qlg_ref[...]).astype(BF16)
            kvn_ref[rows, :] = (_rms(kva, KV_LORA_RANK) * kvlg_ref[...]).astype(BF16)
            krope_ref[rows, :] = rope(kpe * kg_rope, rows)
            kss_ref[rows, :] = jnp.sum(kpe * kpe, axis=-1, keepdims=True)
            ones_col = jnp.where(lane == 0, 1.0, 0.0).astype(BF16)
            va_ref[rows, LANES:HEAD_PAD] = ones_col
            vb_ref[rows, LANES:HEAD_PAD] = ones_col
        w_first = head_weights(wuq0_ref, wukv0_ref, odd=False)
        for r in range(n_chunks):
            prep_rows(*w_first, *slot_a, r)

    slots = (slot_a, slot_b)
    total = hps * n_chunks
    sc_next = scores(slot_a, 0)
    for t in range(total):
        hd, i = divmod(t, n_chunks)
        sc_cur = sc_next
        if i == 0:
            w_head = head_weights(*w_following[2 * hd:2 * hd + 2], odd=(hd + 1) % 2 == 1)
        prep_rows(*w_head, *slots[(hd + 1) % 2], i)
        if t + 1 < total:
            hd2, i2 = divmod(t + 1, n_chunks)
            sc_next = scores(slots[hd2 % 2], i2)
        finish(sc_cur, slots[hd % 2], hd, i)


def _mla_attention(z, rot, q_lat_g, kv_lat_g, w_uq, w_ukv, qg_p, kg_p, b, s, tq, hps):
    assert hps % 2 == 0 and N_HEADS % hps == 0
    const = lambda shape: pl.BlockSpec(shape, lambda bi, g: (0,) * len(shape))
    wuq_pair = lambda f: pl.BlockSpec((Q_LORA_RANK, 2 * QK_HEAD_DIM), lambda bi, g: (0, f(g) // 2))
    wukv_head = lambda f: pl.BlockSpec((KV_LORA_RANK, HEAD_PAD), lambda bi, g: (0, f(g)))
    following = lambda k: (lambda g: jnp.minimum(hps * g + k, N_HEADS - 2 + k % 2))
    following_specs = []
    for k in range(1, hps + 1):
        following_specs += [wuq_pair(following(k)), wukv_head(following(k))]
    rows = lambda w, dt: pltpu.VMEM((s, w), dt)
    return pl.pallas_call(
        functools.partial(_mla_kernel, tq=tq, hps=hps),
        grid=(b, N_HEADS // hps),
        in_specs=[
            pl.BlockSpec((s, SEG), lambda bi, g: (bi, 0)),
            pl.BlockSpec((s, 3 * LANES), lambda bi, g: (bi, 0)),
            const((1, Q_LORA_RANK)), const((1, KV_LORA_RANK)),
            const((1, HEAD_PAD)), const((1, HEAD_PAD)),
            wuq_pair(lambda g: 0), wukv_head(lambda g: 0),
        ] + following_specs,
        out_specs=pl.BlockSpec((s, hps * V_HEAD_DIM), lambda bi, g: (bi, g)),
        out_shape=jax.ShapeDtypeStruct((b * s, N_HEADS * V_HEAD_DIM), BF16),
        scratch_shapes=[
            rows(Q_LORA_RANK, BF16), rows(KV_LORA_RANK, BF16),
            rows(LANES, F32), rows(1, F32),
            rows(HEAD_PAD, BF16), rows(HEAD_PAD, BF16), rows(HEAD_PAD, BF16),
            rows(HEAD_PAD, BF16), rows(HEAD_PAD, BF16), rows(HEAD_PAD, BF16),
        ],
        compiler_params=pltpu.CompilerParams(
            dimension_semantics=("arbitrary", "arbitrary"), vmem_limit_bytes=VMEM_LIMIT),
        name="mla_attention",
    )(z, rot, q_lat_g, kv_lat_g, qg_p, kg_p, *([w_uq, w_ukv] * (hps + 1)))


def _mix_kernel(attn_ref, ga_ref, conv_ref, mrg_ref, prev_ref, next_ref,
                cw_ref, cbias_ref, wba_hbm, wbc_hbm, o_ref,
                u_ref, wba_ref, wbc_ref, stage_ref, sem_ref, *, tm, seq):
    i = pl.program_id(0)

    @pl.when(i == 0)
    def _():
        _load_weights_bf16(((wba_hbm, wba_ref), (wbc_hbm, wbc_ref)), stage_ref, sem_ref)

    seg = lambda ref, c: ref[:, c * SEG:(c + 1) * SEG]

    a = attn_ref[...] * ga_ref[...]
    halves = tuple(slice(c * SEG, (c + 1) * SEG) for c in range(2))
    y_attn = [jnp.dot(a, wba_ref[:, cols], preferred_element_type=F32) for cols in halves]

    u = seg(conv_ref, 1).astype(F32) * seg(conv_ref, 2).astype(F32)
    first = (i * tm) % seq == 0
    last = ((i + 1) * tm) % seq == 0
    prev = (prev_ref[HALO_ROWS - 1:HALO_ROWS, 0:SEG].astype(F32)
            * prev_ref[HALO_ROWS - 1:HALO_ROWS, SEG:2 * SEG].astype(F32))
    nxt = next_ref[0:1, 0:SEG].astype(F32) * next_ref[0:1, SEG:2 * SEG].astype(F32)
    u_ref[7:8, :] = jnp.where(first, 0.0, prev)
    u_ref[8:8 + tm, :] = u
    u_ref[8 + tm:9 + tm, :] = jnp.where(last, 0.0, nxt)
    conv = (u_ref[7:7 + tm, :] * cw_ref[0:1, :] + u * cw_ref[1:2, :]
            + u_ref[9:9 + tm, :] * cw_ref[2:3, :] + cbias_ref[...])
    cv = (seg(conv_ref, 0).astype(F32) * conv * seg(conv_ref, 3).astype(F32)).astype(BF16)

    for c in range(2):
        y_conv = jnp.dot(cv, wbc_ref[:, halves[c]], preferred_element_type=F32)
        o_ref[:, halves[c]] = (seg(mrg_ref, c).astype(F32) * y_attn[c]
                               + seg(mrg_ref, 2 + c).astype(F32) * y_conv).astype(BF16)


def _mix(attn, z, conv_w, conv_b, wba, wbc, seq, tm):
    t, d = attn.shape[0], wba.shape[1]

    def window(rows, first_seg, n_seg, row_start):
        return pl.BlockSpec(
            (pl.Element(rows), pl.Element(n_seg * SEG)),
            lambda i: (pl.multiple_of(row_start(i), HALO_ROWS), first_seg * SEG))

    tile = lambda i: i * tm
    prev_rows = lambda i: jnp.maximum(i * tm - HALO_ROWS, 0)
    next_rows = lambda i: jnp.minimum((i + 1) * tm, t - HALO_ROWS)
    resident = lambda shape: pl.BlockSpec(shape, lambda i: (0,) * len(shape),
                                          pipeline_mode=pl.Buffered(1))
    return pl.pallas_call(
        functools.partial(_mix_kernel, tm=tm, seq=seq),
        grid=(t // tm,),
        in_specs=[
            pl.BlockSpec((tm, attn.shape[1]), lambda i: (i, 0)),
            window(tm, 1, 2, tile), window(tm, 3, 4, tile), window(tm, 7, 4, tile),
            window(HALO_ROWS, 4, 2, prev_rows), window(HALO_ROWS, 4, 2, next_rows),
            resident(conv_w.shape), resident(conv_b.shape),
            pl.BlockSpec(memory_space=pl.ANY), pl.BlockSpec(memory_space=pl.ANY),
        ],
        out_specs=pl.BlockSpec((tm, d), lambda i: (i, 0)),
        out_shape=jax.ShapeDtypeStruct((t, d), BF16),
        scratch_shapes=[pltpu.VMEM((tm + 16, SEG), F32), pltpu.VMEM(wba.shape, BF16),
                        pltpu.VMEM(wbc.shape, BF16)] + _weight_staging(d),
        compiler_params=pltpu.CompilerParams(
            dimension_semantics=("arbitrary",), vmem_limit_bytes=VMEM_LIMIT),
        name="mix",
    )(attn, z, z, z, z, z, conv_w, conv_b, wba, wbc)


def _out_ple_kernel(x_ref, m_ref, p_ref, g_ref, wout_hbm, wg_hbm, wp_hbm, o_ref,
                    x1_ref, wout_ref, wg_ref, wp_ref, stage_ref, sem_ref):
    @pl.when(pl.program_id(0) == 0)
    def _():
        _load_weights_bf16(((wout_hbm, wout_ref), (wg_hbm, wg_ref), (wp_hbm, wp_ref)),
                           stage_ref, sem_ref)

    x1_ref[...] = x_ref[...] + jnp.dot(m_ref[...], wout_ref[...], preferred_element_type=F32)
    pb = p_ref[...].astype(BF16)
    x1 = x1_ref[...]
    h = (_rms(x1, x1.shape[-1]) * g_ref[...]).astype(BF16)
    for c in range(x1.shape[-1] // SEG):
        cols = slice(c * SEG, (c + 1) * SEG)
        proj = jnp.dot(pb, wp_ref[:, cols], preferred_element_type=F32)
        gate = _sigmoid(jnp.dot(h, wg_ref[:, cols], preferred_element_type=F32))
        o_ref[:, cols] = x1_ref[:, cols] + gate * proj


def _out_ple(x2, merged, p2, ple_g, wout, wg, wp, tm):
    t, d = x2.shape
    resident = lambda shape: pl.BlockSpec(shape, lambda i: (0,) * len(shape),
                                          pipeline_mode=pl.Buffered(1))
    in_hbm = pl.BlockSpec(memory_space=pl.ANY)
    return pl.pallas_call(
        _out_ple_kernel,
        grid=(t // tm,),
        in_specs=[
            pl.BlockSpec((tm, d), lambda i: (i, 0)),
            pl.BlockSpec((tm, d), lambda i: (i, 0)),
            pl.BlockSpec((tm, p2.shape[1]), lambda i: (i, 0)),
            resident((1, d)), in_hbm, in_hbm, in_hbm,
        ],
        out_specs=pl.BlockSpec((tm, d), lambda i: (i, 0)),
        out_shape=jax.ShapeDtypeStruct((t, d), F32),
        scratch_shapes=[pltpu.VMEM((tm, d), F32),
                        pltpu.VMEM(wout.shape, BF16), pltpu.VMEM(wg.shape, BF16),
                        pltpu.VMEM(wp.shape, BF16)] + _weight_staging(d),
        compiler_params=pltpu.CompilerParams(
            dimension_semantics=("arbitrary",), vmem_limit_bytes=VMEM_LIMIT),
        name="out_ple",
    )(x2, merged, p2, ple_g, wout, wg, wp)


def _layer(x2, p2, pos, freq, b, s, norm_g, w_in, q_lat_g, kv_lat_g, w_uq, w_ukv,
           q_norm_g, k_norm_g, conv_w, conv_b, w_branch_attn, w_branch_conv,
           w_out, ple_norm_g, w_ple_gate, w_ple_proj):
    d = x2.shape[1]
    pad_g = lambda g: jnp.pad(g, (0, HEAD_PAD - QK_HEAD_DIM)).reshape(1, HEAD_PAD)

    z, rot = _in_proj(x2, norm_g.reshape(1, d), jnp.swapaxes(w_in, 0, 1), pos, freq,
                      tm=IN_PROJ_TM)
    attn = _mla_attention(z, rot, q_lat_g.reshape(1, -1), kv_lat_g.reshape(1, -1),
                          w_uq, w_ukv, pad_g(q_norm_g), pad_g(k_norm_g),
                          b, s, tq=512, hps=2)
    merged = _mix(attn, z, conv_w, conv_b.reshape(1, -1), w_branch_attn, w_branch_conv,
                  seq=s, tm=512)
    return _out_ple(x2, merged, p2, ple_norm_g.reshape(1, d), w_out, w_ple_gate, w_ple_proj,
                    tm=512)


def kernel(x, p, positions, norm_g, w_in, q_lat_g, kv_lat_g, w_uq, w_ukv, q_norm_g, k_norm_g, conv_w, conv_b, w_branch_attn, w_branch_conv, w_out, ple_norm_g, w_ple_gate, w_ple_proj):
    b, s, d = x.shape
    depth = p.shape[0]
    x2 = x.reshape(b * s, d)
    half = QK_ROPE_DIM // 2
    pos = positions.reshape(b * s // IN_PROJ_TM, ROT_GROUPS, IN_PROJ_TM // ROT_GROUPS)
    pos = jnp.repeat(jnp.swapaxes(pos, 1, 2), half, axis=2).reshape(b * s // ROT_GROUPS, LANES)
    inv_freq = 1.0 / (ROPE_THETA ** (jnp.arange(0, QK_ROPE_DIM, 2, dtype=F32) / QK_ROPE_DIM))
    freq = jnp.tile(inv_freq, ROT_GROUPS).reshape(1, LANES)
    for i in range(depth):
        x2 = _layer(x2, p[i].reshape(b * s, -1), pos, freq, b, s, norm_g[i], w_in[i],
                    q_lat_g[i], kv_lat_g[i], w_uq[i], w_ukv[i], q_norm_g[i], k_norm_g[i],
                    conv_w[i], conv_b[i], w_branch_attn[i], w_branch_conv[i], w_out[i],
                    ple_norm_g[i], w_ple_gate[i], w_ple_proj[i])
    return x2.reshape(b, s, d)
```

```python
import functools

import jax
import jax.numpy as jnp
from jax import lax
from jax.experimental import pallas as pl
from jax.experimental.pallas import tpu as pltpu

N_HEADS = 16
QK_NOPE_DIM = 128
QK_ROPE_DIM = 64
QK_HEAD_DIM = QK_NOPE_DIM + QK_ROPE_DIM
V_HEAD_DIM = 128
Q_LORA_RANK = 512
KV_LORA_RANK = 256
ROPE_THETA = 10000.0
EPS = 1e-6
LOG2E = 1.4426950408889634

LANES = 128
HEAD_PAD = 2 * LANES
SEG = 1024
HEAD_COLS = Q_LORA_RANK + KV_LORA_RANK + QK_ROPE_DIM
HEAD_COLS_ALIGN = 64
assert HEAD_COLS % HEAD_COLS_ALIGN == 0 and SEG % HEAD_COLS_ALIGN == 0
HALO_ROWS = 16
VMEM_LIMIT = 56 * 1024 * 1024

BF16 = jnp.bfloat16
F32 = jnp.float32


def _rms(v, axis_size):
    return v * lax.rsqrt(jnp.sum(v * v, axis=-1, keepdims=True) * (1.0 / axis_size) + EPS)


def _sigmoid(v):
    return 0.5 * jnp.tanh(0.5 * v) + 0.5


WEIGHT_CHUNK_ROWS = 256


def _weight_staging(cols):
    return [pltpu.VMEM((2, WEIGHT_CHUNK_ROWS, cols), F32), pltpu.SemaphoreType.DMA((2,))]


def _load_weights_bf16(pairs, stage_ref, sem_ref):
    chunks = []
    for w_hbm, w_vmem in pairs:
        assert w_hbm.shape[0] % WEIGHT_CHUNK_ROWS == 0 and w_hbm.shape[1] == stage_ref.shape[2]
        for r in range(0, w_hbm.shape[0], WEIGHT_CHUNK_ROWS):
            chunks.append((w_hbm, w_vmem, r))

    def copy(c):
        w_hbm, _, r = chunks[c]
        return pltpu.make_async_copy(w_hbm.at[pl.ds(r, WEIGHT_CHUNK_ROWS)],
                                     stage_ref.at[c % 2], sem_ref.at[c % 2])

    copy(0).start()
    for c, (_, w_vmem, r) in enumerate(chunks):
        if c + 1 < len(chunks):
            copy(c + 1).start()
        copy(c).wait()
        w_vmem[r:r + WEIGHT_CHUNK_ROWS, :] = stage_ref[c % 2].astype(BF16)


Z_TILE_ACTS = ("plain", "silu", "silu", "plain", "plain", "plain", "silu",
               "sigmoid", "sigmoid", "sigmoid", "sigmoid")
ROT_TILE = 3
ROT_GROUPS = LANES // (QK_ROPE_DIM // 2)
IN_PROJ_TM = 1024
W_RING = 3
IN_PROJ_VMEM_LIMIT = 60 * 1024 * 1024


def _in_proj_kernel(x_ref, g_ref, wt_hbm, pos_ref, freq_ref, z_ref, rot_ref, h_ref,
                    ring_ref, sem_ref):
    j = pl.program_id(1)

    n_j = pl.num_programs(1)
    step = pl.program_id(0) * n_j + j
    total = pl.num_programs(0) * n_j

    def tile_copy(s):
        row0 = pl.multiple_of(
            jnp.maximum(lax.rem(s, n_j) * SEG - (SEG - HEAD_COLS), 0), HEAD_COLS_ALIGN)
        slot = lax.rem(s, W_RING)
        return pltpu.make_async_copy(wt_hbm.at[pl.ds(row0, SEG)], ring_ref.at[slot],
                                     sem_ref.at[slot])

    @pl.when(step == 0)
    def _():
        tile_copy(step).start()
        tile_copy(step + 1).start()

    @pl.when(step + 2 < total)
    def _():
        tile_copy(step + 2).start()

    tile_copy(step).wait()
    wt_ref = ring_ref.at[lax.rem(step, W_RING)]

    def rotary_tables():
        half = QK_ROPE_DIM // 2
        rq = x_ref.shape[0] // ROT_GROUPS
        lane = lax.broadcasted_iota(jnp.int32, (rq, LANES), 1)
        ang = pos_ref[...].astype(F32) * freq_ref[...]
        cosv, sinv = jnp.cos(ang), jnp.sin(ang)
        for q in range(ROT_GROUPS):
            rows = slice(q * rq, (q + 1) * rq)
            to_x1 = (lambda v: v) if q == 0 else (
                lambda v: pltpu.roll(v, LANES - q * half, axis=1))
            to_x2 = lambda v: pltpu.roll(v, (LANES - q * half + half) % LANES, axis=1)
            rot_ref[rows, 0:LANES] = jnp.where(lane < half, to_x1(cosv), to_x2(cosv))
            rot_ref[rows, LANES:2 * LANES] = jnp.where(lane < half, -to_x1(sinv), 0.0)
            rot_ref[rows, 2 * LANES:3 * LANES] = jnp.where(
                (lane >= half) & (lane < QK_ROPE_DIM), to_x2(sinv), 0.0)

    def is_act(*acts):
        return functools.reduce(jnp.logical_or,
                                [j == tile for tile, a in enumerate(Z_TILE_ACTS) if a in acts])

    def matmul(h=None):
        return lax.dot_general(h_ref[...] if h is None else h, wt_ref[...].astype(BF16),
                               (((1,), (1,)), ((), ())), preferred_element_type=F32)

    assert Z_TILE_ACTS[0] == Z_TILE_ACTS[ROT_TILE] == "plain" and ROT_TILE != 0

    @pl.when(j == 0)
    def _():
        x = x_ref[...]
        h = (_rms(x, x.shape[-1]) * g_ref[...]).astype(BF16)
        h_ref[...] = h
        z_ref[...] = matmul(h).astype(BF16)

    @pl.when(j == ROT_TILE)
    def _():
        z_ref[...] = matmul().astype(BF16)
        rotary_tables()

    @pl.when(is_act("plain") & (j != 0) & (j != ROT_TILE))
    def _():
        z_ref[...] = matmul().astype(BF16)

    @pl.when(is_act("silu", "sigmoid"))
    def _():
        hv = 0.5 * matmul()
        a = jnp.where(is_act("silu"), hv, 0.5)
        z_ref[...] = (a * jnp.tanh(hv) + a).astype(BF16)


def _in_proj(x2, norm_g, w_t, pos, freq, tm):
    t, d = x2.shape
    n_tiles = len(Z_TILE_ACTS)
    assert w_t.shape == (HEAD_COLS + (n_tiles - 1) * SEG, d)
    return pl.pallas_call(
        _in_proj_kernel,
        grid=(t // tm, n_tiles),
        in_specs=[
            pl.BlockSpec((tm, d), lambda i, j: (i, 0)),
            pl.BlockSpec((1, d), lambda i, j: (0, 0)),
            pl.BlockSpec(memory_space=pl.ANY),
            pl.BlockSpec((tm // ROT_GROUPS, LANES), lambda i, j: (i, 0)),
            pl.BlockSpec((1, LANES), lambda i, j: (0, 0)),
        ],
        out_specs=[pl.BlockSpec((tm, SEG), lambda i, j: (i, j)),
                   pl.BlockSpec((tm, 3 * LANES), lambda i, j: (i, 0))],
        out_shape=[jax.ShapeDtypeStruct((t, n_tiles * SEG), BF16),
                   jax.ShapeDtypeStruct((t, 3 * LANES), F32)],
        scratch_shapes=[pltpu.VMEM((tm, d), BF16), pltpu.VMEM((W_RING, SEG, d), F32),
                        pltpu.SemaphoreType.DMA((W_RING,))],
        compiler_params=pltpu.CompilerParams(
            dimension_semantics=("arbitrary", "arbitrary"),
            vmem_limit_bytes=IN_PROJ_VMEM_LIMIT),
        name="in_proj",
    )(x2, norm_g, w_t, pos, freq)


def _mla_kernel(*refs, tq, hps):
    z_ref, rot_ref, qlg_ref, kvlg_ref, qg_ref, kg_ref, wuq0_ref, wukv0_ref = refs[:8]
    w_following = refs[8:8 + 2 * hps]
    o_ref = refs[8 + 2 * hps]
    (qn_ref, kvn_ref, krope_ref, kss_ref,
     qa_ref, ka_ref, va_ref, qb_ref, kb_ref, vb_ref) = refs[9 + 2 * hps:]
    s_len = z_ref.shape[0]
    n_chunks = s_len // tq
    slot_a = (qa_ref, ka_ref, va_ref)
    slot_b = (qb_ref, kb_ref, vb_ref)
    half = QK_ROPE_DIM // 2
    scale = QK_HEAD_DIM ** -0.5 * LOG2E
    inv_d = 1.0 / QK_HEAD_DIM
    qg_nope, qg_rope = qg_ref[:, 0:LANES], qg_ref[:, LANES:HEAD_PAD]
    kg_nope, kg_rope = kg_ref[:, 0:LANES], kg_ref[:, LANES:HEAD_PAD]

    def rope(u, rows):
        return (u * rot_ref[rows, 0:LANES]
                + pltpu.roll(u, LANES - half, axis=1) * rot_ref[rows, LANES:2 * LANES]
                + pltpu.roll(u, half, axis=1) * rot_ref[rows, 2 * LANES:3 * LANES])

    def head_weights(wuq_pair_ref, wukv_ref, odd):
        lane = lax.broadcasted_iota(jnp.int32, (Q_LORA_RANK, LANES), 1)
        if odd:
            mid = pltpu.roll(wuq_pair_ref[:, LANES:2 * LANES], QK_ROPE_DIM, axis=1)
            end = pltpu.roll(wuq_pair_ref[:, 2 * LANES:3 * LANES], QK_ROPE_DIM, axis=1)
            nope = jnp.where(lane < QK_ROPE_DIM, mid, end)
            rope = jnp.where(lane < QK_ROPE_DIM, end, 0.0)
        else:
            nope = wuq_pair_ref[:, 0:LANES]
            rope = jnp.where(lane < QK_ROPE_DIM, wuq_pair_ref[:, LANES:2 * LANES], 0.0)
        return (jnp.concatenate([nope, rope], axis=1).astype(BF16), wukv_ref[...].astype(BF16))

    def prep_rows(wuq, wukv, q_out, k_out, v_out, r):
        rows = slice(r * tq, (r + 1) * tq)
        q = jnp.dot(qn_ref[rows, :], wuq, preferred_element_type=F32)
        kv = jnp.dot(kvn_ref[rows, :], wukv, preferred_element_type=F32)
        q_nope, q_rope = q[:, 0:LANES], q[:, LANES:HEAD_PAD]
        ss = jnp.sum(q_nope * q_nope + q_rope * q_rope, axis=-1, keepdims=True)
        rq = lax.rsqrt(ss * inv_d + EPS) * scale
        q_out[rows, 0:LANES] = (q_nope * rq * qg_nope).astype(BF16)
        q_out[rows, LANES:HEAD_PAD] = (rope(q_rope * qg_rope, rows) * rq).astype(BF16)
        k_nope = kv[:, 0:LANES]
        ssk = jnp.sum(k_nope * k_nope, axis=-1, keepdims=True) + kss_ref[rows, :]
        rk = lax.rsqrt(ssk * inv_d + EPS)
        k_out[rows, 0:LANES] = (k_nope * rk * kg_nope).astype(BF16)
        k_out[rows, LANES:HEAD_PAD] = (krope_ref[rows, :] * rk).astype(BF16)
        v_out[rows, 0:LANES] = kv[:, LANES:HEAD_PAD].astype(BF16)

    def scores(slot, i):
        q_in, k_in, _ = slot
        return lax.dot_general(q_in[i * tq:(i + 1) * tq, :], k_in[...], (((1,), (1,)), ((), ())),
                               preferred_element_type=F32)

    def finish(sc, slot, col, i):
        m = jnp.max(sc, axis=-1, keepdims=True)
        p = jnp.exp2(sc - m).astype(BF16)
        o = jnp.dot(p, slot[2][...], preferred_element_type=F32)
        o_ref[i * tq:(i + 1) * tq, col * LANES:(col + 1) * LANES] = (
            o[:, :V_HEAD_DIM] / o[:, V_HEAD_DIM:V_HEAD_DIM + 1]).astype(BF16)

    @pl.when(pl.program_id(1) == 0)
    def _():
        for r in range(n_chunks):
            rows = slice(r * tq, (r + 1) * tq)
            qa = z_ref[rows, 0:Q_LORA_RANK].astype(F32)
            kva = z_ref[rows, Q_LORA_RANK:Q_LORA_RANK + KV_LORA_RANK].astype(F32)
            lane = lax.broadcasted_iota(jnp.int32, (tq, LANES), 1)
            kpe = jnp.where(lane < QK_ROPE_DIM,
                            z_ref[rows, HEAD_COLS - QK_ROPE_DIM:HEAD_COLS - QK_ROPE_DIM + LANES
                                  ].astype(F32), 0.0)
            qn_ref[rows, :] = (_rms(qa, Q_LORA_RANK) * qlg_ref[...]).astype(BF16)
            kvn_ref[rows, :] = (_rms(kva, KV_LORA_RANK) * kvlg_ref[...]).astype(BF16)
            krope_ref[rows, :] = rope(kpe * kg_rope, rows)
            kss_ref[rows, :] = jnp.sum(kpe * kpe, axis=-1, keepdims=True)
            ones_col = jnp.where(lane == 0, 1.0, 0.0).astype(BF16)
            va_ref[rows, LANES:HEAD_PAD] = ones_col
            vb_ref[rows, LANES:HEAD_PAD] = ones_col
        w_first = head_weights(wuq0_ref, wukv0_ref, odd=False)
        for r in range(n_chunks):
            prep_rows(*w_first, *slot_a, r)

    slots = (slot_a, slot_b)
    total = hps * n_chunks
    sc_next = scores(slot_a, 0)
    for t in range(total):
        hd, i = divmod(t, n_chunks)
        sc_cur = sc_next
        if i == 0:
            w_head = head_weights(*w_following[2 * hd:2 * hd + 2], odd=(hd + 1) % 2 == 1)
        prep_rows(*w_head, *slots[(hd + 1) % 2], i)
        if t + 1 < total:
            hd2, i2 = divmod(t + 1, n_chunks)
            sc_next = scores(slots[hd2 % 2], i2)
        finish(sc_cur, slots[hd % 2], hd, i)


def _mla_attention(z, rot, q_lat_g, kv_lat_g, w_uq, w_ukv, qg_p, kg_p, b, s, tq, hps):
    assert hps % 2 == 0 and N_HEADS % hps == 0
    const = lambda shape: pl.BlockSpec(shape, lambda bi, g: (0,) * len(shape))
    wuq_pair = lambda f: pl.BlockSpec((Q_LORA_RANK, 2 * QK_HEAD_DIM), lambda bi, g: (0, f(g) // 2))
    wukv_head = lambda f: pl.BlockSpec((KV_LORA_RANK, HEAD_PAD), lambda bi, g: (0, f(g)))
    following = lambda k: (lambda g: jnp.minimum(hps * g + k, N_HEADS - 2 + k % 2))
    following_specs = []
    for k in range(1, hps + 1):
        following_specs += [wuq_pair(following(k)), wukv_head(following(k))]
    rows = lambda w, dt: pltpu.VMEM((s, w), dt)
    return pl.pallas_call(
        functools.partial(_mla_kernel, tq=tq, hps=hps),
        grid=(b, N_HEADS // hps),
        in_specs=[
            pl.BlockSpec((s, SEG), lambda bi, g: (bi, 0)),
            pl.BlockSpec((s, 3 * LANES), lambda bi, g: (bi, 0)),
            const((1, Q_LORA_RANK)), const((1, KV_LORA_RANK)),
            const((1, HEAD_PAD)), const((1, HEAD_PAD)),
            wuq_pair(lambda g: 0), wukv_head(lambda g: 0),
        ] + following_specs,
        out_specs=pl.BlockSpec((s, hps * V_HEAD_DIM), lambda bi, g: (bi, g)),
        out_shape=jax.ShapeDtypeStruct((b * s, N_HEADS * V_HEAD_DIM), BF16),
        scratch_shapes=[
            rows(Q_LORA_RANK, BF16), rows(KV_LORA_RANK, BF16),
            rows(LANES, F32), rows(1, F32),
            rows(HEAD_PAD, BF16), rows(HEAD_PAD, BF16), rows(HEAD_PAD, BF16),
            rows(HEAD_PAD, BF16), rows(HEAD_PAD, BF16), rows(HEAD_PAD, BF16),
        ],
        compiler_params=pltpu.CompilerParams(
            dimension_semantics=("arbitrary", "arbitrary"), vmem_limit_bytes=VMEM_LIMIT),
        name="mla_attention",
    )(z, rot, q_lat_g, kv_lat_g, qg_p, kg_p, *([w_uq, w_ukv] * (hps + 1)))


def _mix_kernel(attn_ref, ga_ref, conv_ref, mrg_ref, prev_ref, next_ref,
                cw_ref, cbias_ref, wba_hbm, wbc_hbm, o_ref,
                u_ref, wba_ref, wbc_ref, stage_ref, sem_ref, *, tm, seq):
    i = pl.program_id(0)

    @pl.when(i == 0)
    def _():
        _load_weights_bf16(((wba_hbm, wba_ref), (wbc_hbm, wbc_ref)), stage_ref, sem_ref)

    seg = lambda ref, c: ref[:, c * SEG:(c + 1) * SEG]

    a = attn_ref[...] * ga_ref[...]
    halves = tuple(slice(c * SEG, (c + 1) * SEG) for c in range(2))
    y_attn = [jnp.dot(a, wba_ref[:, cols], preferred_element_type=F32) for cols in halves]

    u = seg(conv_ref, 1).astype(F32) * seg(conv_ref, 2).astype(F32)
    first = (i * tm) % seq == 0
    last = ((i + 1) * tm) % seq == 0
    prev = (prev_ref[HALO_ROWS - 1:HALO_ROWS, 0:SEG].astype(F32)
            * prev_ref[HALO_ROWS - 1:HALO_ROWS, SEG:2 * SEG].astype(F32))
    nxt = next_ref[0:1, 0:SEG].astype(F32) * next_ref[0:1, SEG:2 * SEG].astype(F32)
    u_ref[7:8, :] = jnp.where(first, 0.0, prev)
    u_ref[8:8 + tm, :] = u
    u_ref[8 + tm:9 + tm, :] = jnp.where(last, 0.0, nxt)
    conv = (u_ref[7:7 + tm, :] * cw_ref[0:1, :] + u * cw_ref[1:2, :]
            + u_ref[9:9 + tm, :] * cw_ref[2:3, :] + cbias_ref[...])
    cv = (seg(conv_ref, 0).astype(F32) * conv * seg(conv_ref, 3).astype(F32)).astype(BF16)

    for c in range(2):
        y_conv = jnp.dot(cv, wbc_ref[:, halves[c]], preferred_element_type=F32)
        o_ref[:, halves[c]] = (seg(mrg_ref, c).astype(F32) * y_attn[c]
                               + seg(mrg_ref, 2 + c).astype(F32) * y_conv).astype(BF16)


def _mix(attn, z, conv_w, conv_b, wba, wbc, seq, tm):
    t, d = attn.shape[0], wba.shape[1]

    def window(rows, first_seg, n_seg, row_start):
        return pl.BlockSpec(
            (pl.Element(rows), pl.Element(n_seg * SEG)),
            lambda i: (pl.multiple_of(row_start(i), HALO_ROWS), first_seg * SEG))

    tile = lambda i: i * tm
    prev_rows = lambda i: jnp.maximum(i * tm - HALO_ROWS, 0)
    next_rows = lambda i: jnp.minimum((i + 1) * tm, t - HALO_ROWS)
    resident = lambda shape: pl.BlockSpec(shape, lambda i: (0,) * len(shape),
                                          pipeline_mode=pl.Buffered(1))
    return pl.pallas_call(
        functools.partial(_mix_kernel, tm=tm, seq=seq),
        grid=(t // tm,),
        in_specs=[
            pl.BlockSpec((tm, attn.shape[1]), lambda i: (i, 0)),
            window(tm, 1, 2, tile), window(tm, 3, 4, tile), window(tm, 7, 4, tile),
            window(HALO_ROWS, 4, 2, prev_rows), window(HALO_ROWS, 4, 2, next_rows),
            resident(conv_w.shape), resident(conv_b.shape),
            pl.BlockSpec(memory_space=pl.ANY), pl.BlockSpec(memory_space=pl.ANY),
        ],
        out_specs=pl.BlockSpec((tm, d), lambda i: (i, 0)),
        out_shape=jax.ShapeDtypeStruct((t, d), BF16),
        scratch_shapes=[pltpu.VMEM((tm + 16, SEG), F32), pltpu.VMEM(wba.shape, BF16),
                        pltpu.VMEM(wbc.shape, BF16)] + _weight_staging(d),
        compiler_params=pltpu.CompilerParams(
            dimension_semantics=("arbitrary",), vmem_limit_bytes=VMEM_LIMIT),
        name="mix",
    )(attn, z, z, z, z, z, conv_w, conv_b, wba, wbc)


def _out_ple_kernel(x_ref, m_ref, p_ref, g_ref, wout_hbm, wg_hbm, wp_hbm, o_ref,
                    x1_ref, wout_ref, wg_ref, wp_ref, stage_ref, sem_ref):
    @pl.when(pl.program_id(0) == 0)
    def _():
        _load_weights_bf16(((wout_hbm, wout_ref), (wg_hbm, wg_ref), (wp_hbm, wp_ref)),
                           stage_ref, sem_ref)

    x1_ref[...] = x_ref[...] + jnp.dot(m_ref[...], wout_ref[...], preferred_element_type=F32)
    pb = p_ref[...].astype(BF16)
    x1 = x1_ref[...]
    h = (_rms(x1, x1.shape[-1]) * g_ref[...]).astype(BF16)
    for c in range(x1.shape[-1] // SEG):
        cols = slice(c * SEG, (c + 1) * SEG)
        proj = jnp.dot(pb, wp_ref[:, cols], preferred_element_type=F32)
        gate = _sigmoid(jnp.dot(h, wg_ref[:, cols], preferred_element_type=F32))
        o_ref[:, cols] = x1_ref[:, cols] + gate * proj


def _out_ple(x2, merged, p2, ple_g, wout, wg, wp, tm):
    t, d = x2.shape
    resident = lambda shape: pl.BlockSpec(shape, lambda i: (0,) * len(shape),
                                          pipeline_mode=pl.Buffered(1))
    in_hbm = pl.BlockSpec(memory_space=pl.ANY)
    return pl.pallas_call(
        _out_ple_kernel,
        grid=(t // tm,),
        in_specs=[
            pl.BlockSpec((tm, d), lambda i: (i, 0)),
            pl.BlockSpec((tm, d), lambda i: (i, 0)),
            pl.BlockSpec((tm, p2.shape[1]), lambda i: (i, 0)),
            resident((1, d)), in_hbm, in_hbm, in_hbm,
        ],
        out_specs=pl.BlockSpec((tm, d), lambda i: (i, 0)),
        out_shape=jax.ShapeDtypeStruct((t, d), F32),
        scratch_shapes=[pltpu.VMEM((tm, d), F32),
                        pltpu.VMEM(wout.shape, BF16), pltpu.VMEM(wg.shape, BF16),
                        pltpu.VMEM(wp.shape, BF16)] + _weight_staging(d),
        compiler_params=pltpu.CompilerParams(
            dimension_semantics=("arbitrary",), vmem_limit_bytes=VMEM_LIMIT),
        name="out_ple",
    )(x2, merged, p2, ple_g, wout, wg, wp)


def _layer(x2, p2, pos, freq, b, s, norm_g, w_in, q_lat_g, kv_lat_g, w_uq, w_ukv,
           q_norm_g, k_norm_g, conv_w, conv_b, w_branch_attn, w_branch_conv,
           w_out, ple_norm_g, w_ple_gate, w_ple_proj):
    d = x2.shape[1]
    pad_g = lambda g: jnp.pad(g, (0, HEAD_PAD - QK_HEAD_DIM)).reshape(1, HEAD_PAD)

    z, rot = _in_proj(x2, norm_g.reshape(1, d), jnp.swapaxes(w_in, 0, 1), pos, freq,
                      tm=IN_PROJ_TM)
    attn = _mla_attention(z, rot, q_lat_g.reshape(1, -1), kv_lat_g.reshape(1, -1),
                          w_uq, w_ukv, pad_g(q_norm_g), pad_g(k_norm_g),
                          b, s, tq=512, hps=2)
    merged = _mix(attn, z, conv_w, conv_b.reshape(1, -1), w_branch_attn, w_branch_conv,
                  seq=s, tm=512)
    return _out_ple(x2, merged, p2, ple_norm_g.reshape(1, d), w_out, w_ple_gate, w_ple_proj,
                    tm=512)


def kernel(x, p, positions, norm_g, w_in, q_lat_g, kv_lat_g, w_uq, w_ukv, q_norm_g, k_norm_g, conv_w, conv_b, w_branch_attn, w_branch_conv, w_out, ple_norm_g, w_ple_gate, w_ple_proj):
    b, s, d = x.shape
    depth = p.shape[0]
    x2 = x.reshape(b * s, d)
    half = QK_ROPE_DIM // 2
    pos = positions.reshape(b * s // IN_PROJ_TM, ROT_GROUPS, IN_PROJ_TM // ROT_GROUPS)
    pos = jnp.repeat(jnp.swapaxes(pos, 1, 2), half, axis=2).reshape(b * s // ROT_GROUPS, LANES)
    inv_freq = 1.0 / (ROPE_THETA ** (jnp.arange(0, QK_ROPE_DIM, 2, dtype=F32) / QK_ROPE_DIM))
    freq = jnp.tile(inv_freq, ROT_GROUPS).reshape(1, LANES)
    for i in range(depth):
        x2 = _layer(x2, p[i].reshape(b * s, -1), pos, freq, b, s, norm_g[i], w_in[i],
                    q_lat_g[i], kv_lat_g[i], w_uq[i], w_ukv[i], q_norm_g[i], k_norm_g[i],
                    conv_w[i], conv_b[i], w_branch_attn[i], w_branch_conv[i], w_out[i],
                    ple_norm_g[i], w_ple_gate[i], w_ple_proj[i])
    return x2.reshape(b, s, d)
```
